```python
import jax, jax.numpy as jnp
from jax import lax
import numpy as np

D_MODEL = 1024
BATCH = 8
SEQ = 2048
DEPTH = 1

D_MLSTM = D_MODEL
N_MLSTM_HEADS = 4
MLSTM_HEAD_DIM = D_MLSTM // N_MLSTM_HEADS
QKV_BLOCK = 4
N_QKV_BLOCKS = D_MLSTM // QKV_BLOCK
CONV_WIDTH = 4
CHUNK = 64
D_FOX = D_MODEL
N_FOX_HEADS = 8
FOX_HEAD_DIM = D_FOX // N_FOX_HEADS
Q_BLOCK = 128
D_MIX = D_MLSTM + D_FOX
IN_SPLITS = [D_MLSTM, 2 * D_MLSTM, 2 * D_MLSTM + D_FOX, 2 * D_MLSTM + 2 * D_FOX,
             2 * D_MLSTM + 3 * D_FOX, 2 * D_MLSTM + 4 * D_FOX]
D_IN_PROJ = 2 * D_MLSTM + 4 * D_FOX + N_FOX_HEADS
EPS = 1e-6

kernel_name = "hymba_mlstm_fox_hybrid"


def rms_norm(x, g):
    x32 = x.astype(jnp.float32)
    y = x32 * lax.rsqrt(jnp.mean(x32 * x32, axis=-1, keepdims=True) + EPS)
    return y.astype(x.dtype) * g


def head_rms_norm(h, g):
    B, S, H, dh = h.shape
    h32 = h.astype(jnp.float32)
    y = h32 * lax.rsqrt(jnp.mean(h32 * h32, axis=-1, keepdims=True) + EPS)
    return y.reshape(B, S, H * dh).astype(h.dtype) * g


def head_layer_norm(h, g):
    B, S, H, dh = h.shape
    h32 = h.astype(jnp.float32)
    mu = jnp.mean(h32, axis=-1, keepdims=True)
    var = jnp.mean(jnp.square(h32 - mu), axis=-1, keepdims=True)
    y = (h32 - mu) * lax.rsqrt(var + EPS)
    return y.reshape(B, S, H * dh).astype(h.dtype) * g


def causal_depthwise_conv(x, w, b):
    S = x.shape[1]
    xp = jnp.pad(x, ((0, 0), (CONV_WIDTH - 1, 0), (0, 0)))
    y = b
    for tap in range(CONV_WIDTH):
        y = y + xp[:, tap:tap + S, :] * w[tap]
    return y


def headwise_linear(x, w):
    B, S, C = x.shape
    xb = x.reshape(B, S, N_QKV_BLOCKS, QKV_BLOCK)
    return jnp.einsum('bsni,nio->bsno', xb, w).reshape(B, S, C)


def mlstm_chunkwise(q, k, v, ig, lf):
    B, H, S, dh = q.shape
    NC = S // CHUNK
    qc = q.reshape(B, H, NC, CHUNK, dh)
    kc = k.reshape(B, H, NC, CHUNK, dh)
    vc = v.reshape(B, H, NC, CHUNK, dh)
    igc = ig.reshape(B, H, NC, CHUNK)
    bcum = jnp.cumsum(lf.reshape(B, H, NC, CHUNK), axis=-1)
    gtot = bcum[..., -1]
    a = gtot[..., None] - bcum + igc

    def step(carry, xs):
        C, n, m = carry
        qj, kj, vj, aj, gj = xs
        qC = jnp.einsum('bhld,bhde->bhle', qj, C)
        qn = jnp.einsum('bhld,bhd->bhl', qj, n)
        m_new = jnp.maximum(gj + m, jnp.max(aj, axis=-1))
        decay = jnp.exp(gj + m - m_new)
        w = jnp.exp(aj - m_new[..., None])
        C = decay[..., None, None] * C + jnp.einsum('bhl,bhld,bhle->bhde', w, kj, vj)
        n = decay[..., None] * n + jnp.einsum('bhl,bhld->bhd', w, kj)
        return (C, n, m_new), (qC, qn, m)

    init = (jnp.zeros((B, H, dh, dh), jnp.float32),
            jnp.zeros((B, H, dh), jnp.float32),
            jnp.zeros((B, H), jnp.float32))
    xs = (jnp.moveaxis(qc, 2, 0), jnp.moveaxis(kc, 2, 0), jnp.moveaxis(vc, 2, 0),
          jnp.moveaxis(a, 2, 0), jnp.moveaxis(gtot, 2, 0))
    _, (qC, qn, m_prev) = lax.scan(step, init, xs)
    qC = jnp.moveaxis(qC, 0, 2)
    qn = jnp.moveaxis(qn, 0, 2)
    m_prev = jnp.moveaxis(m_prev, 0, 2)

    causal = jnp.tril(jnp.ones((CHUNK, CHUNK), dtype=bool))
    dlog = bcum[..., :, None] - bcum[..., None, :] + igc[..., None, :]
    dlog = jnp.where(causal, dlog, -jnp.inf)
    m_inter = bcum + m_prev[..., None]
    m_t = jnp.maximum(m_inter, jnp.max(dlog, axis=-1))
    w_inter = jnp.exp(m_inter - m_t)
    scores = jnp.einsum('bhcld,bhcsd->bhcls', qc, kc) * jnp.exp(dlog - m_t[..., None])
    num = w_inter[..., None] * qC + jnp.einsum('bhcls,bhcsd->bhcld', scores, vc)
    den = w_inter * qn + jnp.sum(scores, axis=-1)
    h = num / jnp.maximum(jnp.abs(den), jnp.exp(-m_t))[..., None]
    return h.reshape(B, H, S, dh)


def forgetting_attention(q, k, v, lf):
    B, H, S, dh = q.shape
    NB = S // Q_BLOCK
    scale = dh ** -0.5
    c = jnp.cumsum(lf, axis=-1)
    qb = jnp.moveaxis(q.reshape(B, H, NB, Q_BLOCK, dh), 2, 0)
    cb = jnp.moveaxis(c.reshape(B, H, NB, Q_BLOCK), 2, 0)
    starts = jnp.arange(NB, dtype=jnp.int32) * Q_BLOCK
    key_pos = jnp.arange(S, dtype=jnp.int32)

    def one_block(args):
        qi, ci, start = args
        logits = jnp.einsum('bhqd,bhkd->bhqk', qi, k).astype(jnp.float32) * scale
        logits = logits + ci[..., :, None] - c[..., None, :]
        qpos = start + jnp.arange(Q_BLOCK, dtype=jnp.int32)
        mask = key_pos[None, :] <= qpos[:, None]
        logits = jnp.where(mask, logits, -jnp.inf)
        p = jax.nn.softmax(logits, axis=-1)
        return jnp.einsum('bhqk,bhkd->bhqd', p.astype(v.dtype), v)

    out = lax.map(one_block, (qb, cb, starts))
    return jnp.moveaxis(out, 0, 2).reshape(B, H, S, dh)


def setup_inputs(seed: int = 0) -> dict:
    key = jax.random.key(seed)
    ks = jax.random.split(key, 20)
    f32 = jnp.float32
    nrm = lambda k, shape, s: jax.random.normal(k, shape, f32) * s
    return {
        "x": nrm(ks[0], (BATCH, SEQ, D_MODEL), 1.0),
        "norm_gain": 1.0 + nrm(ks[1], (DEPTH, D_MODEL), 0.02),
        "w_in": nrm(ks[2], (DEPTH, D_MODEL, D_IN_PROJ), D_MODEL ** -0.5),
        "conv_w": nrm(ks[3], (DEPTH, CONV_WIDTH, D_MLSTM), CONV_WIDTH ** -0.5),
        "conv_b": nrm(ks[4], (DEPTH, D_MLSTM), 0.02),
        "w_q_m": nrm(ks[5], (DEPTH, N_QKV_BLOCKS, QKV_BLOCK, QKV_BLOCK), QKV_BLOCK ** -0.5),
        "w_k_m": nrm(ks[6], (DEPTH, N_QKV_BLOCKS, QKV_BLOCK, QKV_BLOCK), QKV_BLOCK ** -0.5),
        "w_v_m": nrm(ks[7], (DEPTH, N_QKV_BLOCKS, QKV_BLOCK, QKV_BLOCK), QKV_BLOCK ** -0.5),
        "w_igate": nrm(ks[8], (DEPTH, 3 * D_MLSTM, N_MLSTM_HEADS), (3 * D_MLSTM) ** -0.5),
        "b_igate": nrm(ks[9], (DEPTH, N_MLSTM_HEADS), 0.1),
        "w_fgate": nrm(ks[10], (DEPTH, 3 * D_MLSTM, N_MLSTM_HEADS), (3 * D_MLSTM) ** -0.5),
        "b_fgate": jnp.tile(jnp.linspace(3.0, 6.0, N_MLSTM_HEADS, dtype=f32)[None], (DEPTH, 1))
                   + nrm(ks[11], (DEPTH, N_MLSTM_HEADS), 0.1),
        "mlstm_norm_gain": 1.0 + nrm(ks[12], (DEPTH, D_MLSTM), 0.02),
        "mlstm_skip": 1.0 + nrm(ks[13], (DEPTH, D_MLSTM), 0.02),
        "fox_forget_bias": jnp.tile(jnp.linspace(1.0, 5.0, N_FOX_HEADS, dtype=f32)[None], (DEPTH, 1))
                           + nrm(ks[14], (DEPTH, N_FOX_HEADS), 0.1),
        "fox_norm_gain": 1.0 + nrm(ks[15], (DEPTH, D_FOX), 0.02),
        "w_out": nrm(ks[16], (DEPTH, D_MIX, D_MODEL), D_MIX ** -0.5),
        "final_norm_gain": 1.0 + nrm(ks[17], (D_MODEL,), 0.02),
    }


def reference(x, norm_gain, w_in, conv_w, conv_b, w_q_m, w_k_m, w_v_m, w_igate, b_igate,
              w_fgate, b_fgate, mlstm_norm_gain, mlstm_skip, fox_forget_bias, fox_norm_gain,
              w_out, final_norm_gain):
    B, S, _ = x.shape
    for l in range(DEPTH):
        hn = rms_norm(x, norm_gain[l])
        proj = hn @ w_in[l]
        x_m, z_m, q_f, k_f, v_f, z_f, f_f = jnp.split(proj, IN_SPLITS, axis=-1)

        xc = jax.nn.silu(causal_depthwise_conv(x_m, conv_w[l], conv_b[l]))
        qm = headwise_linear(xc, w_q_m[l])
        km = headwise_linear(xc, w_k_m[l]) * (MLSTM_HEAD_DIM ** -0.5)
        vm = headwise_linear(x_m, w_v_m[l])
        qkv = jnp.concatenate([qm, km, vm], axis=-1)
        ig = (qkv @ w_igate[l] + b_igate[l]).astype(jnp.float32)
        lfm = jax.nn.log_sigmoid((qkv @ w_fgate[l] + b_fgate[l]).astype(jnp.float32))
        to_heads_m = lambda t: t.reshape(B, S, N_MLSTM_HEADS, MLSTM_HEAD_DIM).transpose(0, 2, 1, 3).astype(jnp.float32)
        hm = mlstm_chunkwise(to_heads_m(qm), to_heads_m(km), to_heads_m(vm),
                             ig.transpose(0, 2, 1), lfm.transpose(0, 2, 1))
        hm = hm.transpose(0, 2, 1, 3).astype(x.dtype)
        hm = head_layer_norm(hm, mlstm_norm_gain[l]) + mlstm_skip[l] * xc
        y_m = hm * jax.nn.silu(z_m)

        lff = jax.nn.log_sigmoid(f_f.astype(jnp.float32) + fox_forget_bias[l])
        to_heads_f = lambda t: t.reshape(B, S, N_FOX_HEADS, FOX_HEAD_DIM).transpose(0, 2, 1, 3)
        hf = forgetting_attention(to_heads_f(q_f), to_heads_f(k_f), to_heads_f(v_f),
                                  lff.transpose(0, 2, 1))
        hf = head_rms_norm(hf.transpose(0, 2, 1, 3), fox_norm_gain[l])
        y_f = hf * jax.nn.silu(z_f)

        y = jnp.concatenate([y_m, y_f], axis=-1) @ w_out[l]
        x = x + y
    return rms_norm(x, final_norm_gain)
```

```python
import functools
import math

import jax
import jax.numpy as jnp
from jax import lax
from jax.experimental import pallas as pl
from jax.experimental.pallas import tpu as pltpu

EPS = 1e-6
N_MLSTM_HEADS = 4
N_FOX_HEADS = 8
QKV_BLOCK = 4
CONV_WIDTH = 4

LANES = 128
SUBLANES = 8
VMEM_LIMIT = 56 * 1024 * 1024

IN_PROJ_ROWS = 512
OUT_PROJ_ROWS = 512
MLSTM_CHUNK = 256
FOX_Q = 256
FOX_K = 256

BF16 = jnp.bfloat16
F32 = jnp.float32
NEG_INF = float("-inf")


def _log_sigmoid(z):
    return jnp.minimum(z, 0.0) - jnp.log(1.0 + jnp.exp(-jnp.abs(z)))


def _sigmoid(z):
    return 1.0 / (1.0 + jnp.exp(-z))


def _split3(v):
    hi = v.astype(BF16)
    r = v - hi.astype(F32)
    mid = r.astype(BF16)
    lo = (r - mid.astype(F32)).astype(BF16)
    return hi, mid, lo


def _cumsum_rows(tri, v):
    w = v.shape[1]
    cat = jnp.concatenate(_split3(v), axis=1)
    cs = jnp.dot(tri, cat, preferred_element_type=F32)
    return cs[:, :w] + cs[:, w:2 * w] + cs[:, 2 * w:]


def _tril_mask(n):
    row = lax.broadcasted_iota(jnp.int32, (n, n), 0)
    col = lax.broadcasted_iota(jnp.int32, (n, n), 1)
    return col <= row


def _in_proj_kernel(x_ref, g_ref, w_ref, wf_ref, fb_ref,
                    xm_ref, zm_ref, q_ref, k_ref, v_ref, zf_ref, c_ref,
                    tri_ref, carry_ref, *, tiles_per_seq, q_scale):
    t = pl.program_id(0)
    rows = x_ref.shape[0]
    d = x_ref.shape[1]

    @pl.when(t == 0)
    def _():
        tri_ref[...] = jnp.where(_tril_mask(rows), 1.0, 0.0).astype(BF16)

    @pl.when(t % tiles_per_seq == 0)
    def _():
        carry_ref[...] = jnp.zeros_like(carry_ref)

    x = x_ref[...]
    ms = jnp.mean(x * x, axis=-1, keepdims=True)
    hn = (x * lax.rsqrt(ms + EPS) * g_ref[...]).astype(BF16)

    outs = (xm_ref, zm_ref, q_ref, k_ref, v_ref, zf_ref)
    for idx, o_ref in enumerate(outs):
        y = jnp.dot(hn, w_ref[:, idx * d:(idx + 1) * d], preferred_element_type=F32)
        if o_ref is q_ref:
            y = y * q_scale
        o_ref[...] = y.astype(o_ref.dtype)

    f = jnp.dot(hn, wf_ref[...], preferred_element_type=F32) + fb_ref[...]
    c = _cumsum_rows(tri_ref[...], _log_sigmoid(f)) + carry_ref[...]
    c_ref[...] = c
    carry_ref[...] = c[rows - 1:rows, :]


def _in_proj(x2, gain, w_main, w_f, f_bias, seq_len, q_scale):
    t_rows, d = x2.shape
    rows = IN_PROJ_ROWS
    assert t_rows % rows == 0 and seq_len % rows == 0
    n_main = w_main.shape[1] // d
    assert n_main == 6
    row_spec = pl.BlockSpec((rows, d), lambda t: (t, 0))
    const = lambda shape: pl.BlockSpec(shape, lambda t: (0, 0), pipeline_mode=pl.Buffered(1))
    kern = functools.partial(_in_proj_kernel, tiles_per_seq=seq_len // rows, q_scale=q_scale)
    return pl.pallas_call(
        kern,
        grid=(t_rows // rows,),
        in_specs=[row_spec, const((1, d)), const(w_main.shape), const(w_f.shape), const((1, LANES))],
        out_specs=[row_spec] * 6 + [pl.BlockSpec((rows, LANES), lambda t: (t, 0))],
        out_shape=[jax.ShapeDtypeStruct((t_rows, d), BF16)] * 6
        + [jax.ShapeDtypeStruct((t_rows, LANES), F32)],
        scratch_shapes=[pltpu.VMEM((rows, rows), BF16), pltpu.VMEM((1, LANES), F32)],
        compiler_params=pltpu.CompilerParams(
            dimension_semantics=("arbitrary",), vmem_limit_bytes=VMEM_LIMIT),
        name="in_proj",
    )(x2, gain, w_main, w_f, f_bias)


def _mlstm_kernel(xm_ref, zm_ref, cw_ref, cb_ref, wq_ref, wk_ref, wv_ref, wg_ref, gb_ref,
                  ng_ref, sk_ref, y_ref,
                  xext_ref, c_sc, n_sc, m_sc, tri_ref, *, k_scale):
    ci = pl.program_id(1)
    L = xm_ref.shape[0]
    dh = wq_ref.shape[1]
    n_heads = wq_ref.shape[0]
    halo = SUBLANES

    @pl.when(ci == 0)
    def _():
        c_sc[...] = jnp.zeros_like(c_sc)
        n_sc[...] = jnp.zeros_like(n_sc)
        m_sc[...] = jnp.zeros_like(m_sc)
        xext_ref[0:halo, :] = jnp.zeros((halo, xext_ref.shape[1]), F32)
        tri_ref[...] = jnp.where(_tril_mask(L), 1.0, 0.0).astype(BF16)

    xm = xm_ref[...].astype(F32)
    xext_ref[halo:halo + L, :] = xm
    conv = cb_ref[...]
    for tap in range(CONV_WIDTH):
        off = halo - (CONV_WIDTH - 1) + tap
        conv = conv + xext_ref[off:off + L, :] * cw_ref[tap:tap + 1, :]
    xext_ref[0:halo, :] = xm[L - halo:L, :]
    xc = conv * _sigmoid(conv)

    qs, ks, vs = [], [], []
    gates = gb_ref[...]
    d_m = n_heads * dh
    for h in range(n_heads):
        sl = slice(h * dh, (h + 1) * dh)
        xc_h = xc[:, sl].astype(BF16)
        q = jnp.dot(xc_h, wq_ref[h], preferred_element_type=F32)
        k = jnp.dot(xc_h, wk_ref[h], preferred_element_type=F32) * k_scale
        v = jnp.dot(xm[:, sl].astype(BF16), wv_ref[h], preferred_element_type=F32)
        qb, kb, vb = q.astype(BF16), k.astype(BF16), v.astype(BF16)
        gates = gates + jnp.dot(qb, wg_ref[h * dh:(h + 1) * dh, :], preferred_element_type=F32)
        gates = gates + jnp.dot(kb, wg_ref[d_m + h * dh:d_m + (h + 1) * dh, :],
                                preferred_element_type=F32)
        gates = gates + jnp.dot(vb, wg_ref[2 * d_m + h * dh:2 * d_m + (h + 1) * dh, :],
                                preferred_element_type=F32)
        qs.append((q, qb))
        ks.append((k, kb))
        vs.append(vb)

    ig = gates[:, :LANES]
    bcum = _cumsum_rows(tri_ref[...], _log_sigmoid(gates[:, LANES:]))
    a_col = ig - bcum
    a_rows = a_col.T
    causal = _tril_mask(L)

    for h in range(n_heads):
        sl = slice(h * dh, (h + 1) * dh)
        (q, qb), (k, kb), vb = qs[h], ks[h], vs[h]
        m_prev = m_sc[h][:, 0:1]
        a_row = a_rows[h:h + 1, :]
        a_c = a_col[:, h:h + 1]
        b_c = bcum[:, h:h + 1]

        dm = jnp.where(causal, a_row, NEG_INF)
        m_run = jnp.maximum(m_prev, jnp.max(dm, axis=1, keepdims=True))
        p = jnp.exp(dm - m_run)
        s = lax.dot_general(qb, kb, (((1,), (1,)), ((), ())), preferred_element_type=F32) * p
        w_inter = jnp.exp(m_prev - m_run)
        num = w_inter * jnp.dot(qb, c_sc[h].astype(BF16), preferred_element_type=F32)
        num = num + jnp.dot(s.astype(BF16), vb, preferred_element_type=F32)
        den = w_inter * jnp.sum(q * n_sc[h], axis=1, keepdims=True)
        den = den + jnp.sum(s, axis=1, keepdims=True)
        m_t = b_c + m_run
        hh = num / jnp.maximum(jnp.abs(den), jnp.exp(-m_t))

        m_last = m_run[L - 1:L, :]
        decay = jnp.exp(m_prev - m_last)
        kw = k * jnp.exp(a_c - m_last)
        c_sc[h] = decay * c_sc[h] + lax.dot_general(
            kw.astype(BF16), vb, (((0,), (0,)), ((), ())), preferred_element_type=F32)
        n_sc[h] = decay * n_sc[h] + jnp.sum(kw, axis=0, keepdims=True)
        m_sc[h] = jnp.broadcast_to(m_t[L - 1:L, :], m_sc.shape[1:])

        mu = jnp.mean(hh, axis=1, keepdims=True)
        cen = hh - mu
        var = jnp.mean(cen * cen, axis=1, keepdims=True)
        hn = cen * lax.rsqrt(var + EPS) * ng_ref[:, sl] + sk_ref[:, sl] * xc[:, sl]
        z = zm_ref[:, sl].astype(F32)
        y_ref[:, sl] = (hn * (z * _sigmoid(z))).astype(y_ref.dtype)


def _mlstm(xm, zm, conv_w, conv_b, wq, wk, wv, wg, gb, ngain, skip):
    b, s, d = xm.shape
    L = MLSTM_CHUNK
    assert s % L == 0
    n_heads, dh, _ = wq.shape
    seq_spec = pl.BlockSpec((None, L, d), lambda bi, ci: (bi, ci, 0))

    def const(shape):
        nd = len(shape)
        return pl.BlockSpec(shape, lambda bi, ci: (0,) * nd)

    kern = functools.partial(_mlstm_kernel, k_scale=dh ** -0.5)
    return pl.pallas_call(
        kern,
        grid=(b, s // L),
        in_specs=[seq_spec, seq_spec, const(conv_w.shape), const(conv_b.shape), const(wq.shape),
                  const(wk.shape), const(wv.shape), const(wg.shape), const(gb.shape),
                  const(ngain.shape), const(skip.shape)],
        out_specs=seq_spec,
        out_shape=jax.ShapeDtypeStruct((b, s, d), BF16),
        scratch_shapes=[
            pltpu.VMEM((L + SUBLANES, d), F32),
            pltpu.VMEM((n_heads, dh, dh), F32),
            pltpu.VMEM((n_heads, 1, dh), F32),
            pltpu.VMEM((n_heads, 1, LANES), F32),
            pltpu.VMEM((L, L), BF16),
        ],
        compiler_params=pltpu.CompilerParams(
            dimension_semantics=("arbitrary", "arbitrary"), vmem_limit_bytes=VMEM_LIMIT),
        name="mlstm",
    )(xm, zm, conv_w, conv_b, wq, wk, wv, wg, gb, ngain, skip)


def _fox_kernel(q_ref, k_ref, v_ref, cq_ref, cr_ref, zf_ref, g_ref, o_ref):
    h = pl.program_id(1)
    qi = pl.program_id(2)
    tq, dh = q_ref.shape
    tk = FOX_K
    assert tq == tk

    q = q_ref[...]
    cq_all = cq_ref[...]
    lane = lax.broadcasted_iota(jnp.int32, cq_all.shape, 1)
    cq = jnp.sum(jnp.where(lane == h, cq_all, 0.0), axis=1, keepdims=True)

    def step(j, carry, masked):
        m_prev, l_prev, acc = carry
        start = pl.multiple_of(j * tk, tk)
        k = k_ref[pl.ds(start, tk), :]
        v = v_ref[pl.ds(start, tk), :]
        s = lax.dot_general(q, k, (((1,), (1,)), ((), ())), preferred_element_type=F32)
        s = s + (cq - cr_ref[j])
        if masked:
            s = jnp.where(_tril_mask(tq), s, NEG_INF)
        m_new = jnp.maximum(m_prev, jnp.max(s, axis=1, keepdims=True))
        p = jnp.exp(s - m_new)
        alpha = jnp.exp(m_prev - m_new)
        l_new = alpha * l_prev + jnp.sum(p, axis=1, keepdims=True)
        acc = alpha * acc + jnp.dot(p.astype(BF16), v, preferred_element_type=F32)
        return m_new, l_new, acc

    init = (jnp.full((tq, 1), NEG_INF, F32), jnp.zeros((tq, 1), F32), jnp.zeros((tq, dh), F32))
    carry = lax.fori_loop(0, qi, lambda j, c: step(j, c, False), init)
    _, l_fin, acc = step(qi, carry, True)

    out = acc / l_fin
    ms = jnp.mean(out * out, axis=1, keepdims=True)
    z = zf_ref[...].astype(F32)
    o_ref[...] = (out * lax.rsqrt(ms + EPS) * g_ref[...] * (z * _sigmoid(z))).astype(o_ref.dtype)


def _fox(q, k, v, c_col, c_row, zf, gain):
    b, s, d = q.shape
    n_heads = N_FOX_HEADS
    dh = d // n_heads
    assert dh == LANES and s % FOX_Q == 0
    nk = s // FOX_K
    q_spec = pl.BlockSpec((None, FOX_Q, dh), lambda bi, hi, qi: (bi, qi, hi))
    kv_spec = pl.BlockSpec((None, s, dh), lambda bi, hi, qi: (bi, 0, hi))
    return pl.pallas_call(
        _fox_kernel,
        grid=(b, n_heads, s // FOX_Q),
        in_specs=[q_spec, kv_spec, kv_spec,
                  pl.BlockSpec((None, FOX_Q, n_heads), lambda bi, hi, qi: (bi, qi, 0)),
                  pl.BlockSpec((None, None, nk, 1, FOX_K), lambda bi, hi, qi: (bi, hi, 0, 0, 0)),
                  q_spec,
                  pl.BlockSpec((1, dh), lambda bi, hi, qi: (0, hi))],
        out_specs=q_spec,
        out_shape=jax.ShapeDtypeStruct((b, s, d), BF16),
        compiler_params=pltpu.CompilerParams(
            dimension_semantics=("arbitrary", "arbitrary", "arbitrary"),
            vmem_limit_bytes=VMEM_LIMIT),
        name="fox",
    )(q, k, v, c_col, c_row, zf, gain)


def _out_proj_kernel(ym_ref, yf_ref, x_ref, wm_ref, wf_ref, g_ref, o_ref, *, final_norm):
    y = jnp.dot(ym_ref[...], wm_ref[...], preferred_element_type=F32)
    y = y + jnp.dot(yf_ref[...], wf_ref[...], preferred_element_type=F32)
    r = x_ref[...] + y
    if final_norm:
        ms = jnp.mean(r * r, axis=-1, keepdims=True)
        r = r * lax.rsqrt(ms + EPS) * g_ref[...]
    o_ref[...] = r


def _out_proj(ym, yf, x2, w_m, w_f, gain, final_norm):
    t_rows, d = x2.shape
    rows = OUT_PROJ_ROWS
    assert t_rows % rows == 0
    row = lambda width: pl.BlockSpec((rows, width), lambda t: (t, 0))
    const = lambda shape: pl.BlockSpec(shape, lambda t: (0, 0))
    kern = functools.partial(_out_proj_kernel, final_norm=final_norm)
    return pl.pallas_call(
        kern,
        grid=(t_rows // rows,),
        in_specs=[row(ym.shape[1]), row(yf.shape[1]), row(d), const(w_m.shape), const(w_f.shape),
                  const((1, d))],
        out_specs=row(d),
        out_shape=jax.ShapeDtypeStruct((t_rows, d), F32),
        compiler_params=pltpu.CompilerParams(
            dimension_semantics=("arbitrary",), vmem_limit_bytes=VMEM_LIMIT),
        name="out_proj",
    )(ym, yf, x2, w_m, w_f, gain)


def _block_diag(w, dh):
    nb, bs, _ = w.shape
    per = dh // bs
    w4 = w.reshape(nb // per, per, bs, bs)
    eye = jnp.eye(per, dtype=w.dtype)
    dense = jnp.einsum('hnio,nm->hnimo', w4, eye)
    return dense.reshape(nb // per, dh, dh)


def kernel(x, norm_gain, w_in, conv_w, conv_b, w_q_m, w_k_m, w_v_m, w_igate, b_igate, w_fgate,
           b_fgate, mlstm_norm_gain, mlstm_skip, fox_forget_bias, fox_norm_gain, w_out,
           final_norm_gain):
    b, s, d = x.shape
    depth = norm_gain.shape[0]
    d_m = conv_w.shape[-1]
    d_f = fox_norm_gain.shape[-1]
    dh_m = d_m // N_MLSTM_HEADS
    dh_f = d_f // N_FOX_HEADS
    n_main = 2 * d_m + 4 * d_f
    assert d_m == d and d_f == d

    x2 = x.reshape(b * s, d)
    for l in range(depth):
        w_main = w_in[l][:, :n_main].astype(BF16)
        w_f = jnp.pad(w_in[l][:, n_main:], ((0, 0), (0, LANES - N_FOX_HEADS))).astype(BF16)
        f_bias = jnp.pad(fox_forget_bias[l], (0, LANES - N_FOX_HEADS)).reshape(1, LANES)
        xm, zm, qf, kf, vf, zf, c = _in_proj(x2, norm_gain[l].reshape(1, d), w_main, w_f, f_bias,
                                              s, dh_f ** -0.5)

        wg = jnp.zeros((3 * d_m, 2 * LANES), F32)
        wg = wg.at[:, :N_MLSTM_HEADS].set(w_igate[l]).at[:, LANES:LANES + N_MLSTM_HEADS].set(w_fgate[l])
        gb = jnp.zeros((1, 2 * LANES), F32)
        gb = gb.at[0, :N_MLSTM_HEADS].set(b_igate[l]).at[0, LANES:LANES + N_MLSTM_HEADS].set(b_fgate[l])
        y_m = _mlstm(xm.reshape(b, s, d_m), zm.reshape(b, s, d_m), conv_w[l],
                     conv_b[l].reshape(1, d_m),
                     _block_diag(w_q_m[l], dh_m).astype(BF16),
                     _block_diag(w_k_m[l], dh_m).astype(BF16),
                     _block_diag(w_v_m[l], dh_m).astype(BF16),
                     wg.astype(BF16), gb,
                     mlstm_norm_gain[l].reshape(1, d_m), mlstm_skip[l].reshape(1, d_m))

        c8 = c.reshape(b, s, LANES)[:, :, :N_FOX_HEADS]
        c_row = jnp.transpose(c8, (0, 2, 1)).reshape(b, N_FOX_HEADS, s // FOX_K, 1, FOX_K)
        y_f = _fox(qf.reshape(b, s, d_f), kf.reshape(b, s, d_f), vf.reshape(b, s, d_f), c8, c_row,
                   zf.reshape(b, s, d_f), fox_norm_gain[l].reshape(1, d_f))

        w_o = w_out[l].astype(BF16)
        x2 = _out_proj(y_m.reshape(b * s, d_m), y_f.reshape(b * s, d_f), x2, w_o[:d_m], w_o[d_m:],
                       final_norm_gain.reshape(1, d), final_norm=(l == depth - 1))
    return x2.reshape(b, s, d)
```

```python
import functools
import math

import jax
import jax.numpy as jnp
from jax import lax
from jax.experimental import pallas as pl
from jax.experimental.pallas import tpu as pltpu

EPS = 1e-6
N_MLSTM_HEADS = 4
N_FOX_HEADS = 8
QKV_BLOCK = 4
CONV_WIDTH = 4

LANES = 128
SUBLANES = 8
VMEM_LIMIT = 56 * 1024 * 1024

IN_PROJ_ROWS = 512
OUT_PROJ_ROWS = 512
MLSTM_CHUNK = 256
FOX_Q = 256
LOG2E = math.log2(math.e)

BF16 = jnp.bfloat16
F32 = jnp.float32
NEG_INF = float("-inf")


def _log_sigmoid(z):
    return jnp.minimum(z, 0.0) - jnp.log(1.0 + jnp.exp(-jnp.abs(z)))


def _sigmoid(z):
    return 1.0 / (1.0 + jnp.exp(-z))


def _split3(v):
    hi = v.astype(BF16)
    r = v - hi.astype(F32)
    mid = r.astype(BF16)
    lo = (r - mid.astype(F32)).astype(BF16)
    return hi, mid, lo


def _cumsum_rows(tri, v):
    w = v.shape[1]
    cat = jnp.concatenate(_split3(v), axis=1)
    cs = jnp.dot(tri, cat, preferred_element_type=F32)
    return cs[:, :w] + cs[:, w:2 * w] + cs[:, 2 * w:]


def _tril_mask(n):
    row = lax.broadcasted_iota(jnp.int32, (n, n), 0)
    col = lax.broadcasted_iota(jnp.int32, (n, n), 1)
    return col <= row


def _in_proj_kernel(x_ref, g_ref, w_ref, wf_ref, fb_ref,
                    xm_ref, zm_ref, q_ref, k_ref, v_ref, zf_ref, c_ref,
                    tri_ref, carry_ref, *, tiles_per_seq, q_scale):
    t = pl.program_id(0)
    rows = x_ref.shape[0]
    d = x_ref.shape[1]

    @pl.when(t == 0)
    def _():
        tri_ref[...] = jnp.where(_tril_mask(rows), 1.0, 0.0).astype(BF16)

    @pl.when(t % tiles_per_seq == 0)
    def _():
        carry_ref[...] = jnp.zeros_like(carry_ref)

    x = x_ref[...]
    ms = jnp.mean(x * x, axis=-1, keepdims=True)
    hn = (x * lax.rsqrt(ms + EPS) * g_ref[...]).astype(BF16)

    outs = (xm_ref, zm_ref, q_ref, k_ref, v_ref, zf_ref)
    for idx, o_ref in enumerate(outs):
        y = jnp.dot(hn, w_ref[:, idx * d:(idx + 1) * d], preferred_element_type=F32)
        if o_ref is q_ref:
            y = y * q_scale
        o_ref[...] = y.astype(o_ref.dtype)

    f = jnp.dot(hn, wf_ref[...], preferred_element_type=F32) + fb_ref[...]
    c = _cumsum_rows(tri_ref[...], _log_sigmoid(f)) + carry_ref[...]
    c_ref[...] = c * LOG2E
    carry_ref[...] = c[rows - 1:rows, :]


def _in_proj(x2, gain, w_main, w_f, f_bias, seq_len, q_scale):
    t_rows, d = x2.shape
    rows = IN_PROJ_ROWS
    assert t_rows % rows == 0 and seq_len % rows == 0
    n_main = w_main.shape[1] // d
    assert n_main == 6
    row_spec = pl.BlockSpec((rows, d), lambda t: (t, 0))
    const = lambda shape: pl.BlockSpec(shape, lambda t: (0, 0), pipeline_mode=pl.Buffered(1))
    kern = functools.partial(_in_proj_kernel, tiles_per_seq=seq_len // rows, q_scale=q_scale)
    return pl.pallas_call(
        kern,
        grid=(t_rows // rows,),
        in_specs=[row_spec, const((1, d)), const(w_main.shape), const(w_f.shape), const((1, LANES))],
        out_specs=[row_spec] * 6 + [pl.BlockSpec((rows, LANES), lambda t: (t, 0))],
        out_shape=[jax.ShapeDtypeStruct((t_rows, d), BF16)] * 6
        + [jax.ShapeDtypeStruct((t_rows, LANES), F32)],
        scratch_shapes=[pltpu.VMEM((rows, rows), BF16), pltpu.VMEM((1, LANES), F32)],
        compiler_params=pltpu.CompilerParams(
            dimension_semantics=("arbitrary",), vmem_limit_bytes=VMEM_LIMIT),
        name="in_proj",
    )(x2, gain, w_main, w_f, f_bias)


def _mlstm_kernel(xm_ref, zm_ref, cw_ref, cb_ref, wq_ref, wk_ref, wv_ref, wg_ref, gb_ref,
                  ng_ref, sk_ref, y_ref,
                  xext_ref, c_sc, n_sc, m_sc, tri_ref, *, k_scale):
    ci = pl.program_id(1)
    L = xm_ref.shape[0]
    dh = wq_ref.shape[1]
    n_heads = wq_ref.shape[0]
    halo = SUBLANES

    @pl.when(ci == 0)
    def _():
        c_sc[...] = jnp.zeros_like(c_sc)
        n_sc[...] = jnp.zeros_like(n_sc)
        m_sc[...] = jnp.zeros_like(m_sc)
        xext_ref[0:halo, :] = jnp.zeros((halo, xext_ref.shape[1]), F32)
        tri_ref[...] = jnp.where(_tril_mask(L), 1.0, 0.0).astype(BF16)

    xm = xm_ref[...].astype(F32)
    xext_ref[halo:halo + L, :] = xm
    conv = cb_ref[...]
    for tap in range(CONV_WIDTH):
        off = halo - (CONV_WIDTH - 1) + tap
        conv = conv + xext_ref[off:off + L, :] * cw_ref[tap:tap + 1, :]
    xext_ref[0:halo, :] = xm[L - halo:L, :]
    xc = conv * _sigmoid(conv)

    qs, ks, vs = [], [], []
    gates = gb_ref[...]
    d_m = n_heads * dh
    for h in range(n_heads):
        sl = slice(h * dh, (h + 1) * dh)
        xc_h = xc[:, sl].astype(BF16)
        q = jnp.dot(xc_h, wq_ref[h], preferred_element_type=F32)
        k = jnp.dot(xc_h, wk_ref[h], preferred_element_type=F32) * k_scale
        v = jnp.dot(xm_ref[:, sl], wv_ref[h], preferred_element_type=F32)
        qb, kb, vb = q.astype(BF16), k.astype(BF16), v.astype(BF16)
        gates = gates + jnp.dot(qb, wg_ref[h * dh:(h + 1) * dh, :], preferred_element_type=F32)
        gates = gates + jnp.dot(kb, wg_ref[d_m + h * dh:d_m + (h + 1) * dh, :],
                                preferred_element_type=F32)
        gates = gates + jnp.dot(vb, wg_ref[2 * d_m + h * dh:2 * d_m + (h + 1) * dh, :],
                                preferred_element_type=F32)
        qs.append((q, qb))
        ks.append((k, kb))
        vs.append(vb)

    ig = gates[:, :LANES]
    bcum = _cumsum_rows(tri_ref[...], _log_sigmoid(gates[:, LANES:]))
    a_col = ig - bcum
    a_rows = a_col.T
    causal = _tril_mask(L)

    for h in range(n_heads):
        sl = slice(h * dh, (h + 1) * dh)
        (q, qb), (k, kb), vb = qs[h], ks[h], vs[h]
        m_prev = m_sc[h][:, 0:1]
        a_row = a_rows[h:h + 1, :]
        a_c = a_col[:, h:h + 1]
        b_c = bcum[:, h:h + 1]

        dm = jnp.where(causal, a_row, NEG_INF)
        m_run = jnp.maximum(m_prev, jnp.max(dm, axis=1, keepdims=True))
        p = jnp.exp(dm - m_run)
        s = lax.dot_general(qb, kb, (((1,), (1,)), ((), ())), preferred_element_type=F32) * p
        w_inter = jnp.exp(m_prev - m_run)
        num = w_inter * jnp.dot(qb, c_sc[h].astype(BF16), preferred_element_type=F32)
        num = num + jnp.dot(s.astype(BF16), vb, preferred_element_type=F32)
        den = w_inter * jnp.sum(q * n_sc[h], axis=1, keepdims=True)
        den = den + jnp.sum(s, axis=1, keepdims=True)
        m_t = b_c + m_run
        hh = num / jnp.maximum(jnp.abs(den), jnp.exp(-m_t))

        m_last = m_run[L - 1:L, :]
        decay = jnp.exp(m_prev - m_last)
        kw = k * jnp.exp(a_c - m_last)
        c_sc[h] = decay * c_sc[h] + lax.dot_general(
            kw.astype(BF16), vb, (((0,), (0,)), ((), ())), preferred_element_type=F32)
        n_sc[h] = decay * n_sc[h] + jnp.sum(kw, axis=0, keepdims=True)
        m_sc[h] = jnp.broadcast_to(m_t[L - 1:L, :], m_sc.shape[1:])

        mu = jnp.mean(hh, axis=1, keepdims=True)
        cen = hh - mu
        var = jnp.mean(cen * cen, axis=1, keepdims=True)
        hn = cen * lax.rsqrt(var + EPS) * ng_ref[:, sl] + sk_ref[:, sl] * xc[:, sl]
        z = zm_ref[:, sl].astype(F32)
        y_ref[:, sl] = (hn * (z * _sigmoid(z))).astype(y_ref.dtype)


def _mlstm(xm, zm, conv_w, conv_b, wq, wk, wv, wg, gb, ngain, skip):
    b, s, d = xm.shape
    L = MLSTM_CHUNK
    assert s % L == 0
    n_heads, dh, _ = wq.shape
    seq_spec = pl.BlockSpec((None, L, d), lambda bi, ci: (bi, ci, 0))

    def const(shape):
        nd = len(shape)
        return pl.BlockSpec(shape, lambda bi, ci: (0,) * nd)

    kern = functools.partial(_mlstm_kernel, k_scale=dh ** -0.5)
    return pl.pallas_call(
        kern,
        grid=(b, s // L),
        in_specs=[seq_spec, seq_spec, const(conv_w.shape), const(conv_b.shape), const(wq.shape),
                  const(wk.shape), const(wv.shape), const(wg.shape), const(gb.shape),
                  const(ngain.shape), const(skip.shape)],
        out_specs=seq_spec,
        out_shape=jax.ShapeDtypeStruct((b, s, d), BF16),
        scratch_shapes=[
            pltpu.VMEM((L + SUBLANES, d), F32),
            pltpu.VMEM((n_heads, dh, dh), F32),
            pltpu.VMEM((n_heads, 1, dh), F32),
            pltpu.VMEM((n_heads, 1, LANES), F32),
            pltpu.VMEM((L, L), BF16),
        ],
        compiler_params=pltpu.CompilerParams(
            dimension_semantics=("arbitrary", "arbitrary"), vmem_limit_bytes=VMEM_LIMIT),
        name="mlstm",
    )(xm, zm, conv_w, conv_b, wq, wk, wv, wg, gb, ngain, skip)


def _fox_kernel(q_ref, k_ref, v_ref, cq_ref, cr_ref, zf_ref, g_ref, o_ref):
    h = pl.program_id(1)
    s_len, dh = q_ref.shape
    tq = FOX_Q
    lane = lax.broadcasted_iota(jnp.int32, (tq, cq_ref.shape[1]), 1)
    causal = _tril_mask(tq)
    gain = g_ref[...]

    for qi in range(s_len // tq):
        r0 = qi * tq
        kv = r0 + tq
        q = q_ref[r0:kv, :]
        cq = jnp.sum(jnp.where(lane == h, cq_ref[r0:kv, :], 0.0), axis=1, keepdims=True)
        s = lax.dot_general(q, k_ref[0:kv, :], (((1,), (1,)), ((), ())),
                            preferred_element_type=F32)
        s = s + (cq - cr_ref[:, 0:kv])
        s_diag = jnp.where(causal, s[:, r0:], NEG_INF)
        s = s_diag if qi == 0 else jnp.concatenate([s[:, :r0], s_diag], axis=1)
        m = jnp.max(s, axis=1, keepdims=True)
        p = jnp.exp2(s - m)
        l = jnp.sum(p, axis=1, keepdims=True)
        acc = jnp.dot(p.astype(BF16), v_ref[0:kv, :], preferred_element_type=F32)

        out = acc / l
        ms = jnp.mean(out * out, axis=1, keepdims=True)
        z = zf_ref[r0:kv, :].astype(F32)
        o_ref[r0:kv, :] = (out * lax.rsqrt(ms + EPS) * gain * (z * _sigmoid(z))).astype(o_ref.dtype)


def _fox(q, k, v, c_col, c_row, zf, gain):
    b, s, d = q.shape
    n_heads = N_FOX_HEADS
    dh = d // n_heads
    assert dh == LANES and s % FOX_Q == 0
    head_spec = pl.BlockSpec((None, s, dh), lambda bi, hi: (bi, 0, hi))
    return pl.pallas_call(
        _fox_kernel,
        grid=(b, n_heads),
        in_specs=[head_spec, head_spec, head_spec,
                  pl.BlockSpec((None, s, n_heads), lambda bi, hi: (bi, 0, 0)),
                  pl.BlockSpec((None, None, 1, s), lambda bi, hi: (bi, hi, 0, 0)),
                  head_spec,
                  pl.BlockSpec((1, dh), lambda bi, hi: (0, hi))],
        out_specs=head_spec,
        out_shape=jax.ShapeDtypeStruct((b, s, d), BF16),
        compiler_params=pltpu.CompilerParams(
            dimension_semantics=("arbitrary", "arbitrary"), vmem_limit_bytes=VMEM_LIMIT),
        name="fox",
    )(q, k, v, c_col, c_row, zf, gain)


def _out_proj_kernel(ym_ref, yf_ref, x_ref, wm_ref, wf_ref, g_ref, o_ref, *, final_norm):
    y = jnp.dot(ym_ref[...], wm_ref[...], preferred_element_type=F32)
    y = y + jnp.dot(yf_ref[...], wf_ref[...], preferred_element_type=F32)
    r = x_ref[...] + y
    if final_norm:
        ms = jnp.mean(r * r, axis=-1, keepdims=True)
        r = r * lax.rsqrt(ms + EPS) * g_ref[...]
    o_ref[...] = r


def _out_proj(ym, yf, x2, w_m, w_f, gain, final_norm):
    t_rows, d = x2.shape
    rows = OUT_PROJ_ROWS
    assert t_rows % rows == 0
    row = lambda width: pl.BlockSpec((rows, width), lambda t: (t, 0))
    const = lambda shape: pl.BlockSpec(shape, lambda t: (0, 0))
    kern = functools.partial(_out_proj_kernel, final_norm=final_norm)
    return pl.pallas_call(
        kern,
        grid=(t_rows // rows,),
        in_specs=[row(ym.shape[1]), row(yf.shape[1]), row(d), const(w_m.shape), const(w_f.shape),
                  const((1, d))],
        out_specs=row(d),
        out_shape=jax.ShapeDtypeStruct((t_rows, d), F32),
        compiler_params=pltpu.CompilerParams(
            dimension_semantics=("arbitrary",), vmem_limit_bytes=VMEM_LIMIT),
        name="out_proj",
    )(ym, yf, x2, w_m, w_f, gain)


def _block_diag(w, dh):
    nb, bs, _ = w.shape
    d = nb * bs
    tiled = jnp.broadcast_to(w.reshape(d, 1, bs), (d, dh // bs, bs)).reshape(d, dh)
    r = lax.broadcasted_iota(jnp.int32, (d, dh), 0)
    c = lax.broadcasted_iota(jnp.int32, (d, dh), 1)
    keep = (r % dh) // bs == c // bs
    return jnp.where(keep, tiled, 0.0).reshape(d // dh, dh, dh)


def kernel(x, norm_gain, w_in, conv_w, conv_b, w_q_m, w_k_m, w_v_m, w_igate, b_igate, w_fgate,
           b_fgate, mlstm_norm_gain, mlstm_skip, fox_forget_bias, fox_norm_gain, w_out,
           final_norm_gain):
    b, s, d = x.shape
    depth = norm_gain.shape[0]
    d_m = conv_w.shape[-1]
    d_f = fox_norm_gain.shape[-1]
    dh_m = d_m // N_MLSTM_HEADS
    dh_f = d_f // N_FOX_HEADS
    n_main = 2 * d_m + 4 * d_f
    assert d_m == d and d_f == d

    x2 = x.reshape(b * s, d)
    for l in range(depth):
        w_main = w_in[l][:, :n_main].astype(BF16)
        w_f = jnp.pad(w_in[l][:, n_main:], ((0, 0), (0, LANES - N_FOX_HEADS))).astype(BF16)
        f_bias = jnp.pad(fox_forget_bias[l], (0, LANES - N_FOX_HEADS)).reshape(1, LANES)
        xm, zm, qf, kf, vf, zf, c = _in_proj(x2, norm_gain[l].reshape(1, d), w_main, w_f, f_bias,
                                              s, dh_f ** -0.5 * LOG2E)

        lane_pad = ((0, 0), (0, LANES - N_MLSTM_HEADS))
        wg = jnp.concatenate([jnp.pad(w_igate[l], lane_pad), jnp.pad(w_fgate[l], lane_pad)], axis=1)
        gb = jnp.concatenate([jnp.pad(b_igate[l][None], lane_pad),
                              jnp.pad(b_fgate[l][None], lane_pad)], axis=1)
        y_m = _mlstm(xm.reshape(b, s, d_m), zm.reshape(b, s, d_m), conv_w[l],
                     conv_b[l].reshape(1, d_m),
                     _block_diag(w_q_m[l], dh_m).astype(BF16),
                     _block_diag(w_k_m[l], dh_m).astype(BF16),
                     _block_diag(w_v_m[l], dh_m).astype(BF16),
                     wg.astype(BF16), gb,
                     mlstm_norm_gain[l].reshape(1, d_m), mlstm_skip[l].reshape(1, d_m))

        c8 = c.reshape(b, s, LANES)[:, :, :N_FOX_HEADS]
        c_row = jnp.transpose(c8, (0, 2, 1)).reshape(b, N_FOX_HEADS, 1, s)
        y_f = _fox(qf.reshape(b, s, d_f), kf.reshape(b, s, d_f), vf.reshape(b, s, d_f), c8, c_row,
                   zf.reshape(b, s, d_f), fox_norm_gain[l].reshape(1, d_f))

        w_o = w_out[l].astype(BF16)
        x2 = _out_proj(y_m.reshape(b * s, d_m), y_f.reshape(b * s, d_f), x2, w_o[:d_m], w_o[d_m:],
                       final_norm_gain.reshape(1, d), final_norm=(l == depth - 1))
    return x2.reshape(b, s, d)
```

```python
import functools
import math

import jax
import jax.numpy as jnp
from jax import lax
from jax.experimental import pallas as pl
from jax.experimental.pallas import tpu as pltpu

EPS = 1e-6
N_MLSTM_HEADS = 4
N_FOX_HEADS = 8
QKV_BLOCK = 4
CONV_WIDTH = 4

LANES = 128
SUBLANES = 8
VMEM_LIMIT = 56 * 1024 * 1024

IN_PROJ_ROWS = 512
OUT_PROJ_ROWS = 512
MLSTM_CHUNK = 256
FOX_Q = 256
LOG2E = math.log2(math.e)

BF16 = jnp.bfloat16
F32 = jnp.float32
NEG_INF = float("-inf")


def _log_sigmoid(z):
    return jnp.minimum(z, 0.0) - jnp.log(1.0 + jnp.exp(-jnp.abs(z)))


def _sigmoid(z):
    return 1.0 / (1.0 + jnp.exp(-z))


def _split3(v):
    hi = v.astype(BF16)
    r = v - hi.astype(F32)
    mid = r.astype(BF16)
    lo = (r - mid.astype(F32)).astype(BF16)
    return hi, mid, lo


def _cumsum_rows(tri, v):
    w = v.shape[1]
    cat = jnp.concatenate(_split3(v), axis=1)
    cs = jnp.dot(tri, cat, preferred_element_type=F32)
    return cs[:, :w] + cs[:, w:2 * w] + cs[:, 2 * w:]


def _tril_mask(n):
    row = lax.broadcasted_iota(jnp.int32, (n, n), 0)
    col = lax.broadcasted_iota(jnp.int32, (n, n), 1)
    return col <= row


def _in_proj_kernel(x_ref, g_ref, w_ref, wf_ref, fb_ref,
                    xm_ref, zm_ref, q_ref, k_ref, v_ref, zf_ref, c_ref,
                    tri_ref, carry_ref, *, tiles_per_seq, q_scale):
    t = pl.program_id(0)
    rows = x_ref.shape[0]
    d = x_ref.shape[1]

    @pl.when(t == 0)
    def _():
        tri_ref[...] = jnp.where(_tril_mask(rows), 1.0, 0.0).astype(BF16)

    @pl.when(t % tiles_per_seq == 0)
    def _():
        carry_ref[...] = jnp.zeros_like(carry_ref)

    x = x_ref[...]
    ms = jnp.mean(x * x, axis=-1, keepdims=True)
    hn = (x * lax.rsqrt(ms + EPS) * g_ref[...]).astype(BF16)

    outs = (xm_ref, zm_ref, q_ref, k_ref, v_ref, zf_ref)
    for idx, o_ref in enumerate(outs):
        y = jnp.dot(hn, w_ref[:, idx * d:(idx + 1) * d], preferred_element_type=F32)
        if o_ref is q_ref:
            y = y * q_scale
        o_ref[...] = y.astype(o_ref.dtype)

    f = jnp.dot(hn, wf_ref[...], preferred_element_type=F32) + fb_ref[...]
    c = _cumsum_rows(tri_ref[...], _log_sigmoid(f)) + carry_ref[...]
    c_ref[...] = c * LOG2E
    carry_ref[...] = c[rows - 1:rows, :]


def _in_proj(x2, gain, w_main, w_f, f_bias, seq_len, q_scale):
    t_rows, d = x2.shape
    rows = IN_PROJ_ROWS
    assert t_rows % rows == 0 and seq_len % rows == 0
    n_main = w_main.shape[1] // d
    assert n_main == 6
    row_spec = pl.BlockSpec((rows, d), lambda t: (t, 0))
    const = lambda shape: pl.BlockSpec(shape, lambda t: (0, 0), pipeline_mode=pl.Buffered(1))
    kern = functools.partial(_in_proj_kernel, tiles_per_seq=seq_len // rows, q_scale=q_scale)
    return pl.pallas_call(
        kern,
        grid=(t_rows // rows,),
        in_specs=[row_spec, const((1, d)), const(w_main.shape), const(w_f.shape), const((1, LANES))],
        out_specs=[row_spec] * 6 + [pl.BlockSpec((rows, LANES), lambda t: (t, 0))],
        out_shape=[jax.ShapeDtypeStruct((t_rows, d), BF16)] * 6
        + [jax.ShapeDtypeStruct((t_rows, LANES), F32)],
        scratch_shapes=[pltpu.VMEM((rows, rows), BF16), pltpu.VMEM((1, LANES), F32)],
        compiler_params=pltpu.CompilerParams(
            dimension_semantics=("arbitrary",), vmem_limit_bytes=VMEM_LIMIT),
        name="in_proj",
    )(x2, gain, w_main, w_f, f_bias)


def _mlstm_kernel(xm_ref, zm_ref, cw_ref, cb_ref, wq_ref, wk_ref, wv_ref, wg_ref, gb_ref,
                  ng_ref, sk_ref, y_ref,
                  xext_ref, c_sc, n_sc, m_sc, tri_ref, *, k_scale):
    ci = pl.program_id(1)
    L = xm_ref.shape[0]
    dh = wq_ref.shape[1]
    n_heads = wq_ref.shape[0]
    halo = SUBLANES

    @pl.when(ci == 0)
    def _():
        c_sc[...] = jnp.zeros_like(c_sc)
        n_sc[...] = jnp.zeros_like(n_sc)
        m_sc[...] = jnp.zeros_like(m_sc)
        xext_ref[0:halo, :] = jnp.zeros((halo, xext_ref.shape[1]), F32)
        tri_ref[...] = jnp.where(_tril_mask(L), 1.0, 0.0).astype(BF16)

    xm = xm_ref[...].astype(F32)
    xext_ref[halo:halo + L, :] = xm
    conv = cb_ref[...]
    for tap in range(CONV_WIDTH):
        off = halo - (CONV_WIDTH - 1) + tap
        conv = conv + xext_ref[off:off + L, :] * cw_ref[tap:tap + 1, :]
    xext_ref[0:halo, :] = xm[L - halo:L, :]
    xc = conv * _sigmoid(conv)

    qs, ks, vs = [], [], []
    gates = gb_ref[...]
    d_m = n_heads * dh
    for h in range(n_heads):
        sl = slice(h * dh, (h + 1) * dh)
        xc_h = xc[:, sl].astype(BF16)
        q = jnp.dot(xc_h, wq_ref[h], preferred_element_type=F32)
        k = jnp.dot(xc_h, wk_ref[h], preferred_element_type=F32) * k_scale
        v = jnp.dot(xm_ref[:, sl], wv_ref[h], preferred_element_type=F32)
        qb, kb, vb = q.astype(BF16), k.astype(BF16), v.astype(BF16)
        gates = gates + jnp.dot(qb, wg_ref[h * dh:(h + 1) * dh, :], preferred_element_type=F32)
        gates = gates + jnp.dot(kb, wg_ref[d_m + h * dh:d_m + (h + 1) * dh, :],
                                preferred_element_type=F32)
        gates = gates + jnp.dot(vb, wg_ref[2 * d_m + h * dh:2 * d_m + (h + 1) * dh, :],
                                preferred_element_type=F32)
        qs.append((q, qb))
        ks.append((k, kb))
        vs.append(vb)

    ig = gates[:, :LANES]
    bcum = _cumsum_rows(tri_ref[...], _log_sigmoid(gates[:, LANES:]))
    a_col = ig - bcum
    a_rows = a_col.T
    causal = _tril_mask(L)

    for h in range(n_heads):
        sl = slice(h * dh, (h + 1) * dh)
        (q, qb), (k, kb), vb = qs[h], ks[h], vs[h]
        m_prev = m_sc[h][:, 0:1]
        a_row = a_rows[h:h + 1, :]
        a_c = a_col[:, h:h + 1]
        b_c = bcum[:, h:h + 1]

        dm = jnp.where(causal, a_row, NEG_INF)
        m_run = jnp.maximum(m_prev, jnp.max(dm, axis=1, keepdims=True))
        p = jnp.exp(dm - m_run)
        s = lax.dot_general(qb, kb, (((1,), (1,)), ((), ())), preferred_element_type=F32) * p
        w_inter = jnp.exp(m_prev - m_run)
        num = w_inter * jnp.dot(qb, c_sc[h].astype(BF16), preferred_element_type=F32)
        num = num + jnp.dot(s.astype(BF16), vb, preferred_element_type=F32)
        den = w_inter * jnp.sum(q * n_sc[h], axis=1, keepdims=True)
        den = den + jnp.sum(s, axis=1, keepdims=True)
        m_t = b_c + m_run
        hh = num / jnp.maximum(jnp.abs(den), jnp.exp(-m_t))

        m_last = m_run[L - 1:L, :]
        decay = jnp.exp(m_prev - m_last)
        kw = k * jnp.exp(a_c - m_last)
        c_sc[h] = decay * c_sc[h] + lax.dot_general(
            kw.astype(BF16), vb, (((0,), (0,)), ((), ())), preferred_element_type=F32)
        n_sc[h] = decay * n_sc[h] + jnp.sum(kw, axis=0, keepdims=True)
        m_sc[h] = jnp.broadcast_to(m_t[L - 1:L, :], m_sc.shape[1:])

        mu = jnp.mean(hh, axis=1, keepdims=True)
        cen = hh - mu
        var = jnp.mean(cen * cen, axis=1, keepdims=True)
        hn = cen * lax.rsqrt(var + EPS) * ng_ref[:, sl] + sk_ref[:, sl] * xc[:, sl]
        z = zm_ref[:, sl].astype(F32)
        y_ref[:, sl] = (hn * (z * _sigmoid(z))).astype(y_ref.dtype)


def _mlstm(xm, zm, conv_w, conv_b, wq, wk, wv, wg, gb, ngain, skip):
    b, s, d = xm.shape
    L = MLSTM_CHUNK
    assert s % L == 0
    n_heads, dh, _ = wq.shape
    seq_spec = pl.BlockSpec((None, L, d), lambda bi, ci: (bi, ci, 0))

    def const(shape):
        nd = len(shape)
        return pl.BlockSpec(shape, lambda bi, ci: (0,) * nd)

    kern = functools.partial(_mlstm_kernel, k_scale=dh ** -0.5)
    return pl.pallas_call(
        kern,
        grid=(b, s // L),
        in_specs=[seq_spec, seq_spec, const(conv_w.shape), const(conv_b.shape), const(wq.shape),
                  const(wk.shape), const(wv.shape), const(wg.shape), const(gb.shape),
                  const(ngain.shape), const(skip.shape)],
        out_specs=seq_spec,
        out_shape=jax.ShapeDtypeStruct((b, s, d), BF16),
        scratch_shapes=[
            pltpu.VMEM((L + SUBLANES, d), F32),
            pltpu.VMEM((n_heads, dh, dh), F32),
            pltpu.VMEM((n_heads, 1, dh), F32),
            pltpu.VMEM((n_heads, 1, LANES), F32),
            pltpu.VMEM((L, L), BF16),
        ],
        compiler_params=pltpu.CompilerParams(
            dimension_semantics=("arbitrary", "arbitrary"), vmem_limit_bytes=VMEM_LIMIT),
        name="mlstm",
    )(xm, zm, conv_w, conv_b, wq, wk, wv, wg, gb, ngain, skip)


def _fox_kernel(q_ref, k_ref, v_ref, cr_ref, zf_ref, g_ref, o_ref):
    s_len, dh = q_ref.shape
    tq = FOX_Q
    causal = _tril_mask(tq)
    gain = g_ref[...]

    for qi in reversed(range(s_len // tq)):
        r0 = qi * tq
        kv = r0 + tq
        s = lax.dot_general(q_ref[r0:kv, :], k_ref[0:kv, :], (((1,), (1,)), ((), ())),
                            preferred_element_type=F32)
        s = s - cr_ref[:, 0:kv]
        s_diag = jnp.where(causal, s[:, r0:], NEG_INF)
        s = s_diag if qi == 0 else jnp.concatenate([s[:, :r0], s_diag], axis=1)
        m = jnp.max(s, axis=1, keepdims=True)
        p = jnp.exp2(s - m).astype(BF16)
        v_ext = jnp.concatenate([v_ref[0:kv, :], jnp.ones((kv, dh), BF16)], axis=1)
        acc = jnp.dot(p, v_ext, preferred_element_type=F32)

        out = acc[:, :dh] / acc[:, dh:]
        ms = jnp.mean(out * out, axis=1, keepdims=True)
        z = zf_ref[r0:kv, :].astype(F32)
        o_ref[r0:kv, :] = (out * lax.rsqrt(ms + EPS) * gain * (z * _sigmoid(z))).astype(o_ref.dtype)


def _fox(q, k, v, c_row, zf, gain):
    b, s, d = q.shape
    n_heads = N_FOX_HEADS
    dh = d // n_heads
    assert dh == LANES and s % FOX_Q == 0
    head_spec = pl.BlockSpec((None, s, dh), lambda bi, hi: (bi, 0, hi))
    return pl.pallas_call(
        _fox_kernel,
        grid=(b, n_heads),
        in_specs=[head_spec, head_spec, head_spec,
                  pl.BlockSpec((None, None, 1, s), lambda bi, hi: (bi, hi, 0, 0)),
                  head_spec,
                  pl.BlockSpec((1, dh), lambda bi, hi: (0, hi))],
        out_specs=head_spec,
        out_shape=jax.ShapeDtypeStruct((b, s, d), BF16),
        compiler_params=pltpu.CompilerParams(
            dimension_semantics=("arbitrary", "arbitrary"), vmem_limit_bytes=VMEM_LIMIT),
        name="fox",
    )(q, k, v, c_row, zf, gain)


def _out_proj_kernel(ym_ref, yf_ref, x_ref, wm_ref, wf_ref, g_ref, o_ref, *, final_norm):
    y = jnp.dot(ym_ref[...], wm_ref[...], preferred_element_type=F32)
    y = y + jnp.dot(yf_ref[...], wf_ref[...], preferred_element_type=F32)
    r = x_ref[...] + y
    if final_norm:
        ms = jnp.mean(r * r, axis=-1, keepdims=True)
        r = r * lax.rsqrt(ms + EPS) * g_ref[...]
    o_ref[...] = r


def _out_proj(ym, yf, x2, w_m, w_f, gain, final_norm):
    t_rows, d = x2.shape
    rows = OUT_PROJ_ROWS
    assert t_rows % rows == 0
    row = lambda width: pl.BlockSpec((rows, width), lambda t: (t, 0))
    const = lambda shape: pl.BlockSpec(shape, lambda t: (0, 0))
    kern = functools.partial(_out_proj_kernel, final_norm=final_norm)
    return pl.pallas_call(
        kern,
        grid=(t_rows // rows,),
        in_specs=[row(ym.shape[1]), row(yf.shape[1]), row(d), const(w_m.shape), const(w_f.shape),
                  const((1, d))],
        out_specs=row(d),
        out_shape=jax.ShapeDtypeStruct((t_rows, d), F32),
        compiler_params=pltpu.CompilerParams(
            dimension_semantics=("arbitrary",), vmem_limit_bytes=VMEM_LIMIT),
        name="out_proj",
    )(ym, yf, x2, w_m, w_f, gain)


def _block_diag(w, dh):
    nb, bs, _ = w.shape
    d = nb * bs
    tiled = jnp.broadcast_to(w.reshape(d, 1, bs), (d, dh // bs, bs)).reshape(d, dh)
    r = lax.broadcasted_iota(jnp.int32, (d, dh), 0)
    c = lax.broadcasted_iota(jnp.int32, (d, dh), 1)
    keep = (r % dh) // bs == c // bs
    return jnp.where(keep, tiled, 0.0).reshape(d // dh, dh, dh)


def kernel(x, norm_gain, w_in, conv_w, conv_b, w_q_m, w_k_m, w_v_m, w_igate, b_igate, w_fgate,
           b_fgate, mlstm_norm_gain, mlstm_skip, fox_forget_bias, fox_norm_gain, w_out,
           final_norm_gain):
    b, s, d = x.shape
    depth = norm_gain.shape[0]
    d_m = conv_w.shape[-1]
    d_f = fox_norm_gain.shape[-1]
    dh_m = d_m // N_MLSTM_HEADS
    dh_f = d_f // N_FOX_HEADS
    n_main = 2 * d_m + 4 * d_f
    assert d_m == d and d_f == d

    x2 = x.reshape(b * s, d)
    for l in range(depth):
        w_main = w_in[l][:, :n_main].astype(BF16)
        w_f = jnp.pad(w_in[l][:, n_main:], ((0, 0), (0, LANES - N_FOX_HEADS))).astype(BF16)
        f_bias = jnp.pad(fox_forget_bias[l], (0, LANES - N_FOX_HEADS)).reshape(1, LANES)
        xm, zm, qf, kf, vf, zf, c = _in_proj(x2, norm_gain[l].reshape(1, d), w_main, w_f, f_bias,
                                              s, dh_f ** -0.5 * LOG2E)

        lane_pad = ((0, 0), (0, LANES - N_MLSTM_HEADS))
        wg = jnp.concatenate([jnp.pad(w_igate[l], lane_pad), jnp.pad(w_fgate[l], lane_pad)], axis=1)
        gb = jnp.concatenate([jnp.pad(b_igate[l][None], lane_pad),
                              jnp.pad(b_fgate[l][None], lane_pad)], axis=1)
        y_m = _mlstm(xm.reshape(b, s, d_m), zm.reshape(b, s, d_m), conv_w[l],
                     conv_b[l].reshape(1, d_m),
                     _block_diag(w_q_m[l], dh_m).astype(BF16),
                     _block_diag(w_k_m[l], dh_m).astype(BF16),
                     _block_diag(w_v_m[l], dh_m).astype(BF16),
                     wg.astype(BF16), gb,
                     mlstm_norm_gain[l].reshape(1, d_m), mlstm_skip[l].reshape(1, d_m))

        c8 = c.reshape(b, s, LANES)[:, :, :N_FOX_HEADS]
        c_row = jnp.transpose(c8, (0, 2, 1)).reshape(b, N_FOX_HEADS, 1, s)
        y_f = _fox(qf.reshape(b, s, d_f), kf.reshape(b, s, d_f), vf.reshape(b, s, d_f), c_row,
                   zf.reshape(b, s, d_f), fox_norm_gain[l].reshape(1, d_f))

        w_o = w_out[l].astype(BF16)
        x2 = _out_proj(y_m.reshape(b * s, d_m), y_f.reshape(b * s, d_f), x2, w_o[:d_m], w_o[d_m:],
                       final_norm_gain.reshape(1, d), final_norm=(l == depth - 1))
    return x2.reshape(b, s, d)
```

```python
import functools
import math

import jax
import jax.numpy as jnp
from jax import lax
from jax.experimental import pallas as pl
from jax.experimental.pallas import tpu as pltpu

EPS = 1e-6
N_MLSTM_HEADS = 4
N_FOX_HEADS = 8
QKV_BLOCK = 4
CONV_WIDTH = 4

LANES = 128
SUBLANES = 8
VMEM_LIMIT = 56 * 1024 * 1024

IN_PROJ_ROWS = 256
OUT_PROJ_ROWS = 512
MLSTM_CHUNK = 256
MLSTM_SEQS = 2
FOX_Q = 256
LOG2E = math.log2(math.e)

BF16 = jnp.bfloat16
F32 = jnp.float32
NEG_INF = float("-inf")


def _log_sigmoid(z):
    return jnp.minimum(z, 0.0) - jnp.log(1.0 + jnp.exp(-jnp.abs(z)))


def _sigmoid(z):
    return 1.0 / (1.0 + jnp.exp(-z))


def _silu(z):
    return z * _sigmoid(z)


def _split3(v):
    hi = v.astype(BF16)
    r = v - hi.astype(F32)
    mid = r.astype(BF16)
    lo = (r - mid.astype(F32)).astype(BF16)
    return hi, mid, lo


def _cumsum_rows(tri, v):
    w = v.shape[1]
    cat = jnp.concatenate(_split3(v), axis=1)
    cs = jnp.dot(tri, cat, preferred_element_type=F32)
    return cs[:, :w] + cs[:, w:2 * w] + cs[:, 2 * w:]


def _tril_mask(n):
    row = lax.broadcasted_iota(jnp.int32, (n, n), 0)
    col = lax.broadcasted_iota(jnp.int32, (n, n), 1)
    return col <= row


def _in_proj_kernel(x_ref, g_ref, w_ref, wf_ref, fb_ref,
                    xm_ref, gm_ref, q_ref, k_ref, v_ref, gf_ref, c_ref,
                    tri_ref, carry_ref, *, tiles_per_seq, q_scale):
    t = pl.program_id(0)
    rows = x_ref.shape[0]
    d = x_ref.shape[1]

    @pl.when(t == 0)
    def _():
        tri_ref[...] = jnp.where(_tril_mask(rows), 1.0, 0.0).astype(BF16)

    @pl.when(t % tiles_per_seq == 0)
    def _():
        carry_ref[...] = jnp.zeros_like(carry_ref)

    x = x_ref[...]
    ms = jnp.mean(x * x, axis=-1, keepdims=True)
    hn = (x * lax.rsqrt(ms + EPS) * g_ref[...]).astype(BF16)

    def proj(idx):
        return jnp.dot(hn, w_ref[:, idx * d:(idx + 1) * d], preferred_element_type=F32)

    xm_ref[...] = proj(0).astype(xm_ref.dtype)
    gm_ref[...] = _silu(proj(1)).astype(gm_ref.dtype)
    q_ref[...] = (proj(2) * q_scale).astype(q_ref.dtype)
    k_ref[...] = proj(3).astype(k_ref.dtype)
    v_ref[...] = proj(4).astype(v_ref.dtype)
    gf_ref[...] = _silu(proj(5)).astype(gf_ref.dtype)

    f = jnp.dot(hn, wf_ref[...], preferred_element_type=F32) + fb_ref[...]
    c = _cumsum_rows(tri_ref[...], _log_sigmoid(f)) + carry_ref[...]
    c_ref[...] = c * LOG2E
    carry_ref[...] = c[rows - 1:rows, :]


def _in_proj(x2, gain, w_main, w_f, f_bias, seq_len, q_scale):
    t_rows, d = x2.shape
    rows = IN_PROJ_ROWS
    assert t_rows % rows == 0 and seq_len % rows == 0
    assert w_main.shape[1] == 6 * d
    row_spec = pl.BlockSpec((rows, d), lambda t: (t, 0))
    const = lambda shape: pl.BlockSpec(shape, lambda t: (0, 0), pipeline_mode=pl.Buffered(1))
    kern = functools.partial(_in_proj_kernel, tiles_per_seq=seq_len // rows, q_scale=q_scale)
    return pl.pallas_call(
        kern,
        grid=(t_rows // rows,),
        in_specs=[row_spec, const((1, d)), const(w_main.shape), const(w_f.shape), const((1, LANES))],
        out_specs=[row_spec] * 6 + [pl.BlockSpec((rows, LANES), lambda t: (t, 0))],
        out_shape=[jax.ShapeDtypeStruct((t_rows, d), BF16)] * 6
        + [jax.ShapeDtypeStruct((t_rows, LANES), F32)],
        scratch_shapes=[pltpu.VMEM((rows, rows), BF16), pltpu.VMEM((1, LANES), F32)],
        compiler_params=pltpu.CompilerParams(
            dimension_semantics=("arbitrary",), vmem_limit_bytes=VMEM_LIMIT),
        name="in_proj",
    )(x2, gain, w_main, w_f, f_bias)


def _mlstm_kernel(xm_ref, gm_ref, cw_ref, cb_ref, wq_ref, wk_ref, wv_ref, wg_ref, gb_ref,
                  ng_ref, sk_ref, y_ref,
                  tail_ref, c_sc, n_sc, m_sc, tri_ref, shift_ref, *, k_scale):
    ci = pl.program_id(1)
    n_seqs, L, _ = xm_ref.shape

    @pl.when(ci == 0)
    def _():
        c_sc[...] = jnp.zeros_like(c_sc)
        n_sc[...] = jnp.zeros_like(n_sc)
        m_sc[...] = jnp.zeros_like(m_sc)
        tail_ref[...] = jnp.zeros_like(tail_ref)
        row = lax.broadcasted_iota(jnp.int32, (L, L), 0)
        col = lax.broadcasted_iota(jnp.int32, (L, L), 1)
        tri_ref[...] = jnp.where(col <= row, 1.0, 0.0).astype(BF16)
        for j in range(1, CONV_WIDTH):
            shift_ref[j - 1] = jnp.where(col == row - j, 1.0, 0.0).astype(BF16)

    seqs = range(n_seqs)
    fronts = [_mlstm_front(xm_ref.at[i], cw_ref, cb_ref, wq_ref, wk_ref, wv_ref, wg_ref, gb_ref,
                           tail_ref.at[i], tri_ref, shift_ref, k_scale) for i in seqs]
    mixed = [_mlstm_mix(*fronts[i][1:], c_sc.at[i], n_sc.at[i], m_sc.at[i]) for i in seqs]
    for i in seqs:
        _mlstm_out(mixed[i], fronts[i][0], gm_ref.at[i], ng_ref, sk_ref, y_ref.at[i])


def _mlstm_front(xm_ref, cw_ref, cb_ref, wq_ref, wk_ref, wv_ref, wg_ref, gb_ref,
                 tail_ref, tri_ref, shift_ref, k_scale):
    L, d_m = xm_ref.shape
    dh = wq_ref.shape[1]
    n_heads = wq_ref.shape[0]
    halo = SUBLANES

    x_b = xm_ref[...]
    x_f = x_b.astype(F32)
    conv = cb_ref[...] + x_f * cw_ref[CONV_WIDTH - 1:CONV_WIDTH, :]
    edge = jnp.concatenate([tail_ref[...], jnp.zeros((halo, d_m), F32)], axis=0)
    head = jnp.zeros((halo, d_m), F32)
    for j in range(1, CONV_WIDTH):
        w_j = cw_ref[CONV_WIDTH - 1 - j:CONV_WIDTH - j, :]
        conv = conv + jnp.dot(shift_ref[j - 1], x_b, preferred_element_type=F32) * w_j
        head = head + edge[halo - j:2 * halo - j, :] * w_j
    conv = jnp.concatenate([conv[:halo] + head, conv[halo:]], axis=0)
    tail_ref[...] = x_f[L - halo:L, :]
    xc = _silu(conv)
    xc_b = xc.astype(BF16)

    qs, ks, vs = [], [], []
    for h in range(n_heads):
        sl = slice(h * dh, (h + 1) * dh)
        q = jnp.dot(xc_b[:, sl], wq_ref[h], preferred_element_type=F32)
        k = jnp.dot(xc_b[:, sl], wk_ref[h], preferred_element_type=F32) * k_scale
        v = jnp.dot(x_b[:, sl], wv_ref[h], preferred_element_type=F32)
        qs.append(q.astype(BF16))
        ks.append((k, k.astype(BF16)))
        vs.append(v.astype(BF16))
    qkv = jnp.concatenate(qs + [kb for _, kb in ks] + vs, axis=1)
    gates = jnp.dot(qkv, wg_ref[...], preferred_element_type=F32) + gb_ref[...]

    ig = gates[:, :LANES]
    bcum = _cumsum_rows(tri_ref[...], _log_sigmoid(gates[:, LANES:]))
    a_col = ig - bcum
    return xc, qs, ks, vs, a_col, bcum


def _mlstm_mix(qs, ks, vs, a_col, bcum, c_sc, n_sc, m_sc):
    n_heads = len(qs)
    L, dh = qs[0].shape
    a_rows = a_col.T
    causal = _tril_mask(L)
    ones_blk = jnp.ones((L, LANES), BF16)

    heads = range(n_heads)
    nt = (((1,), (1,)), ((), ()))
    m_prev = [m_sc[h][:, 0:1] for h in heads]
    dm = [jnp.where(causal, a_rows[h:h + 1, :], NEG_INF) for h in heads]
    m_run = [jnp.maximum(m_prev[h], jnp.max(dm[h], axis=1, keepdims=True)) for h in heads]
    w_inter = [jnp.exp(m_prev[h] - m_run[h]) for h in heads]
    s_b = [(lax.dot_general(qs[h], ks[h][1], nt, preferred_element_type=F32)
            * jnp.exp(dm[h] - m_run[h])).astype(BF16) for h in heads]
    num = [w_inter[h] * jnp.dot(qs[h], c_sc[h].astype(BF16), preferred_element_type=F32)
           + jnp.dot(s_b[h], vs[h], preferred_element_type=F32) for h in heads]
    qn = [lax.dot_general(qs[h], jnp.broadcast_to(n_sc[h], (LANES, dh)).astype(BF16), nt,
                          preferred_element_type=F32) for h in heads]
    den = [w_inter[h] * qn[h] + jnp.dot(s_b[h], ones_blk, preferred_element_type=F32)
           for h in heads]
    m_t = [bcum[:, h:h + 1] + m_run[h] for h in heads]
    inv = [1.0 / jnp.maximum(jnp.abs(den[h]), jnp.exp(-m_t[h])) for h in heads]
    hh = [num[h] * jnp.concatenate([inv[h]] * (dh // LANES), axis=1) for h in heads]

    for h in heads:
        m_last = m_run[h][L - 1:L, :]
        decay = jnp.exp(m_prev[h] - m_last)
        kw = ks[h][0] * jnp.exp(a_col[:, h:h + 1] - m_last)
        c_sc[h] = decay * c_sc[h] + lax.dot_general(
            kw.astype(BF16), vs[h], (((0,), (0,)), ((), ())), preferred_element_type=F32)
        n_sc[h] = decay * n_sc[h] + jnp.sum(kw, axis=0, keepdims=True)
        m_sc[h] = jnp.broadcast_to(m_t[h][L - 1:L, :], m_sc.shape[1:])
    return hh


def _mlstm_out(hh, xc, gm_ref, ng_ref, sk_ref, y_ref):
    heads = range(len(hh))
    dh = hh[0].shape[1]
    mu = [jnp.mean(hh[h], axis=1, keepdims=True) for h in heads]
    cen = [hh[h] - mu[h] for h in heads]
    var = [jnp.mean(cen[h] * cen[h], axis=1, keepdims=True) for h in heads]
    for h in heads:
        sl = slice(h * dh, (h + 1) * dh)
        hn = cen[h] * lax.rsqrt(var[h] + EPS) * ng_ref[:, sl] + sk_ref[:, sl] * xc[:, sl]
        y_ref[:, sl] = (hn * gm_ref[:, sl].astype(F32)).astype(y_ref.dtype)


def _mlstm(xm, gm, conv_w, conv_b, wq, wk, wv, wg, gb, ngain, skip):
    b, s, d = xm.shape
    L = MLSTM_CHUNK
    assert s % L == 0
    n_heads, dh, _ = wq.shape
    n_seqs = MLSTM_SEQS
    assert b % n_seqs == 0
    seq_spec = pl.BlockSpec((n_seqs, L, d), lambda bi, ci: (bi, ci, 0))

    def const(shape):
        nd = len(shape)
        return pl.BlockSpec(shape, lambda bi, ci: (0,) * nd)

    kern = functools.partial(_mlstm_kernel, k_scale=dh ** -0.5)
    return pl.pallas_call(
        kern,
        grid=(b // n_seqs, s // L),
        in_specs=[seq_spec, seq_spec, const(conv_w.shape), const(conv_b.shape), const(wq.shape),
                  const(wk.shape), const(wv.shape), const(wg.shape), const(gb.shape),
                  const(ngain.shape), const(skip.shape)],
        out_specs=seq_spec,
        out_shape=jax.ShapeDtypeStruct((b, s, d), BF16),
        scratch_shapes=[
            pltpu.VMEM((n_seqs, SUBLANES, d), F32),
            pltpu.VMEM((n_seqs, n_heads, dh, dh), F32),
            pltpu.VMEM((n_seqs, n_heads, 1, dh), F32),
            pltpu.VMEM((n_seqs, n_heads, 1, LANES), F32),
            pltpu.VMEM((L, L), BF16),
            pltpu.VMEM((CONV_WIDTH - 1, L, L), BF16),
        ],
        compiler_params=pltpu.CompilerParams(
            dimension_semantics=("arbitrary", "arbitrary"), vmem_limit_bytes=VMEM_LIMIT),
        name="mlstm",
    )(xm, gm, conv_w, conv_b, wq, wk, wv, wg, gb, ngain, skip)


def _fox_kernel(q_ref, k_ref, v_ref, cr_ref, gf_ref, g_ref, o_ref):
    s_len, dh = q_ref.shape
    tq = FOX_Q
    causal = _tril_mask(tq)
    gain = g_ref[...]

    for qi in reversed(range(s_len // tq)):
        r0 = qi * tq
        kv = r0 + tq
        s = lax.dot_general(q_ref[r0:kv, :], k_ref[0:kv, :], (((1,), (1,)), ((), ())),
                            preferred_element_type=F32)
        s = s - cr_ref[:, 0:kv]
        s_diag = jnp.where(causal, s[:, r0:], NEG_INF)
        s = s_diag if qi == 0 else jnp.concatenate([s[:, :r0], s_diag], axis=1)
        m = jnp.max(s, axis=1, keepdims=True)
        p = jnp.exp2(s - m).astype(BF16)
        v_ext = jnp.concatenate([v_ref[0:kv, :], jnp.ones((kv, dh), BF16)], axis=1)
        acc = jnp.dot(p, v_ext, preferred_element_type=F32)

        out = acc[:, :dh] / acc[:, dh:]
        ms = jnp.mean(out * out, axis=1, keepdims=True)
        o_ref[r0:kv, :] = (out * lax.rsqrt(ms + EPS) * gain
                           * gf_ref[r0:kv, :].astype(F32)).astype(o_ref.dtype)


def _fox(q, k, v, c_row, gf, gain):
    b, s, d = q.shape
    n_heads = N_FOX_HEADS
    dh = d // n_heads
    assert dh == LANES and s % FOX_Q == 0
    head_spec = pl.BlockSpec((None, s, dh), lambda bi, hi: (bi, 0, hi))
    return pl.pallas_call(
        _fox_kernel,
        grid=(b, n_heads),
        in_specs=[head_spec, head_spec, head_spec,
                  pl.BlockSpec((None, None, 1, s), lambda bi, hi: (bi, hi, 0, 0)),
                  head_spec,
                  pl.BlockSpec((1, dh), lambda bi, hi: (0, hi))],
        out_specs=head_spec,
        out_shape=jax.ShapeDtypeStruct((b, s, d), BF16),
        compiler_params=pltpu.CompilerParams(
            dimension_semantics=("arbitrary", "arbitrary"), vmem_limit_bytes=VMEM_LIMIT),
        name="fox",
    )(q, k, v, c_row, gf, gain)


def _out_proj_kernel(ym_ref, yf_ref, x_ref, wm_ref, wf_ref, g_ref, o_ref, *, final_norm):
    y = jnp.dot(ym_ref[...], wm_ref[...], preferred_element_type=F32)
    y = y + jnp.dot(yf_ref[...], wf_ref[...], preferred_element_type=F32)
    r = x_ref[...] + y
    if final_norm:
        ms = jnp.mean(r * r, axis=-1, keepdims=True)
        r = r * lax.rsqrt(ms + EPS) * g_ref[...]
    o_ref[...] = r


def _out_proj(ym, yf, x2, w_m, w_f, gain, final_norm):
    t_rows, d = x2.shape
    rows = OUT_PROJ_ROWS
    assert t_rows % rows == 0
    row = lambda width: pl.BlockSpec((rows, width), lambda t: (t, 0))
    const = lambda shape: pl.BlockSpec(shape, lambda t: (0, 0))
    kern = functools.partial(_out_proj_kernel, final_norm=final_norm)
    return pl.pallas_call(
        kern,
        grid=(t_rows // rows,),
        in_specs=[row(ym.shape[1]), row(yf.shape[1]), row(d), const(w_m.shape), const(w_f.shape),
                  const((1, d))],
        out_specs=row(d),
        out_shape=jax.ShapeDtypeStruct((t_rows, d), F32),
        compiler_params=pltpu.CompilerParams(
            dimension_semantics=("arbitrary",), vmem_limit_bytes=VMEM_LIMIT),
        name="out_proj",
    )(ym, yf, x2, w_m, w_f, gain)


def _block_diag(w, dh):
    nb, bs, _ = w.shape
    d = nb * bs
    tiled = jnp.broadcast_to(w.reshape(d, 1, bs), (d, dh // bs, bs)).reshape(d, dh)
    r = lax.broadcasted_iota(jnp.int32, (d, dh), 0)
    c = lax.broadcasted_iota(jnp.int32, (d, dh), 1)
    keep = (r % dh) // bs == c // bs
    return jnp.where(keep, tiled, 0.0).reshape(d // dh, dh, dh)


def kernel(x, norm_gain, w_in, conv_w, conv_b, w_q_m, w_k_m, w_v_m, w_igate, b_igate, w_fgate,
           b_fgate, mlstm_norm_gain, mlstm_skip, fox_forget_bias, fox_norm_gain, w_out,
           final_norm_gain):
    b, s, d = x.shape
    depth = norm_gain.shape[0]
    d_m = conv_w.shape[-1]
    d_f = fox_norm_gain.shape[-1]
    dh_m = d_m // N_MLSTM_HEADS
    dh_f = d_f // N_FOX_HEADS
    n_main = 2 * d_m + 4 * d_f
    assert d_m == d and d_f == d

    x2 = x.reshape(b * s, d)
    for l in range(depth):
        w_main = w_in[l][:, :n_main].astype(BF16)
        w_f = jnp.pad(w_in[l][:, n_main:], ((0, 0), (0, LANES - N_FOX_HEADS))).astype(BF16)
        f_bias = jnp.pad(fox_forget_bias[l], (0, LANES - N_FOX_HEADS)).reshape(1, LANES)
        xm, gm, qf, kf, vf, gf, c = _in_proj(
            x2, norm_gain[l].reshape(1, d), w_main, w_f, f_bias, s, dh_f ** -0.5 * LOG2E)

        lane_pad = ((0, 0), (0, LANES - N_MLSTM_HEADS))
        wg = jnp.concatenate([jnp.pad(w_igate[l], lane_pad), jnp.pad(w_fgate[l], lane_pad)], axis=1)
        gb = jnp.concatenate([jnp.pad(b_igate[l][None], lane_pad),
                              jnp.pad(b_fgate[l][None], lane_pad)], axis=1)
        y_m = _mlstm(xm.reshape(b, s, d_m), gm.reshape(b, s, d_m), conv_w[l],
                     conv_b[l].reshape(1, d_m),
                     _block_diag(w_q_m[l], dh_m).astype(BF16),
                     _block_diag(w_k_m[l], dh_m).astype(BF16),
                     _block_diag(w_v_m[l], dh_m).astype(BF16),
                     wg.astype(BF16), gb,
                     mlstm_norm_gain[l].reshape(1, d_m), mlstm_skip[l].reshape(1, d_m))

        c8 = c.reshape(b, s, LANES)[:, :, :N_FOX_HEADS]
        c_row = jnp.transpose(c8, (0, 2, 1)).reshape(b, N_FOX_HEADS, 1, s)
        y_f = _fox(qf.reshape(b, s, d_f), kf.reshape(b, s, d_f), vf.reshape(b, s, d_f), c_row,
                   gf.reshape(b, s, d_f), fox_norm_gain[l].reshape(1, d_f))

        w_o = w_out[l].astype(BF16)
        x2 = _out_proj(y_m.reshape(b * s, d_m), y_f.reshape(b * s, d_f), x2, w_o[:d_m], w_o[d_m:],
                       final_norm_gain.reshape(1, d), final_norm=(l == depth - 1))
    return x2.reshape(b, s, d)
```

```python
import functools
import math

import jax
import jax.numpy as jnp
from jax import lax
from jax.experimental import pallas as pl
from jax.experimental.pallas import tpu as pltpu

EPS = 1e-6
N_MLSTM_HEADS = 4
N_FOX_HEADS = 8
QKV_BLOCK = 4
CONV_WIDTH = 4

LANES = 128
SUBLANES = 8
VMEM_LIMIT = 56 * 1024 * 1024

IN_PROJ_ROWS = 256
OUT_PROJ_ROWS = 1024
OUT_PROJ_SUB = 256
MLSTM_CHUNK = 256
MLSTM_SEQS = 2
FOX_Q = 256
LOG2E = math.log2(math.e)

BF16 = jnp.bfloat16
F32 = jnp.float32
NEG_INF = float("-inf")


def _log_sigmoid(z):
    return jnp.minimum(z, 0.0) - jnp.log(1.0 + jnp.exp(-jnp.abs(z)))


def _sigmoid(z):
    return 1.0 / (1.0 + jnp.exp(-z))


def _silu(z):
    return z * _sigmoid(z)


def _split3(v):
    hi = v.astype(BF16)
    r = v - hi.astype(F32)
    mid = r.astype(BF16)
    lo = (r - mid.astype(F32)).astype(BF16)
    return hi, mid, lo


def _cumsum_rows(tri, v):
    w = v.shape[1]
    cat = jnp.concatenate(_split3(v), axis=1)
    cs = jnp.dot(tri, cat, preferred_element_type=F32)
    return cs[:, :w] + cs[:, w:2 * w] + cs[:, 2 * w:]


def _tril_mask(n):
    row = lax.broadcasted_iota(jnp.int32, (n, n), 0)
    col = lax.broadcasted_iota(jnp.int32, (n, n), 1)
    return col <= row


def _in_proj_kernel(x_ref, g_ref, w_ref, fb_ref,
                    xm_ref, gm_ref, q_ref, k_ref, v_ref, gf_ref, c_ref,
                    wb_ref, wfb_ref, tri_ref, carry_ref, *, tiles_per_seq, q_scale):
    t = pl.program_id(0)
    rows = x_ref.shape[0]
    d = x_ref.shape[1]
    n_main = wb_ref.shape[1]
    n_gate = w_ref.shape[1] - n_main

    @pl.when(t == 0)
    def _():
        tri_ref[...] = jnp.where(_tril_mask(rows), 1.0, 0.0).astype(BF16)
        for c0 in range(0, n_main, d):
            wb_ref[:, c0:c0 + d] = w_ref[:, c0:c0 + d].astype(BF16)
        wfb_ref[...] = jnp.zeros_like(wfb_ref)
        wfb_ref[:, 0:n_gate] = w_ref[:, n_main:n_main + n_gate].astype(BF16)

    @pl.when(t % tiles_per_seq == 0)
    def _():
        carry_ref[...] = jnp.zeros_like(carry_ref)

    x = x_ref[...]
    ms = jnp.mean(x * x, axis=-1, keepdims=True)
    hn = (x * lax.rsqrt(ms + EPS) * g_ref[...]).astype(BF16)

    def proj(idx):
        return jnp.dot(hn, wb_ref[:, idx * d:(idx + 1) * d], preferred_element_type=F32)

    xm_ref[...] = proj(0).astype(xm_ref.dtype)
    gm_ref[...] = _silu(proj(1)).astype(gm_ref.dtype)
    q_ref[...] = (proj(2) * q_scale).astype(q_ref.dtype)
    k_ref[...] = proj(3).astype(k_ref.dtype)
    v_ref[...] = proj(4).astype(v_ref.dtype)
    gf_ref[...] = _silu(proj(5)).astype(gf_ref.dtype)

    f = jnp.dot(hn, wfb_ref[...], preferred_element_type=F32) + fb_ref[...]
    c = _cumsum_rows(tri_ref[...], _log_sigmoid(f)) + carry_ref[...]
    c_ref[...] = (c * LOG2E).T[0:c_ref.shape[0], :]
    carry_ref[...] = c[rows - 1:rows, :]


def _in_proj(x2, gain, w_in, f_bias, seq_len, q_scale):
    t_rows, d = x2.shape
    rows = IN_PROJ_ROWS
    assert t_rows % rows == 0 and seq_len % rows == 0
    n_main = 6 * d
    n_gate = w_in.shape[1] - n_main
    assert 0 < n_gate <= SUBLANES
    row_spec = pl.BlockSpec((rows, d), lambda t: (t, 0))
    const = lambda shape: pl.BlockSpec(shape, lambda t: (0, 0), pipeline_mode=pl.Buffered(1))
    kern = functools.partial(_in_proj_kernel, tiles_per_seq=seq_len // rows, q_scale=q_scale)
    return pl.pallas_call(
        kern,
        grid=(t_rows // rows,),
        in_specs=[row_spec, const((1, d)), const(w_in.shape), const((1, LANES))],
        out_specs=[row_spec] * 6 + [pl.BlockSpec((SUBLANES, rows), lambda t: (0, t))],
        out_shape=[jax.ShapeDtypeStruct((t_rows, d), BF16)] * 6
        + [jax.ShapeDtypeStruct((SUBLANES, t_rows), F32)],
        scratch_shapes=[pltpu.VMEM((d, n_main), BF16), pltpu.VMEM((d, LANES), BF16),
                        pltpu.VMEM((rows, rows), BF16), pltpu.VMEM((1, LANES), F32)],
        compiler_params=pltpu.CompilerParams(
            dimension_semantics=("arbitrary",), vmem_limit_bytes=VMEM_LIMIT),
        name="in_proj",
    )(x2, gain, w_in, f_bias)


def _mlstm_kernel(xm_ref, gm_ref, cw_ref, cb_ref, wq_ref, wk_ref, wv_ref, wg_ref, gb_ref,
                  ng_ref, sk_ref, y_ref,
                  tail_ref, c_sc, n_sc, m_sc, tri_ref, shift_ref, *, k_scale):
    ci = pl.program_id(1)
    n_seqs, L, _ = xm_ref.shape

    @pl.when(ci == 0)
    def _():
        c_sc[...] = jnp.zeros_like(c_sc)
        n_sc[...] = jnp.zeros_like(n_sc)
        m_sc[...] = jnp.zeros_like(m_sc)
        tail_ref[...] = jnp.zeros_like(tail_ref)
        row = lax.broadcasted_iota(jnp.int32, (L, L), 0)
        col = lax.broadcasted_iota(jnp.int32, (L, L), 1)
        tri_ref[...] = jnp.where(col <= row, 1.0, 0.0).astype(BF16)
        for j in range(1, CONV_WIDTH):
            shift_ref[j - 1] = jnp.where(col == row - j, 1.0, 0.0).astype(BF16)

    seqs = range(n_seqs)
    fronts = [_mlstm_front(xm_ref.at[i], cw_ref, cb_ref, wq_ref, wk_ref, wv_ref, wg_ref, gb_ref,
                           tail_ref.at[i], tri_ref, shift_ref, k_scale) for i in seqs]
    mixed = [_mlstm_mix(*fronts[i][1:], c_sc.at[i], n_sc.at[i], m_sc.at[i]) for i in seqs]
    for i in seqs:
        _mlstm_out(mixed[i], fronts[i][0], gm_ref.at[i], ng_ref, sk_ref, y_ref.at[i])


def _mlstm_front(xm_ref, cw_ref, cb_ref, wq_ref, wk_ref, wv_ref, wg_ref, gb_ref,
                 tail_ref, tri_ref, shift_ref, k_scale):
    L, d_m = xm_ref.shape
    dh = wq_ref.shape[1]
    n_heads = wq_ref.shape[0]
    halo = SUBLANES

    x_b = xm_ref[...]
    x_f = x_b.astype(F32)
    conv = cb_ref[...] + x_f * cw_ref[CONV_WIDTH - 1:CONV_WIDTH, :]
    edge = jnp.concatenate([tail_ref[...], jnp.zeros((halo, d_m), F32)], axis=0)
    head = jnp.zeros((halo, d_m), F32)
    for j in range(1, CONV_WIDTH):
        w_j = cw_ref[CONV_WIDTH - 1 - j:CONV_WIDTH - j, :]
        conv = conv + jnp.dot(shift_ref[j - 1], x_b, preferred_element_type=F32) * w_j
        head = head + edge[halo - j:2 * halo - j, :] * w_j
    conv = jnp.concatenate([conv[:halo] + head, conv[halo:]], axis=0)
    tail_ref[...] = x_f[L - halo:L, :]
    xc = _silu(conv)
    xc_b = xc.astype(BF16)

    qs, ks, vs = [], [], []
    for h in range(n_heads):
        sl = slice(h * dh, (h + 1) * dh)
        q = jnp.dot(xc_b[:, sl], wq_ref[h], preferred_element_type=F32)
        k = jnp.dot(xc_b[:, sl], wk_ref[h], preferred_element_type=F32) * k_scale
        v = jnp.dot(x_b[:, sl], wv_ref[h], preferred_element_type=F32)
        qs.append(q.astype(BF16))
        ks.append((k, k.astype(BF16)))
        vs.append(v.astype(BF16))
    qkv = jnp.concatenate(qs + [kb for _, kb in ks] + vs, axis=1)
    gates = jnp.dot(qkv, wg_ref[...], preferred_element_type=F32) + gb_ref[...]

    ig = gates[:, :LANES]
    bcum = _cumsum_rows(tri_ref[...], _log_sigmoid(gates[:, LANES:]))
    a_col = ig - bcum
    return xc, qs, ks, vs, a_col, bcum


def _mlstm_mix(qs, ks, vs, a_col, bcum, c_sc, n_sc, m_sc):
    n_heads = len(qs)
    L, dh = qs[0].shape
    a_rows = a_col.T
    causal = _tril_mask(L)
    ones_blk = jnp.ones((L, LANES), BF16)

    heads = range(n_heads)
    nt = (((1,), (1,)), ((), ()))
    m_prev = [m_sc[h][:, 0:1] for h in heads]
    dm = [jnp.where(causal, a_rows[h:h + 1, :], NEG_INF) for h in heads]
    m_run = [jnp.maximum(m_prev[h], jnp.max(dm[h], axis=1, keepdims=True)) for h in heads]
    w_inter = [jnp.exp(m_prev[h] - m_run[h]) for h in heads]
    s_b = [(lax.dot_general(qs[h], ks[h][1], nt, preferred_element_type=F32)
            * jnp.exp(dm[h] - m_run[h])).astype(BF16) for h in heads]
    num = [w_inter[h] * jnp.dot(qs[h], c_sc[h].astype(BF16), preferred_element_type=F32)
           + jnp.dot(s_b[h], vs[h], preferred_element_type=F32) for h in heads]
    qn = [lax.dot_general(qs[h], jnp.broadcast_to(n_sc[h], (LANES, dh)).astype(BF16), nt,
                          preferred_element_type=F32) for h in heads]
    den = [w_inter[h] * qn[h] + jnp.dot(s_b[h], ones_blk, preferred_element_type=F32)
           for h in heads]
    m_t = [bcum[:, h:h + 1] + m_run[h] for h in heads]
    inv = [1.0 / jnp.maximum(jnp.abs(den[h]), jnp.exp(-m_t[h])) for h in heads]
    hh = [num[h] * jnp.concatenate([inv[h]] * (dh // LANES), axis=1) for h in heads]

    for h in heads:
        m_last = m_run[h][L - 1:L, :]
        decay = jnp.exp(m_prev[h] - m_last)
        kw = ks[h][0] * jnp.exp(a_col[:, h:h + 1] - m_last)
        c_sc[h] = decay * c_sc[h] + lax.dot_general(
            kw.astype(BF16), vs[h], (((0,), (0,)), ((), ())), preferred_element_type=F32)
        n_sc[h] = decay * n_sc[h] + jnp.sum(kw, axis=0, keepdims=True)
        m_sc[h] = jnp.broadcast_to(m_t[h][L - 1:L, :], m_sc.shape[1:])
    return hh


def _mlstm_out(hh, xc, gm_ref, ng_ref, sk_ref, y_ref):
    heads = range(len(hh))
    dh = hh[0].shape[1]
    mu = [jnp.mean(hh[h], axis=1, keepdims=True) for h in heads]
    cen = [hh[h] - mu[h] for h in heads]
    var = [jnp.mean(cen[h] * cen[h], axis=1, keepdims=True) for h in heads]
    for h in heads:
        sl = slice(h * dh, (h + 1) * dh)
        hn = cen[h] * lax.rsqrt(var[h] + EPS) * ng_ref[:, sl] + sk_ref[:, sl] * xc[:, sl]
        y_ref[:, sl] = (hn * gm_ref[:, sl].astype(F32)).astype(y_ref.dtype)


def _mlstm(xm, gm, conv_w, conv_b, wq, wk, wv, wg, gb, ngain, skip):
    b, s, d = xm.shape
    L = MLSTM_CHUNK
    assert s % L == 0
    n_heads, dh, _ = wq.shape
    n_seqs = MLSTM_SEQS
    assert b % n_seqs == 0
    seq_spec = pl.BlockSpec((n_seqs, L, d), lambda bi, ci: (bi, ci, 0))

    def const(shape):
        nd = len(shape)
        return pl.BlockSpec(shape, lambda bi, ci: (0,) * nd)

    kern = functools.partial(_mlstm_kernel, k_scale=dh ** -0.5)
    return pl.pallas_call(
        kern,
        grid=(b // n_seqs, s // L),
        in_specs=[seq_spec, seq_spec, const(conv_w.shape), const(conv_b.shape), const(wq.shape),
                  const(wk.shape), const(wv.shape), const(wg.shape), const(gb.shape),
                  const(ngain.shape), const(skip.shape)],
        out_specs=seq_spec,
        out_shape=jax.ShapeDtypeStruct((b, s, d), BF16),
        scratch_shapes=[
            pltpu.VMEM((n_seqs, SUBLANES, d), F32),
            pltpu.VMEM((n_seqs, n_heads, dh, dh), F32),
            pltpu.VMEM((n_seqs, n_heads, 1, dh), F32),
            pltpu.VMEM((n_seqs, n_heads, 1, LANES), F32),
            pltpu.VMEM((L, L), BF16),
            pltpu.VMEM((CONV_WIDTH - 1, L, L), BF16),
        ],
        compiler_params=pltpu.CompilerParams(
            dimension_semantics=("arbitrary", "arbitrary"), vmem_limit_bytes=VMEM_LIMIT),
        name="mlstm",
    )(xm, gm, conv_w, conv_b, wq, wk, wv, wg, gb, ngain, skip)


def _fox_kernel(q_ref, k_ref, v_ref, cr_ref, gf_ref, g_ref, o_ref):
    s_len, dh = q_ref.shape
    tq = FOX_Q
    causal = _tril_mask(tq)
    gain = g_ref[...]
    c_keys = cr_ref[pl.ds(pl.program_id(1), 1), :]

    for qi in reversed(range(s_len // tq)):
        r0 = qi * tq
        kv = r0 + tq
        s = lax.dot_general(q_ref[r0:kv, :], k_ref[0:kv, :], (((1,), (1,)), ((), ())),
                            preferred_element_type=F32)
        s = s - c_keys[:, 0:kv]
        s_diag = jnp.where(causal, s[:, r0:], NEG_INF)
        s = s_diag if qi == 0 else jnp.concatenate([s[:, :r0], s_diag], axis=1)
        m = jnp.max(s, axis=1, keepdims=True)
        p = jnp.exp2(s - m).astype(BF16)
        v_ext = jnp.concatenate([v_ref[0:kv, :], jnp.ones((kv, dh), BF16)], axis=1)
        acc = jnp.dot(p, v_ext, preferred_element_type=F32)

        out = acc[:, :dh] / acc[:, dh:]
        ms = jnp.mean(out * out, axis=1, keepdims=True)
        o_ref[r0:kv, :] = (out * lax.rsqrt(ms + EPS) * gain
                           * gf_ref[r0:kv, :].astype(F32)).astype(o_ref.dtype)


def _fox(q, k, v, c_row, gf, gain):
    b, s, d = q.shape
    n_heads = N_FOX_HEADS
    dh = d // n_heads
    assert dh == LANES and s % FOX_Q == 0
    head_spec = pl.BlockSpec((None, s, dh), lambda bi, hi: (bi, 0, hi))
    return pl.pallas_call(
        _fox_kernel,
        grid=(b, n_heads),
        in_specs=[head_spec, head_spec, head_spec,
                  pl.BlockSpec((c_row.shape[0], s), lambda bi, hi: (0, bi)),
                  head_spec,
                  pl.BlockSpec((1, dh), lambda bi, hi: (0, hi))],
        out_specs=head_spec,
        out_shape=jax.ShapeDtypeStruct((b, s, d), BF16),
        compiler_params=pltpu.CompilerParams(
            dimension_semantics=("arbitrary", "arbitrary"), vmem_limit_bytes=VMEM_LIMIT),
        name="fox",
    )(q, k, v, c_row, gf, gain)


def _out_proj_kernel(ym_ref, yf_ref, x_ref, w_ref, g_ref, o_ref, wb_ref, *, final_norm):
    d_m = ym_ref.shape[1]

    @pl.when(pl.program_id(0) == 0)
    def _():
        wb_ref[...] = w_ref[...].astype(BF16)

    rows = x_ref.shape[0]
    for r0 in range(0, rows, OUT_PROJ_SUB):
        sl = slice(r0, r0 + OUT_PROJ_SUB)
        y = jnp.dot(ym_ref[sl, :], wb_ref[0:d_m, :], preferred_element_type=F32)
        y = y + jnp.dot(yf_ref[sl, :], wb_ref[d_m:, :], preferred_element_type=F32)
        r = x_ref[sl, :] + y
        if final_norm:
            ms = jnp.mean(r * r, axis=-1, keepdims=True)
            r = r * lax.rsqrt(ms + EPS) * g_ref[...]
        o_ref[sl, :] = r


def _out_proj(ym, yf, x2, w_out, gain, final_norm):
    t_rows, d = x2.shape
    rows = OUT_PROJ_ROWS
    assert t_rows % rows == 0
    row = lambda width: pl.BlockSpec((rows, width), lambda t: (t, 0))
    const = lambda shape: pl.BlockSpec(shape, lambda t: (0, 0), pipeline_mode=pl.Buffered(1))
    kern = functools.partial(_out_proj_kernel, final_norm=final_norm)
    return pl.pallas_call(
        kern,
        grid=(t_rows // rows,),
        in_specs=[row(ym.shape[1]), row(yf.shape[1]), row(d), const(w_out.shape), const((1, d))],
        out_specs=row(d),
        out_shape=jax.ShapeDtypeStruct((t_rows, d), F32),
        scratch_shapes=[pltpu.VMEM(w_out.shape, BF16)],
        compiler_params=pltpu.CompilerParams(
            dimension_semantics=("arbitrary",), vmem_limit_bytes=VMEM_LIMIT),
        name="out_proj",
    )(ym, yf, x2, w_out, gain)


def _block_diag(w, dh):
    nb, bs, _ = w.shape
    d = nb * bs
    tiled = jnp.broadcast_to(w.reshape(d, 1, bs), (d, dh // bs, bs)).reshape(d, dh)
    r = lax.broadcasted_iota(jnp.int32, (d, dh), 0)
    c = lax.broadcasted_iota(jnp.int32, (d, dh), 1)
    keep = (r % dh) // bs == c // bs
    return jnp.where(keep, tiled, 0.0).reshape(d // dh, dh, dh)


def kernel(x, norm_gain, w_in, conv_w, conv_b, w_q_m, w_k_m, w_v_m, w_igate, b_igate, w_fgate,
           b_fgate, mlstm_norm_gain, mlstm_skip, fox_forget_bias, fox_norm_gain, w_out,
           final_norm_gain):
    b, s, d = x.shape
    depth = norm_gain.shape[0]
    d_m = conv_w.shape[-1]
    d_f = fox_norm_gain.shape[-1]
    dh_m = d_m // N_MLSTM_HEADS
    dh_f = d_f // N_FOX_HEADS
    n_main = 2 * d_m + 4 * d_f
    assert d_m == d and d_f == d

    x2 = x.reshape(b * s, d)
    for l in range(depth):
        assert w_in.shape[2] == n_main + N_FOX_HEADS
        f_bias = jnp.pad(fox_forget_bias[l], (0, LANES - N_FOX_HEADS)).reshape(1, LANES)
        xm, gm, qf, kf, vf, gf, c_keys = _in_proj(
            x2, norm_gain[l].reshape(1, d), w_in[l], f_bias, s, dh_f ** -0.5 * LOG2E)

        lane_pad = ((0, 0), (0, LANES - N_MLSTM_HEADS))
        wg = jnp.concatenate([jnp.pad(w_igate[l], lane_pad), jnp.pad(w_fgate[l], lane_pad)], axis=1)
        gb = jnp.concatenate([jnp.pad(b_igate[l][None], lane_pad),
                              jnp.pad(b_fgate[l][None], lane_pad)], axis=1)
        y_m = _mlstm(xm.reshape(b, s, d_m), gm.reshape(b, s, d_m), conv_w[l],
                     conv_b[l].reshape(1, d_m),
                     _block_diag(w_q_m[l], dh_m).astype(BF16),
                     _block_diag(w_k_m[l], dh_m).astype(BF16),
                     _block_diag(w_v_m[l], dh_m).astype(BF16),
                     wg.astype(BF16), gb,
                     mlstm_norm_gain[l].reshape(1, d_m), mlstm_skip[l].reshape(1, d_m))

        y_f = _fox(qf.reshape(b, s, d_f), kf.reshape(b, s, d_f), vf.reshape(b, s, d_f), c_keys,
                   gf.reshape(b, s, d_f), fox_norm_gain[l].reshape(1, d_f))

        x2 = _out_proj(y_m.reshape(b * s, d_m), y_f.reshape(b * s, d_f), x2, w_out[l],
                       final_norm_gain.reshape(1, d), final_norm=(l == depth - 1))
    return x2.reshape(b, s, d)
```

```python
import functools
import math

import jax
import jax.numpy as jnp
from jax import lax
from jax.experimental import pallas as pl
from jax.experimental.pallas import tpu as pltpu

EPS = 1e-6
N_MLSTM_HEADS = 4
N_FOX_HEADS = 8
QKV_BLOCK = 4
CONV_WIDTH = 4

LANES = 128
SUBLANES = 8
VMEM_LIMIT = 56 * 1024 * 1024

IN_PROJ_ROWS = 256
OUT_PROJ_ROWS = 1024
OUT_PROJ_SUB = 256
MLSTM_CHUNK = 256
MLSTM_SEQS = 2
FOX_Q = 256
LOG2E = math.log2(math.e)

BF16 = jnp.bfloat16
F32 = jnp.float32
NEG_INF = float("-inf")


def _log_sigmoid(z):
    return jnp.minimum(z, 0.0) - jnp.log(1.0 + jnp.exp(-jnp.abs(z)))


def _sigmoid(z):
    return 1.0 / (1.0 + jnp.exp(-z))


def _silu(z):
    return z * _sigmoid(z)


def _split3(v):
    hi = v.astype(BF16)
    r = v - hi.astype(F32)
    mid = r.astype(BF16)
    lo = (r - mid.astype(F32)).astype(BF16)
    return hi, mid, lo


def _cumsum_rows(tri, v):
    w = v.shape[1]
    cat = jnp.concatenate(_split3(v), axis=1)
    cs = jnp.dot(tri, cat, preferred_element_type=F32)
    return cs[:, :w] + cs[:, w:2 * w] + cs[:, 2 * w:]


def _tril_mask(n):
    row = lax.broadcasted_iota(jnp.int32, (n, n), 0)
    col = lax.broadcasted_iota(jnp.int32, (n, n), 1)
    return col <= row


def _in_proj_kernel(x_ref, g_ref, w_ref, fb_ref,
                    xm_ref, gm_ref, q_ref, k_ref, v_ref, gf_ref, c_ref,
                    wb_ref, wfb_ref, tri_ref, carry_ref, *, tiles_per_seq, q_scale):
    t = pl.program_id(0)
    rows = x_ref.shape[0]
    d = x_ref.shape[1]
    n_main = wb_ref.shape[1]
    n_gate = w_ref.shape[0] - n_main

    @pl.when(t == 0)
    def _():
        tri_ref[...] = jnp.where(_tril_mask(rows), 1.0, 0.0).astype(BF16)
        for c0 in range(0, n_main, d):
            wb_ref[:, c0:c0 + d] = w_ref[c0:c0 + d, :].T.astype(BF16)
        wfb_ref[...] = jnp.zeros_like(wfb_ref)
        wfb_ref[0:n_gate, :] = w_ref[n_main:n_main + n_gate, :].astype(BF16)

    @pl.when(t % tiles_per_seq == 0)
    def _():
        carry_ref[...] = jnp.zeros_like(carry_ref)

    x = x_ref[...]
    ms = jnp.mean(x * x, axis=-1, keepdims=True)
    hn = (x * lax.rsqrt(ms + EPS) * g_ref[...]).astype(BF16)

    def proj(idx):
        return jnp.dot(hn, wb_ref[:, idx * d:(idx + 1) * d], preferred_element_type=F32)

    xm_ref[...] = proj(0).astype(xm_ref.dtype)
    gm_ref[...] = _silu(proj(1)).astype(gm_ref.dtype)
    q_ref[...] = (proj(2) * q_scale).astype(q_ref.dtype)
    k_ref[...] = proj(3).astype(k_ref.dtype)
    v_ref[...] = proj(4).astype(v_ref.dtype)
    gf_ref[...] = _silu(proj(5)).astype(gf_ref.dtype)

    f = lax.dot_general(hn, wfb_ref[...], (((1,), (1,)), ((), ())),
                        preferred_element_type=F32) + fb_ref[...]
    c = _cumsum_rows(tri_ref[...], _log_sigmoid(f)) + carry_ref[...]
    c_ref[...] = (c * LOG2E).T[0:c_ref.shape[0], :]
    carry_ref[...] = c[rows - 1:rows, :]


def _in_proj(x2, gain, w_in_t, f_bias, seq_len, q_scale):
    t_rows, d = x2.shape
    rows = IN_PROJ_ROWS
    assert t_rows % rows == 0 and seq_len % rows == 0
    n_main = 6 * d
    n_gate = w_in_t.shape[0] - n_main
    assert 0 < n_gate <= SUBLANES and w_in_t.shape[1] == d
    row_spec = pl.BlockSpec((rows, d), lambda t: (t, 0))
    const = lambda shape: pl.BlockSpec(shape, lambda t: (0, 0), pipeline_mode=pl.Buffered(1))
    kern = functools.partial(_in_proj_kernel, tiles_per_seq=seq_len // rows, q_scale=q_scale)
    return pl.pallas_call(
        kern,
        grid=(t_rows // rows,),
        in_specs=[row_spec, const((1, d)), const(w_in_t.shape), const((1, LANES))],
        out_specs=[row_spec] * 6 + [pl.BlockSpec((SUBLANES, rows), lambda t: (0, t))],
        out_shape=[jax.ShapeDtypeStruct((t_rows, d), BF16)] * 6
        + [jax.ShapeDtypeStruct((SUBLANES, t_rows), F32)],
        scratch_shapes=[pltpu.VMEM((d, n_main), BF16), pltpu.VMEM((LANES, d), BF16),
                        pltpu.VMEM((rows, rows), BF16), pltpu.VMEM((1, LANES), F32)],
        compiler_params=pltpu.CompilerParams(
            dimension_semantics=("arbitrary",), vmem_limit_bytes=VMEM_LIMIT),
        name="in_proj",
    )(x2, gain, w_in_t, f_bias)


def _mlstm_kernel(xm_ref, gm_ref, cw_ref, cb_ref, wq_ref, wk_ref, wv_ref, wg_ref, gb_ref,
                  ng_ref, sk_ref, y_ref,
                  tail_ref, c_sc, n_sc, m_sc, tri_ref, shift_ref, *, k_scale):
    ci = pl.program_id(1)
    n_seqs, L, _ = xm_ref.shape

    @pl.when(ci == 0)
    def _():
        c_sc[...] = jnp.zeros_like(c_sc)
        n_sc[...] = jnp.zeros_like(n_sc)
        m_sc[...] = jnp.zeros_like(m_sc)
        tail_ref[...] = jnp.zeros_like(tail_ref)
        row = lax.broadcasted_iota(jnp.int32, (L, L), 0)
        col = lax.broadcasted_iota(jnp.int32, (L, L), 1)
        tri_ref[...] = jnp.where(col <= row, 1.0, 0.0).astype(BF16)
        for j in range(1, CONV_WIDTH):
            shift_ref[j - 1] = jnp.where(col == row - j, 1.0, 0.0).astype(BF16)

    seqs = range(n_seqs)
    fronts = [_mlstm_front(xm_ref.at[i], cw_ref, cb_ref, wq_ref, wk_ref, wv_ref, wg_ref, gb_ref,
                           tail_ref.at[i], tri_ref, shift_ref, k_scale) for i in seqs]
    mixed = [_mlstm_mix(*fronts[i][1:], c_sc.at[i], n_sc.at[i], m_sc.at[i]) for i in seqs]
    for i in seqs:
        _mlstm_out(mixed[i], fronts[i][0], gm_ref.at[i], ng_ref, sk_ref, y_ref.at[i])


def _mlstm_front(xm_ref, cw_ref, cb_ref, wq_ref, wk_ref, wv_ref, wg_ref, gb_ref,
                 tail_ref, tri_ref, shift_ref, k_scale):
    L, d_m = xm_ref.shape
    dh = wq_ref.shape[1]
    n_heads = wq_ref.shape[0]
    halo = SUBLANES

    x_b = xm_ref[...]
    x_f = x_b.astype(F32)
    conv = cb_ref[...] + x_f * cw_ref[CONV_WIDTH - 1:CONV_WIDTH, :]
    edge = jnp.concatenate([tail_ref[...], jnp.zeros((halo, d_m), F32)], axis=0)
    head = jnp.zeros((halo, d_m), F32)
    for j in range(1, CONV_WIDTH):
        w_j = cw_ref[CONV_WIDTH - 1 - j:CONV_WIDTH - j, :]
        conv = conv + jnp.dot(shift_ref[j - 1], x_b, preferred_element_type=F32) * w_j
        head = head + edge[halo - j:2 * halo - j, :] * w_j
    conv = jnp.concatenate([conv[:halo] + head, conv[halo:]], axis=0)
    tail_ref[...] = x_f[L - halo:L, :]
    xc = _silu(conv)
    xc_b = xc.astype(BF16)

    qs, ks, vs = [], [], []
    for h in range(n_heads):
        sl = slice(h * dh, (h + 1) * dh)
        q = jnp.dot(xc_b[:, sl], wq_ref[h], preferred_element_type=F32)
        k = jnp.dot(xc_b[:, sl], wk_ref[h], preferred_element_type=F32) * k_scale
        v = jnp.dot(x_b[:, sl], wv_ref[h], preferred_element_type=F32)
        qs.append(q.astype(BF16))
        ks.append((k, k.astype(BF16)))
        vs.append(v.astype(BF16))
    qkv = jnp.concatenate(qs + [kb for _, kb in ks] + vs, axis=1)
    gates = jnp.dot(qkv, wg_ref[...], preferred_element_type=F32) + gb_ref[...]

    ig = gates[:, :LANES]
    bcum = _cumsum_rows(tri_ref[...], _log_sigmoid(gates[:, LANES:]))
    a_col = ig - bcum
    return xc, qs, ks, vs, a_col, bcum


def _mlstm_mix(qs, ks, vs, a_col, bcum, c_sc, n_sc, m_sc):
    n_heads = len(qs)
    L, dh = qs[0].shape
    a_rows = a_col.T
    causal = _tril_mask(L)
    ones_blk = jnp.ones((L, LANES), BF16)

    heads = range(n_heads)
    nt = (((1,), (1,)), ((), ()))
    m_prev = [m_sc[h][:, 0:1] for h in heads]
    dm = [jnp.where(causal, a_rows[h:h + 1, :], NEG_INF) for h in heads]
    m_run = [jnp.maximum(m_prev[h], jnp.max(dm[h], axis=1, keepdims=True)) for h in heads]
    w_inter = [jnp.exp(m_prev[h] - m_run[h]) for h in heads]
    s_b = [(lax.dot_general(qs[h], ks[h][1], nt, preferred_element_type=F32)
            * jnp.exp(dm[h] - m_run[h])).astype(BF16) for h in heads]
    num = [w_inter[h] * jnp.dot(qs[h], c_sc[h].astype(BF16), preferred_element_type=F32)
           + jnp.dot(s_b[h], vs[h], preferred_element_type=F32) for h in heads]
    qn = [lax.dot_general(qs[h], jnp.broadcast_to(n_sc[h], (LANES, dh)).astype(BF16), nt,
                          preferred_element_type=F32) for h in heads]
    den = [w_inter[h] * qn[h] + jnp.dot(s_b[h], ones_blk, preferred_element_type=F32)
           for h in heads]
    m_t = [bcum[:, h:h + 1] + m_run[h] for h in heads]
    inv = [1.0 / jnp.maximum(jnp.abs(den[h]), jnp.exp(-m_t[h])) for h in heads]
    hh = [num[h] * jnp.concatenate([inv[h]] * (dh // LANES), axis=1) for h in heads]

    for h in heads:
        m_last = m_run[h][L - 1:L, :]
        decay = jnp.exp(m_prev[h] - m_last)
        kw = ks[h][0] * jnp.exp(a_col[:, h:h + 1] - m_last)
        c_sc[h] = decay * c_sc[h] + lax.dot_general(
            kw.astype(BF16), vs[h], (((0,), (0,)), ((), ())), preferred_element_type=F32)
        n_sc[h] = decay * n_sc[h] + jnp.sum(kw, axis=0, keepdims=True)
        m_sc[h] = jnp.broadcast_to(m_t[h][L - 1:L, :], m_sc.shape[1:])
    return hh


def _mlstm_out(hh, xc, gm_ref, ng_ref, sk_ref, y_ref):
    heads = range(len(hh))
    dh = hh[0].shape[1]
    mu = [jnp.mean(hh[h], axis=1, keepdims=True) for h in heads]
    cen = [hh[h] - mu[h] for h in heads]
    var = [jnp.mean(cen[h] * cen[h], axis=1, keepdims=True) for h in heads]
    for h in heads:
        sl = slice(h * dh, (h + 1) * dh)
        hn = cen[h] * lax.rsqrt(var[h] + EPS) * ng_ref[:, sl] + sk_ref[:, sl] * xc[:, sl]
        y_ref[:, sl] = (hn * gm_ref[:, sl].astype(F32)).astype(y_ref.dtype)


def _mlstm(xm, gm, conv_w, conv_b, wq, wk, wv, wg, gb, ngain, skip):
    b, s, d = xm.shape
    L = MLSTM_CHUNK
    assert s % L == 0
    n_heads, dh, _ = wq.shape
    n_seqs = MLSTM_SEQS
    assert b % n_seqs == 0
    seq_spec = pl.BlockSpec((n_seqs, L, d), lambda bi, ci: (bi, ci, 0))

    def const(shape):
        nd = len(shape)
        return pl.BlockSpec(shape, lambda bi, ci: (0,) * nd)

    kern = functools.partial(_mlstm_kernel, k_scale=dh ** -0.5)
    return pl.pallas_call(
        kern,
        grid=(b // n_seqs, s // L),
        in_specs=[seq_spec, seq_spec, const(conv_w.shape), const(conv_b.shape), const(wq.shape),
                  const(wk.shape), const(wv.shape), const(wg.shape), const(gb.shape),
                  const(ngain.shape), const(skip.shape)],
        out_specs=seq_spec,
        out_shape=jax.ShapeDtypeStruct((b, s, d), BF16),
        scratch_shapes=[
            pltpu.VMEM((n_seqs, SUBLANES, d), F32),
            pltpu.VMEM((n_seqs, n_heads, dh, dh), F32),
            pltpu.VMEM((n_seqs, n_heads, 1, dh), F32),
            pltpu.VMEM((n_seqs, n_heads, 1, LANES), F32),
            pltpu.VMEM((L, L), BF16),
            pltpu.VMEM((CONV_WIDTH - 1, L, L), BF16),
        ],
        compiler_params=pltpu.CompilerParams(
            dimension_semantics=("arbitrary", "arbitrary"), vmem_limit_bytes=VMEM_LIMIT),
        name="mlstm",
    )(xm, gm, conv_w, conv_b, wq, wk, wv, wg, gb, ngain, skip)


def _fox_kernel(q_ref, k_ref, v_ref, cr_ref, gf_ref, g_ref, o_ref):
    s_len, dh = q_ref.shape
    tq = FOX_Q
    causal = _tril_mask(tq)
    gain = g_ref[...]
    c_keys = cr_ref[pl.ds(pl.program_id(1), 1), :]

    for qi in reversed(range(s_len // tq)):
        r0 = qi * tq
        kv = r0 + tq
        s = lax.dot_general(q_ref[r0:kv, :], k_ref[0:kv, :], (((1,), (1,)), ((), ())),
                            preferred_element_type=F32)
        s = s - c_keys[:, 0:kv]
        s_diag = jnp.where(causal, s[:, r0:], NEG_INF)
        s = s_diag if qi == 0 else jnp.concatenate([s[:, :r0], s_diag], axis=1)
        m = jnp.max(s, axis=1, keepdims=True)
        p = jnp.exp2(s - m).astype(BF16)
        v_ext = jnp.concatenate([v_ref[0:kv, :], jnp.ones((kv, dh), BF16)], axis=1)
        acc = jnp.dot(p, v_ext, preferred_element_type=F32)

        out = acc[:, :dh] / acc[:, dh:]
        ms = jnp.mean(out * out, axis=1, keepdims=True)
        o_ref[r0:kv, :] = (out * lax.rsqrt(ms + EPS) * gain
                           * gf_ref[r0:kv, :].astype(F32)).astype(o_ref.dtype)


def _fox(q, k, v, c_row, gf, gain):
    b, s, d = q.shape
    n_heads = N_FOX_HEADS
    dh = d // n_heads
    assert dh == LANES and s % FOX_Q == 0
    head_spec = pl.BlockSpec((None, s, dh), lambda bi, hi: (bi, 0, hi))
    return pl.pallas_call(
        _fox_kernel,
        grid=(b, n_heads),
        in_specs=[head_spec, head_spec, head_spec,
                  pl.BlockSpec((c_row.shape[0], s), lambda bi, hi: (0, bi)),
                  head_spec,
                  pl.BlockSpec((1, dh), lambda bi, hi: (0, hi))],
        out_specs=head_spec,
        out_shape=jax.ShapeDtypeStruct((b, s, d), BF16),
        compiler_params=pltpu.CompilerParams(
            dimension_semantics=("arbitrary", "arbitrary"), vmem_limit_bytes=VMEM_LIMIT),
        name="fox",
    )(q, k, v, c_row, gf, gain)


def _out_proj_kernel(ym_ref, yf_ref, x_ref, w_ref, g_ref, o_ref, wb_ref, *, final_norm):
    d_m = ym_ref.shape[1]

    @pl.when(pl.program_id(0) == 0)
    def _():
        wb_ref[...] = w_ref[...].astype(BF16)

    rows = x_ref.shape[0]
    for r0 in range(0, rows, OUT_PROJ_SUB):
        sl = slice(r0, r0 + OUT_PROJ_SUB)
        y = jnp.dot(ym_ref[sl, :], wb_ref[0:d_m, :], preferred_element_type=F32)
        y = y + jnp.dot(yf_ref[sl, :], wb_ref[d_m:, :], preferred_element_type=F32)
        r = x_ref[sl, :] + y
        if final_norm:
            ms = jnp.mean(r * r, axis=-1, keepdims=True)
            r = r * lax.rsqrt(ms + EPS) * g_ref[...]
        o_ref[sl, :] = r


def _out_proj(ym, yf, x2, w_out, gain, final_norm):
    t_rows, d = x2.shape
    rows = OUT_PROJ_ROWS
    assert t_rows % rows == 0
    row = lambda width: pl.BlockSpec((rows, width), lambda t: (t, 0))
    const = lambda shape: pl.BlockSpec(shape, lambda t: (0, 0), pipeline_mode=pl.Buffered(1))
    kern = functools.partial(_out_proj_kernel, final_norm=final_norm)
    return pl.pallas_call(
        kern,
        grid=(t_rows // rows,),
        in_specs=[row(ym.shape[1]), row(yf.shape[1]), row(d), const(w_out.shape), const((1, d))],
        out_specs=row(d),
        out_shape=jax.ShapeDtypeStruct((t_rows, d), F32),
        scratch_shapes=[pltpu.VMEM(w_out.shape, BF16)],
        compiler_params=pltpu.CompilerParams(
            dimension_semantics=("arbitrary",), vmem_limit_bytes=VMEM_LIMIT),
        name="out_proj",
    )(ym, yf, x2, w_out, gain)


def _block_diag(w, dh):
    nb, bs, _ = w.shape
    d = nb * bs
    tiled = jnp.broadcast_to(w.reshape(d, 1, bs), (d, dh // bs, bs)).reshape(d, dh)
    r = lax.broadcasted_iota(jnp.int32, (d, dh), 0)
    c = lax.broadcasted_iota(jnp.int32, (d, dh), 1)
    keep = (r % dh) // bs == c // bs
    return jnp.where(keep, tiled, 0.0).reshape(d // dh, dh, dh)


def kernel(x, norm_gain, w_in, conv_w, conv_b, w_q_m, w_k_m, w_v_m, w_igate, b_igate, w_fgate,
           b_fgate, mlstm_norm_gain, mlstm_skip, fox_forget_bias, fox_norm_gain, w_out,
           final_norm_gain):
    b, s, d = x.shape
    depth = norm_gain.shape[0]
    d_m = conv_w.shape[-1]
    d_f = fox_norm_gain.shape[-1]
    dh_m = d_m // N_MLSTM_HEADS
    dh_f = d_f // N_FOX_HEADS
    n_main = 2 * d_m + 4 * d_f
    assert d_m == d and d_f == d

    x2 = x.reshape(b * s, d)
    for l in range(depth):
        assert w_in.shape[2] == n_main + N_FOX_HEADS
        f_bias = jnp.pad(fox_forget_bias[l], (0, LANES - N_FOX_HEADS)).reshape(1, LANES)
        xm, gm, qf, kf, vf, gf, c_keys = _in_proj(
            x2, norm_gain[l].reshape(1, d), w_in[l].T, f_bias, s, dh_f ** -0.5 * LOG2E)

        lane_pad = ((0, 0), (0, LANES - N_MLSTM_HEADS))
        wg = jnp.concatenate([jnp.pad(w_igate[l], lane_pad), jnp.pad(w_fgate[l], lane_pad)], axis=1)
        gb = jnp.concatenate([jnp.pad(b_igate[l][None], lane_pad),
                              jnp.pad(b_fgate[l][None], lane_pad)], axis=1)
        y_m = _mlstm(xm.reshape(b, s, d_m), gm.reshape(b, s, d_m), conv_w[l],
                     conv_b[l].reshape(1, d_m),
                     _block_diag(w_q_m[l], dh_m).astype(BF16),
                     _block_diag(w_k_m[l], dh_m).astype(BF16),
                     _block_diag(w_v_m[l], dh_m).astype(BF16),
                     wg.astype(BF16), gb,
                     mlstm_norm_gain[l].reshape(1, d_m), mlstm_skip[l].reshape(1, d_m))

        y_f = _fox(qf.reshape(b, s, d_f), kf.reshape(b, s, d_f), vf.reshape(b, s, d_f), c_keys,
                   gf.reshape(b, s, d_f), fox_norm_gain[l].reshape(1, d_f))

        x2 = _out_proj(y_m.reshape(b * s, d_m), y_f.reshape(b * s, d_f), x2, w_out[l],
                       final_norm_gain.reshape(1, d), final_norm=(l == depth - 1))
    return x2.reshape(b, s, d)
```

```python
import functools
import math

import jax
import jax.numpy as jnp
from jax import lax
from jax.experimental import pallas as pl
from jax.experimental.pallas import tpu as pltpu

EPS = 1e-6
N_MLSTM_HEADS = 4
N_FOX_HEADS = 8
QKV_BLOCK = 4
CONV_WIDTH = 4

LANES = 128
SUBLANES = 8
VMEM_LIMIT = 56 * 1024 * 1024

IN_PROJ_ROWS = 256
OUT_PROJ_ROWS = 1024
OUT_PROJ_SUB = 256
MLSTM_CHUNK = 256
MLSTM_SEQS = 2
FOX_Q = 256
LOG2E = math.log2(math.e)

BF16 = jnp.bfloat16
F32 = jnp.float32
NEG_INF = float("-inf")


def _log_sigmoid(z):
    return jnp.minimum(z, 0.0) - jnp.log(1.0 + jnp.exp(-jnp.abs(z)))


def _sigmoid(z):
    return 1.0 / (1.0 + jnp.exp(-z))


def _silu(z):
    return z * _sigmoid(z)


def _split3(v):
    hi = v.astype(BF16)
    r = v - hi.astype(F32)
    mid = r.astype(BF16)
    lo = (r - mid.astype(F32)).astype(BF16)
    return hi, mid, lo


def _cumsum_rows(tri, v):
    w = v.shape[1]
    cat = jnp.concatenate(_split3(v), axis=1)
    cs = jnp.dot(tri, cat, preferred_element_type=F32)
    return cs[:, :w] + cs[:, w:2 * w] + cs[:, 2 * w:]


def _tril_mask(n):
    row = lax.broadcasted_iota(jnp.int32, (n, n), 0)
    col = lax.broadcasted_iota(jnp.int32, (n, n), 1)
    return col <= row


def _in_proj_kernel(x_ref, g_ref, w_ref, fb_ref,
                    xm_ref, gm_ref, q_ref, k_ref, v_ref, gf_ref, c_ref,
                    wb_ref, wfb_ref, tri_ref, carry_ref, *, tiles_per_seq, q_scale):
    t = pl.program_id(0)
    rows = x_ref.shape[0]
    d = x_ref.shape[1]
    n_main = wb_ref.shape[1]
    n_gate = w_ref.shape[0] - n_main

    @pl.when(t == 0)
    def _():
        tri_ref[...] = jnp.where(_tril_mask(rows), 1.0, 0.0).astype(BF16)
        for c0 in range(0, n_main, d):
            wb_ref[:, c0:c0 + d] = w_ref[c0:c0 + d, :].T.astype(BF16)
        wfb_ref[...] = jnp.zeros_like(wfb_ref)
        wfb_ref[0:n_gate, :] = w_ref[n_main:n_main + n_gate, :].astype(BF16)

    @pl.when(t % tiles_per_seq == 0)
    def _():
        carry_ref[...] = jnp.zeros_like(carry_ref)

    x = x_ref[...]
    ms = jnp.mean(x * x, axis=-1, keepdims=True)
    hn = (x * lax.rsqrt(ms + EPS) * g_ref[...]).astype(BF16)

    def proj(idx):
        return jnp.dot(hn, wb_ref[:, idx * d:(idx + 1) * d], preferred_element_type=F32)

    xm_ref[...] = proj(0).astype(xm_ref.dtype)
    gm_ref[...] = _silu(proj(1)).astype(gm_ref.dtype)
    q_ref[...] = (proj(2) * q_scale).astype(q_ref.dtype)
    k_ref[...] = proj(3).astype(k_ref.dtype)
    v_ref[...] = proj(4).astype(v_ref.dtype)
    gf_ref[...] = _silu(proj(5)).astype(gf_ref.dtype)

    f = lax.dot_general(hn, wfb_ref[...], (((1,), (1,)), ((), ())),
                        preferred_element_type=F32) + fb_ref[...]
    c = _cumsum_rows(tri_ref[...], _log_sigmoid(f)) + carry_ref[...]
    c_ref[...] = (c * LOG2E).T[0:c_ref.shape[0], :]
    carry_ref[...] = c[rows - 1:rows, :]


def _in_proj(x2, gain, w_in_t, f_bias, seq_len, q_scale):
    t_rows, d = x2.shape
    rows = IN_PROJ_ROWS
    assert t_rows % rows == 0 and seq_len % rows == 0
    n_main = 6 * d
    n_gate = w_in_t.shape[0] - n_main
    assert 0 < n_gate <= SUBLANES and w_in_t.shape[1] == d
    row_spec = pl.BlockSpec((rows, d), lambda t: (t, 0))
    const = lambda shape: pl.BlockSpec(shape, lambda t: (0, 0), pipeline_mode=pl.Buffered(1))
    kern = functools.partial(_in_proj_kernel, tiles_per_seq=seq_len // rows, q_scale=q_scale)
    return pl.pallas_call(
        kern,
        grid=(t_rows // rows,),
        in_specs=[row_spec, const((1, d)), const(w_in_t.shape), const((1, LANES))],
        out_specs=[row_spec] * 6 + [pl.BlockSpec((SUBLANES, rows), lambda t: (0, t))],
        out_shape=[jax.ShapeDtypeStruct((t_rows, d), BF16)] * 6
        + [jax.ShapeDtypeStruct((SUBLANES, t_rows), F32)],
        scratch_shapes=[pltpu.VMEM((d, n_main), BF16), pltpu.VMEM((LANES, d), BF16),
                        pltpu.VMEM((rows, rows), BF16), pltpu.VMEM((1, LANES), F32)],
        compiler_params=pltpu.CompilerParams(
            dimension_semantics=("arbitrary",), vmem_limit_bytes=VMEM_LIMIT),
        name="in_proj",
    )(x2, gain, w_in_t, f_bias)


def _mlstm_kernel(xm_ref, gm_ref, cw_ref, cb_ref, wq_ref, wk_ref, wv_ref, wg_ref, gb_ref,
                  ng_ref, sk_ref, y_ref,
                  tail_ref, c_sc, n_sc, m_sc, tri_ref, shift_ref, gw_ref, *, k_scale):
    ci = pl.program_id(1)
    n_seqs, L, d_m = xm_ref.shape
    n_heads, dh, _ = wq_ref.shape

    @pl.when(ci == 0)
    def _():
        for h in range(n_heads):
            sl = slice(h * dh, (h + 1) * dh)
            g_q = jnp.dot(wq_ref[h], wg_ref[h * dh:(h + 1) * dh, :], preferred_element_type=F32)
            g_k = jnp.dot(wk_ref[h], wg_ref[d_m + h * dh:d_m + (h + 1) * dh, :],
                          preferred_element_type=F32)
            g_v = jnp.dot(wv_ref[h], wg_ref[2 * d_m + h * dh:2 * d_m + (h + 1) * dh, :],
                          preferred_element_type=F32)
            gw_ref[0, sl, :] = (g_q + g_k * k_scale).astype(BF16)
            gw_ref[1, sl, :] = g_v.astype(BF16)
        c_sc[...] = jnp.zeros_like(c_sc)
        n_sc[...] = jnp.zeros_like(n_sc)
        m_sc[...] = jnp.zeros_like(m_sc)
        tail_ref[...] = jnp.zeros_like(tail_ref)
        row = lax.broadcasted_iota(jnp.int32, (L, L), 0)
        col = lax.broadcasted_iota(jnp.int32, (L, L), 1)
        tri_ref[...] = jnp.where(col <= row, 1.0, 0.0).astype(BF16)
        for j in range(1, CONV_WIDTH):
            shift_ref[j - 1] = jnp.where(col == row - j, 1.0, 0.0).astype(BF16)

    seqs = range(n_seqs)
    fronts = [_mlstm_front(xm_ref.at[i], cw_ref, cb_ref, wq_ref, wk_ref, wv_ref, gw_ref, gb_ref,
                           tail_ref.at[i], tri_ref, shift_ref, k_scale) for i in seqs]
    mixed = [_mlstm_mix(*fronts[i][1:], c_sc.at[i], n_sc.at[i], m_sc.at[i]) for i in seqs]
    for i in seqs:
        _mlstm_out(mixed[i], fronts[i][0], gm_ref.at[i], ng_ref, sk_ref, y_ref.at[i])


def _mlstm_front(xm_ref, cw_ref, cb_ref, wq_ref, wk_ref, wv_ref, gw_ref, gb_ref,
                 tail_ref, tri_ref, shift_ref, k_scale):
    L, d_m = xm_ref.shape
    dh = wq_ref.shape[1]
    n_heads = wq_ref.shape[0]
    halo = SUBLANES

    x_b = xm_ref[...]
    x_f = x_b.astype(F32)
    conv = cb_ref[...] + x_f * cw_ref[CONV_WIDTH - 1:CONV_WIDTH, :]
    edge = jnp.concatenate([tail_ref[...], jnp.zeros((halo, d_m), F32)], axis=0)
    head = jnp.zeros((halo, d_m), F32)
    for j in range(1, CONV_WIDTH):
        w_j = cw_ref[CONV_WIDTH - 1 - j:CONV_WIDTH - j, :]
        conv = conv + jnp.dot(shift_ref[j - 1], x_b, preferred_element_type=F32) * w_j
        head = head + edge[halo - j:2 * halo - j, :] * w_j
    conv = jnp.concatenate([conv[:halo] + head, conv[halo:]], axis=0)
    tail_ref[...] = x_f[L - halo:L, :]
    xc = _silu(conv)
    xc_b = xc.astype(BF16)

    gates = (jnp.dot(xc_b, gw_ref[0], preferred_element_type=F32)
             + jnp.dot(x_b, gw_ref[1], preferred_element_type=F32) + gb_ref[...])
    qs, ks, vs = [], [], []
    for h in range(n_heads):
        sl = slice(h * dh, (h + 1) * dh)
        q = jnp.dot(xc_b[:, sl], wq_ref[h], preferred_element_type=F32)
        k = jnp.dot(xc_b[:, sl], wk_ref[h], preferred_element_type=F32) * k_scale
        v = jnp.dot(x_b[:, sl], wv_ref[h], preferred_element_type=F32)
        qs.append(q.astype(BF16))
        ks.append((k, k.astype(BF16)))
        vs.append(v.astype(BF16))

    ig = gates[:, :LANES]
    bcum = _cumsum_rows(tri_ref[...], _log_sigmoid(gates[:, LANES:]))
    a_col = ig - bcum
    return xc, qs, ks, vs, a_col, bcum


def _mlstm_mix(qs, ks, vs, a_col, bcum, c_sc, n_sc, m_sc):
    n_heads = len(qs)
    L, dh = qs[0].shape
    a_rows = a_col.T
    causal = _tril_mask(L)
    ones_blk = jnp.ones((L, LANES), BF16)

    heads = range(n_heads)
    nt = (((1,), (1,)), ((), ()))
    m_prev = [m_sc[h][:, 0:1] for h in heads]
    dm = [jnp.where(causal, a_rows[h:h + 1, :], NEG_INF) for h in heads]
    m_run = [jnp.maximum(m_prev[h], jnp.max(dm[h], axis=1, keepdims=True)) for h in heads]
    w_inter = [jnp.exp(m_prev[h] - m_run[h]) for h in heads]
    s_b = [(lax.dot_general(qs[h], ks[h][1], nt, preferred_element_type=F32)
            * jnp.exp(dm[h] - m_run[h])).astype(BF16) for h in heads]
    num = [w_inter[h] * jnp.dot(qs[h], c_sc[h].astype(BF16), preferred_element_type=F32)
           + jnp.dot(s_b[h], vs[h], preferred_element_type=F32) for h in heads]
    qn = [lax.dot_general(qs[h], jnp.broadcast_to(n_sc[h], (LANES, dh)).astype(BF16), nt,
                          preferred_element_type=F32) for h in heads]
    den = [w_inter[h] * qn[h] + jnp.dot(s_b[h], ones_blk, preferred_element_type=F32)
           for h in heads]
    m_t = [bcum[:, h:h + 1] + m_run[h] for h in heads]
    inv = [1.0 / jnp.maximum(jnp.abs(den[h]), jnp.exp(-m_t[h])) for h in heads]
    hh = [num[h] * jnp.concatenate([inv[h]] * (dh // LANES), axis=1) for h in heads]

    for h in heads:
        m_last = m_run[h][L - 1:L, :]
        decay = jnp.exp(m_prev[h] - m_last)
        kw = ks[h][0] * jnp.exp(a_col[:, h:h + 1] - m_last)
        c_sc[h] = decay * c_sc[h] + lax.dot_general(
            kw.astype(BF16), vs[h], (((0,), (0,)), ((), ())), preferred_element_type=F32)
        n_sc[h] = decay * n_sc[h] + jnp.sum(kw, axis=0, keepdims=True)
        m_sc[h] = jnp.broadcast_to(m_t[h][L - 1:L, :], m_sc.shape[1:])
    return hh


def _mlstm_out(hh, xc, gm_ref, ng_ref, sk_ref, y_ref):
    heads = range(len(hh))
    dh = hh[0].shape[1]
    mu = [jnp.mean(hh[h], axis=1, keepdims=True) for h in heads]
    cen = [hh[h] - mu[h] for h in heads]
    var = [jnp.mean(cen[h] * cen[h], axis=1, keepdims=True) for h in heads]
    for h in heads:
        sl = slice(h * dh, (h + 1) * dh)
        hn = cen[h] * lax.rsqrt(var[h] + EPS) * ng_ref[:, sl] + sk_ref[:, sl] * xc[:, sl]
        y_ref[:, sl] = (hn * gm_ref[:, sl].astype(F32)).astype(y_ref.dtype)


def _mlstm(xm, gm, conv_w, conv_b, wq, wk, wv, wg, gb, ngain, skip):
    b, s, d = xm.shape
    L = MLSTM_CHUNK
    assert s % L == 0
    n_heads, dh, _ = wq.shape
    n_seqs = MLSTM_SEQS
    assert b % n_seqs == 0
    seq_spec = pl.BlockSpec((n_seqs, L, d), lambda bi, ci: (bi, ci, 0))

    def const(shape):
        nd = len(shape)
        return pl.BlockSpec(shape, lambda bi, ci: (0,) * nd)

    kern = functools.partial(_mlstm_kernel, k_scale=dh ** -0.5)
    return pl.pallas_call(
        kern,
        grid=(b // n_seqs, s // L),
        in_specs=[seq_spec, seq_spec, const(conv_w.shape), const(conv_b.shape), const(wq.shape),
                  const(wk.shape), const(wv.shape), const(wg.shape), const(gb.shape),
                  const(ngain.shape), const(skip.shape)],
        out_specs=seq_spec,
        out_shape=jax.ShapeDtypeStruct((b, s, d), BF16),
        scratch_shapes=[
            pltpu.VMEM((n_seqs, SUBLANES, d), F32),
            pltpu.VMEM((n_seqs, n_heads, dh, dh), F32),
            pltpu.VMEM((n_seqs, n_heads, 1, dh), F32),
            pltpu.VMEM((n_seqs, n_heads, 1, LANES), F32),
            pltpu.VMEM((L, L), BF16),
            pltpu.VMEM((CONV_WIDTH - 1, L, L), BF16),
            pltpu.VMEM((2, d, wg.shape[1]), BF16),
        ],
        compiler_params=pltpu.CompilerParams(
            dimension_semantics=("arbitrary", "arbitrary"), vmem_limit_bytes=VMEM_LIMIT),
        name="mlstm",
    )(xm, gm, conv_w, conv_b, wq, wk, wv, wg, gb, ngain, skip)


def _fox_kernel(q_ref, k_ref, v_ref, cr_ref, gf_ref, g_ref, o_ref):
    s_len, dh = q_ref.shape
    tq = FOX_Q
    causal = _tril_mask(tq)
    gain = g_ref[...]
    c_keys = cr_ref[pl.ds(pl.program_id(1), 1), :]

    for qi in reversed(range(s_len // tq)):
        r0 = qi * tq
        kv = r0 + tq
        s = lax.dot_general(q_ref[r0:kv, :], k_ref[0:kv, :], (((1,), (1,)), ((), ())),
                            preferred_element_type=F32)
        s = s - c_keys[:, 0:kv]
        s_diag = jnp.where(causal, s[:, r0:], NEG_INF)
        s = s_diag if qi == 0 else jnp.concatenate([s[:, :r0], s_diag], axis=1)
        m = jnp.max(s, axis=1, keepdims=True)
        p = jnp.exp2(s - m).astype(BF16)
        v_ext = jnp.concatenate([v_ref[0:kv, :], jnp.ones((kv, dh), BF16)], axis=1)
        acc = jnp.dot(p, v_ext, preferred_element_type=F32)

        out = acc[:, :dh] / acc[:, dh:]
        ms = jnp.mean(out * out, axis=1, keepdims=True)
        o_ref[r0:kv, :] = (out * lax.rsqrt(ms + EPS) * gain
                           * gf_ref[r0:kv, :].astype(F32)).astype(o_ref.dtype)


def _fox(q, k, v, c_row, gf, gain):
    b, s, d = q.shape
    n_heads = N_FOX_HEADS
    dh = d // n_heads
    assert dh == LANES and s % FOX_Q == 0
    head_spec = pl.BlockSpec((None, s, dh), lambda bi, hi: (bi, 0, hi))
    return pl.pallas_call(
        _fox_kernel,
        grid=(b, n_heads),
        in_specs=[head_spec, head_spec, head_spec,
                  pl.BlockSpec((c_row.shape[0], s), lambda bi, hi: (0, bi)),
                  head_spec,
                  pl.BlockSpec((1, dh), lambda bi, hi: (0, hi))],
        out_specs=head_spec,
        out_shape=jax.ShapeDtypeStruct((b, s, d), BF16),
        compiler_params=pltpu.CompilerParams(
            dimension_semantics=("arbitrary", "arbitrary"), vmem_limit_bytes=VMEM_LIMIT),
        name="fox",
    )(q, k, v, c_row, gf, gain)


def _out_proj_kernel(ym_ref, yf_ref, x_ref, w_ref, g_ref, o_ref, wb_ref, *, final_norm):
    d_m = ym_ref.shape[1]

    @pl.when(pl.program_id(0) == 0)
    def _():
        wb_ref[...] = w_ref[...].astype(BF16)

    rows = x_ref.shape[0]
    for r0 in range(0, rows, OUT_PROJ_SUB):
        sl = slice(r0, r0 + OUT_PROJ_SUB)
        y = jnp.dot(ym_ref[sl, :], wb_ref[0:d_m, :], preferred_element_type=F32)
        y = y + jnp.dot(yf_ref[sl, :], wb_ref[d_m:, :], preferred_element_type=F32)
        r = x_ref[sl, :] + y
        if final_norm:
            ms = jnp.mean(r * r, axis=-1, keepdims=True)
            r = r * lax.rsqrt(ms + EPS) * g_ref[...]
        o_ref[sl, :] = r


def _out_proj(ym, yf, x2, w_out, gain, final_norm):
    t_rows, d = x2.shape
    rows = OUT_PROJ_ROWS
    assert t_rows % rows == 0
    row = lambda width: pl.BlockSpec((rows, width), lambda t: (t, 0))
    const = lambda shape: pl.BlockSpec(shape, lambda t: (0, 0), pipeline_mode=pl.Buffered(1))
    kern = functools.partial(_out_proj_kernel, final_norm=final_norm)
    return pl.pallas_call(
        kern,
        grid=(t_rows // rows,),
        in_specs=[row(ym.shape[1]), row(yf.shape[1]), row(d), const(w_out.shape), const((1, d))],
        out_specs=row(d),
        out_shape=jax.ShapeDtypeStruct((t_rows, d), F32),
        scratch_shapes=[pltpu.VMEM(w_out.shape, BF16)],
        compiler_params=pltpu.CompilerParams(
            dimension_semantics=("arbitrary",), vmem_limit_bytes=VMEM_LIMIT),
        name="out_proj",
    )(ym, yf, x2, w_out, gain)


def _block_diag(w, dh):
    nb, bs, _ = w.shape
    d = nb * bs
    tiled = jnp.broadcast_to(w.reshape(d, 1, bs), (d, dh // bs, bs)).reshape(d, dh)
    r = lax.broadcasted_iota(jnp.int32, (d, dh), 0)
    c = lax.broadcasted_iota(jnp.int32, (d, dh), 1)
    keep = (r % dh) // bs == c // bs
    return jnp.where(keep, tiled, 0.0).reshape(d // dh, dh, dh)


def kernel(x, norm_gain, w_in, conv_w, conv_b, w_q_m, w_k_m, w_v_m, w_igate, b_igate, w_fgate,
           b_fgate, mlstm_norm_gain, mlstm_skip, fox_forget_bias, fox_norm_gain, w_out,
           final_norm_gain):
    b, s, d = x.shape
    depth = norm_gain.shape[0]
    d_m = conv_w.shape[-1]
    d_f = fox_norm_gain.shape[-1]
    dh_m = d_m // N_MLSTM_HEADS
    dh_f = d_f // N_FOX_HEADS
    n_main = 2 * d_m + 4 * d_f
    assert d_m == d and d_f == d

    x2 = x.reshape(b * s, d)
    for l in range(depth):
        assert w_in.shape[2] == n_main + N_FOX_HEADS
        f_bias = jnp.pad(fox_forget_bias[l], (0, LANES - N_FOX_HEADS)).reshape(1, LANES)
        xm, gm, qf, kf, vf, gf, c_keys = _in_proj(
            x2, norm_gain[l].reshape(1, d), w_in[l].T, f_bias, s, dh_f ** -0.5 * LOG2E)

        lane_pad = ((0, 0), (0, LANES - N_MLSTM_HEADS))
        wg = jnp.concatenate([jnp.pad(w_igate[l], lane_pad), jnp.pad(w_fgate[l], lane_pad)], axis=1)
        gb = jnp.concatenate([jnp.pad(b_igate[l][None], lane_pad),
                              jnp.pad(b_fgate[l][None], lane_pad)], axis=1)
        y_m = _mlstm(xm.reshape(b, s, d_m), gm.reshape(b, s, d_m), conv_w[l],
                     conv_b[l].reshape(1, d_m),
                     _block_diag(w_q_m[l], dh_m).astype(BF16),
                     _block_diag(w_k_m[l], dh_m).astype(BF16),
                     _block_diag(w_v_m[l], dh_m).astype(BF16),
                     wg.astype(BF16), gb,
                     mlstm_norm_gain[l].reshape(1, d_m), mlstm_skip[l].reshape(1, d_m))

        y_f = _fox(qf.reshape(b, s, d_f), kf.reshape(b, s, d_f), vf.reshape(b, s, d_f), c_keys,
                   gf.reshape(b, s, d_f), fox_norm_gain[l].reshape(1, d_f))

        x2 = _out_proj(y_m.reshape(b * s, d_m), y_f.reshape(b * s, d_f), x2, w_out[l],
                       final_norm_gain.reshape(1, d), final_norm=(l == depth - 1))
    return x2.reshape(b, s, d)
```

```python
import functools
import math

import jax
import jax.numpy as jnp
from jax import lax
from jax.experimental import pallas as pl
from jax.experimental.pallas import tpu as pltpu

EPS = 1e-6
N_MLSTM_HEADS = 4
N_FOX_HEADS = 8
QKV_BLOCK = 4
CONV_WIDTH = 4

LANES = 128
SUBLANES = 8
VMEM_LIMIT = 56 * 1024 * 1024

IN_PROJ_ROWS = 256
OUT_PROJ_ROWS = 1024
OUT_PROJ_SUB = 256
MLSTM_CHUNK = 256
MLSTM_SEQS = 2
FOX_Q = 512
LOG2E = math.log2(math.e)

BF16 = jnp.bfloat16
F32 = jnp.float32
NEG_INF = float("-inf")


def _log_sigmoid(z):
    return jnp.minimum(z, 0.0) - jnp.log(1.0 + jnp.exp(-jnp.abs(z)))


def _sigmoid(z):
    return 1.0 / (1.0 + jnp.exp(-z))


def _silu(z):
    return z * _sigmoid(z)


def _split3(v):
    hi = v.astype(BF16)
    r = v - hi.astype(F32)
    mid = r.astype(BF16)
    lo = (r - mid.astype(F32)).astype(BF16)
    return hi, mid, lo


def _cumsum_rows(tri, v):
    w = v.shape[1]
    cat = jnp.concatenate(_split3(v), axis=1)
    cs = jnp.dot(tri, cat, preferred_element_type=F32)
    return cs[:, :w] + cs[:, w:2 * w] + cs[:, 2 * w:]


def _tril_mask(n):
    row = lax.broadcasted_iota(jnp.int32, (n, n), 0)
    col = lax.broadcasted_iota(jnp.int32, (n, n), 1)
    return col <= row


def _in_proj_kernel(x_ref, g_ref, w_ref, fb_ref,
                    xm_ref, gm_ref, q_ref, k_ref, v_ref, gf_ref, c_ref,
                    wb_ref, wfb_ref, tri_ref, carry_ref, *, tiles_per_seq, q_scale):
    t = pl.program_id(0)
    rows = x_ref.shape[0]
    d = x_ref.shape[1]
    n_main = wb_ref.shape[1]
    n_gate = w_ref.shape[0] - n_main

    @pl.when(t == 0)
    def _():
        tri_ref[...] = jnp.where(_tril_mask(rows), 1.0, 0.0).astype(BF16)
        for c0 in range(0, n_main, d):
            wb_ref[:, c0:c0 + d] = w_ref[c0:c0 + d, :].T.astype(BF16)
        wfb_ref[...] = jnp.zeros_like(wfb_ref)
        wfb_ref[0:n_gate, :] = w_ref[n_main:n_main + n_gate, :].astype(BF16)

    @pl.when(t % tiles_per_seq == 0)
    def _():
        carry_ref[...] = jnp.zeros_like(carry_ref)

    x = x_ref[...]
    ms = jnp.mean(x * x, axis=-1, keepdims=True)
    hn = (x * lax.rsqrt(ms + EPS) * g_ref[...]).astype(BF16)

    def proj(idx):
        return jnp.dot(hn, wb_ref[:, idx * d:(idx + 1) * d], preferred_element_type=F32)

    half = rows // 2
    f = jnp.concatenate(
        [lax.dot_general(hn[r0:r0 + half, :], wfb_ref[...], (((1,), (1,)), ((), ())),
                         preferred_element_type=F32) for r0 in (0, half)], axis=0) + fb_ref[...]
    c = _cumsum_rows(tri_ref[...], _log_sigmoid(f)) + carry_ref[...]
    c_ref[...] = (c * LOG2E).T[0:c_ref.shape[0], :]
    carry_ref[...] = c[rows - 1:rows, :]

    xm_ref[...] = proj(0).astype(xm_ref.dtype)
    gm_ref[...] = _silu(proj(1)).astype(gm_ref.dtype)
    q_ref[...] = (proj(2) * q_scale).astype(q_ref.dtype)
    k_ref[...] = proj(3).astype(k_ref.dtype)
    v_ref[...] = proj(4).astype(v_ref.dtype)
    gf_ref[...] = _silu(proj(5)).astype(gf_ref.dtype)


def _in_proj(x2, gain, w_in_t, f_bias, seq_len, q_scale):
    t_rows, d = x2.shape
    rows = IN_PROJ_ROWS
    assert t_rows % rows == 0 and seq_len % rows == 0
    n_main = 6 * d
    n_gate = w_in_t.shape[0] - n_main
    assert 0 < n_gate <= SUBLANES and w_in_t.shape[1] == d
    row_spec = pl.BlockSpec((rows, d), lambda t: (t, 0))
    const = lambda shape: pl.BlockSpec(shape, lambda t: (0, 0), pipeline_mode=pl.Buffered(1))
    kern = functools.partial(_in_proj_kernel, tiles_per_seq=seq_len // rows, q_scale=q_scale)
    return pl.pallas_call(
        kern,
        grid=(t_rows // rows,),
        in_specs=[row_spec, const((1, d)), const(w_in_t.shape), const((1, LANES))],
        out_specs=[row_spec] * 6 + [pl.BlockSpec((SUBLANES, rows), lambda t: (0, t))],
        out_shape=[jax.ShapeDtypeStruct((t_rows, d), BF16)] * 6
        + [jax.ShapeDtypeStruct((SUBLANES, t_rows), F32)],
        scratch_shapes=[pltpu.VMEM((d, n_main), BF16), pltpu.VMEM((LANES, d), BF16),
                        pltpu.VMEM((rows, rows), BF16), pltpu.VMEM((1, LANES), F32)],
        compiler_params=pltpu.CompilerParams(
            dimension_semantics=("arbitrary",), vmem_limit_bytes=VMEM_LIMIT),
        name="in_proj",
    )(x2, gain, w_in_t, f_bias)


def _mlstm_kernel(xm_ref, gm_ref, cw_ref, cb_ref, wq_ref, wk_ref, wv_ref, wg_ref, gb_ref,
                  ng_ref, sk_ref, y_ref,
                  tail_ref, c_sc, n_sc, m_sc, tri_ref, shift_ref, gw_ref, *, k_scale):
    ci = pl.program_id(1)
    n_seqs, L, d_m = xm_ref.shape
    n_heads, dh, _ = wq_ref.shape

    @pl.when(ci == 0)
    def _():
        for h in range(n_heads):
            sl = slice(h * dh, (h + 1) * dh)
            g_q = jnp.dot(wq_ref[h], wg_ref[h * dh:(h + 1) * dh, :], preferred_element_type=F32)
            g_k = jnp.dot(wk_ref[h], wg_ref[d_m + h * dh:d_m + (h + 1) * dh, :],
                          preferred_element_type=F32)
            g_v = jnp.dot(wv_ref[h], wg_ref[2 * d_m + h * dh:2 * d_m + (h + 1) * dh, :],
                          preferred_element_type=F32)
            gw_ref[0, sl, :] = (g_q + g_k * k_scale).astype(BF16)
            gw_ref[1, sl, :] = g_v.astype(BF16)
        c_sc[...] = jnp.zeros_like(c_sc)
        n_sc[...] = jnp.zeros_like(n_sc)
        m_sc[...] = jnp.zeros_like(m_sc)
        tail_ref[...] = jnp.zeros_like(tail_ref)
        row = lax.broadcasted_iota(jnp.int32, (L, L), 0)
        col = lax.broadcasted_iota(jnp.int32, (L, L), 1)
        tri_ref[...] = jnp.where(col <= row, 1.0, 0.0).astype(BF16)
        for j in range(1, CONV_WIDTH):
            shift_ref[j - 1] = jnp.where(col == row - j, 1.0, 0.0).astype(BF16)

    seqs = range(n_seqs)
    fronts = [_mlstm_front(xm_ref.at[i], cw_ref, cb_ref, wq_ref, wk_ref, wv_ref, gw_ref, gb_ref,
                           tail_ref.at[i], tri_ref, shift_ref, k_scale) for i in seqs]
    mixed = [_mlstm_mix(*fronts[i][1:], c_sc.at[i], n_sc.at[i], m_sc.at[i]) for i in seqs]
    for i in seqs:
        _mlstm_out(mixed[i], fronts[i][0], gm_ref.at[i], ng_ref, sk_ref, y_ref.at[i])


def _mlstm_front(xm_ref, cw_ref, cb_ref, wq_ref, wk_ref, wv_ref, gw_ref, gb_ref,
                 tail_ref, tri_ref, shift_ref, k_scale):
    L, d_m = xm_ref.shape
    dh = wq_ref.shape[1]
    n_heads = wq_ref.shape[0]
    halo = SUBLANES

    x_b = xm_ref[...]
    x_f = x_b.astype(F32)
    conv = cb_ref[...] + x_f * cw_ref[CONV_WIDTH - 1:CONV_WIDTH, :]
    edge = jnp.concatenate([tail_ref[...], jnp.zeros((halo, d_m), F32)], axis=0)
    head = jnp.zeros((halo, d_m), F32)
    for j in range(1, CONV_WIDTH):
        w_j = cw_ref[CONV_WIDTH - 1 - j:CONV_WIDTH - j, :]
        conv = conv + jnp.dot(shift_ref[j - 1], x_b, preferred_element_type=F32) * w_j
        head = head + edge[halo - j:2 * halo - j, :] * w_j
    conv = jnp.concatenate([conv[:halo] + head, conv[halo:]], axis=0)
    tail_ref[...] = x_f[L - halo:L, :]
    xc = _silu(conv)
    xc_b = xc.astype(BF16)

    gates = (jnp.dot(xc_b, gw_ref[0], preferred_element_type=F32)
             + jnp.dot(x_b, gw_ref[1], preferred_element_type=F32) + gb_ref[...])
    qs, ks, vs = [], [], []
    for h in range(n_heads):
        sl = slice(h * dh, (h + 1) * dh)
        q = jnp.dot(xc_b[:, sl], wq_ref[h], preferred_element_type=F32)
        k = jnp.dot(xc_b[:, sl], wk_ref[h], preferred_element_type=F32) * k_scale
        v = jnp.dot(x_b[:, sl], wv_ref[h], preferred_element_type=F32)
        qs.append(q.astype(BF16))
        ks.append((k, k.astype(BF16)))
        vs.append(v.astype(BF16))

    ig = gates[:, :LANES]
    bcum = _cumsum_rows(tri_ref[...], _log_sigmoid(gates[:, LANES:]))
    a_col = ig - bcum
    return xc, qs, ks, vs, a_col, bcum


def _mlstm_mix(qs, ks, vs, a_col, bcum, c_sc, n_sc, m_sc):
    n_heads = len(qs)
    L, dh = qs[0].shape
    a_rows = a_col.T
    causal = _tril_mask(L)
    ones_blk = jnp.ones((L, LANES), BF16)

    heads = range(n_heads)
    nt = (((1,), (1,)), ((), ()))
    m_prev = [m_sc[h][:, 0:1] for h in heads]
    dm = [jnp.where(causal, a_rows[h:h + 1, :], NEG_INF) for h in heads]
    m_run = [jnp.maximum(m_prev[h], jnp.max(dm[h], axis=1, keepdims=True)) for h in heads]
    w_inter = [jnp.exp(m_prev[h] - m_run[h]) for h in heads]
    s_b = [(lax.dot_general(qs[h], ks[h][1], nt, preferred_element_type=F32)
            * jnp.exp(dm[h] - m_run[h])).astype(BF16) for h in heads]
    num = [w_inter[h] * jnp.dot(qs[h], c_sc[h].astype(BF16), preferred_element_type=F32)
           + jnp.dot(s_b[h], vs[h], preferred_element_type=F32) for h in heads]
    qn = [lax.dot_general(qs[h], jnp.broadcast_to(n_sc[h], (LANES, dh)).astype(BF16), nt,
                          preferred_element_type=F32) for h in heads]
    den = [w_inter[h] * qn[h] + jnp.dot(s_b[h], ones_blk, preferred_element_type=F32)
           for h in heads]
    m_t = [bcum[:, h:h + 1] + m_run[h] for h in heads]
    inv = [1.0 / jnp.maximum(jnp.abs(den[h]), jnp.exp(-m_t[h])) for h in heads]
    hh = [num[h] * jnp.concatenate([inv[h]] * (dh // LANES), axis=1) for h in heads]

    for h in heads:
        m_last = m_run[h][L - 1:L, :]
        decay = jnp.exp(m_prev[h] - m_last)
        kw = ks[h][0] * jnp.exp(a_col[:, h:h + 1] - m_last)
        c_sc[h] = decay * c_sc[h] + lax.dot_general(
            kw.astype(BF16), vs[h], (((0,), (0,)), ((), ())), preferred_element_type=F32)
        n_sc[h] = decay * n_sc[h] + jnp.sum(kw, axis=0, keepdims=True)
        m_sc[h] = jnp.broadcast_to(m_t[h][L - 1:L, :], m_sc.shape[1:])
    return hh


def _mlstm_out(hh, xc, gm_ref, ng_ref, sk_ref, y_ref):
    heads = range(len(hh))
    dh = hh[0].shape[1]
    mu = [jnp.mean(hh[h], axis=1, keepdims=True) for h in heads]
    cen = [hh[h] - mu[h] for h in heads]
    var = [jnp.mean(cen[h] * cen[h], axis=1, keepdims=True) for h in heads]
    for h in heads:
        sl = slice(h * dh, (h + 1) * dh)
        hn = cen[h] * lax.rsqrt(var[h] + EPS) * ng_ref[:, sl] + sk_ref[:, sl] * xc[:, sl]
        y_ref[:, sl] = (hn * gm_ref[:, sl].astype(F32)).astype(y_ref.dtype)


def _mlstm(xm, gm, conv_w, conv_b, wq, wk, wv, wg, gb, ngain, skip):
    b, s, d = xm.shape
    L = MLSTM_CHUNK
    assert s % L == 0
    n_heads, dh, _ = wq.shape
    n_seqs = MLSTM_SEQS
    assert b % n_seqs == 0
    seq_spec = pl.BlockSpec((n_seqs, L, d), lambda bi, ci: (bi, ci, 0))

    def const(shape):
        nd = len(shape)
        return pl.BlockSpec(shape, lambda bi, ci: (0,) * nd)

    kern = functools.partial(_mlstm_kernel, k_scale=dh ** -0.5)
    return pl.pallas_call(
        kern,
        grid=(b // n_seqs, s // L),
        in_specs=[seq_spec, seq_spec, const(conv_w.shape), const(conv_b.shape), const(wq.shape),
                  const(wk.shape), const(wv.shape), const(wg.shape), const(gb.shape),
                  const(ngain.shape), const(skip.shape)],
        out_specs=seq_spec,
        out_shape=jax.ShapeDtypeStruct((b, s, d), BF16),
        scratch_shapes=[
            pltpu.VMEM((n_seqs, SUBLANES, d), F32),
            pltpu.VMEM((n_seqs, n_heads, dh, dh), F32),
            pltpu.VMEM((n_seqs, n_heads, 1, dh), F32),
            pltpu.VMEM((n_seqs, n_heads, 1, LANES), F32),
            pltpu.VMEM((L, L), BF16),
            pltpu.VMEM((CONV_WIDTH - 1, L, L), BF16),
            pltpu.VMEM((2, d, wg.shape[1]), BF16),
        ],
        compiler_params=pltpu.CompilerParams(
            dimension_semantics=("arbitrary", "arbitrary"), vmem_limit_bytes=VMEM_LIMIT),
        name="mlstm",
    )(xm, gm, conv_w, conv_b, wq, wk, wv, wg, gb, ngain, skip)


def _fox_kernel(q_ref, k_ref, v_ref, cr_ref, gf_ref, g_ref, o_ref):
    s_len, dh = q_ref.shape
    tq = FOX_Q
    causal = _tril_mask(tq)
    gain = g_ref[...]
    c_keys = cr_ref[pl.ds(pl.program_id(1), 1), :]

    for qi in reversed(range(s_len // tq)):
        r0 = qi * tq
        kv = r0 + tq
        s = lax.dot_general(q_ref[r0:kv, :], k_ref[0:kv, :], (((1,), (1,)), ((), ())),
                            preferred_element_type=F32)
        s = s - c_keys[:, 0:kv]
        s_diag = jnp.where(causal, s[:, r0:], NEG_INF)
        s = s_diag if qi == 0 else jnp.concatenate([s[:, :r0], s_diag], axis=1)
        m = jnp.max(s, axis=1, keepdims=True)
        p = jnp.exp2(s - m).astype(BF16)
        v_ext = jnp.concatenate([v_ref[0:kv, :], jnp.ones((kv, dh), BF16)], axis=1)
        acc = jnp.dot(p, v_ext, preferred_element_type=F32)

        out = acc[:, :dh] / acc[:, dh:]
        ms = jnp.mean(out * out, axis=1, keepdims=True)
        o_ref[r0:kv, :] = (out * lax.rsqrt(ms + EPS) * gain
                           * gf_ref[r0:kv, :].astype(F32)).astype(o_ref.dtype)


def _fox(q, k, v, c_row, gf, gain):
    b, s, d = q.shape
    n_heads = N_FOX_HEADS
    dh = d // n_heads
    assert dh == LANES and s % FOX_Q == 0
    head_spec = pl.BlockSpec((None, s, dh), lambda bi, hi: (bi, 0, hi))
    return pl.pallas_call(
        _fox_kernel,
        grid=(b, n_heads),
        in_specs=[head_spec, head_spec, head_spec,
                  pl.BlockSpec((c_row.shape[0], s), lambda bi, hi: (0, bi)),
                  head_spec,
                  pl.BlockSpec((1, dh), lambda bi, hi: (0, hi))],
        out_specs=head_spec,
        out_shape=jax.ShapeDtypeStruct((b, s, d), BF16),
        compiler_params=pltpu.CompilerParams(
            dimension_semantics=("arbitrary", "arbitrary"), vmem_limit_bytes=VMEM_LIMIT),
        name="fox",
    )(q, k, v, c_row, gf, gain)


def _out_proj_kernel(ym_ref, yf_ref, x_ref, w_ref, g_ref, o_ref, wb_ref, *, final_norm):
    d_m = ym_ref.shape[1]

    @pl.when(pl.program_id(0) == 0)
    def _():
        wb_ref[...] = w_ref[...].astype(BF16)

    rows = x_ref.shape[0]
    for r0 in range(0, rows, OUT_PROJ_SUB):
        sl = slice(r0, r0 + OUT_PROJ_SUB)
        y = jnp.dot(ym_ref[sl, :], wb_ref[0:d_m, :], preferred_element_type=F32)
        y = y + jnp.dot(yf_ref[sl, :], wb_ref[d_m:, :], preferred_element_type=F32)
        r = x_ref[sl, :] + y
        if final_norm:
            ms = jnp.mean(r * r, axis=-1, keepdims=True)
            r = r * lax.rsqrt(ms + EPS) * g_ref[...]
        o_ref[sl, :] = r


def _out_proj(ym, yf, x2, w_out, gain, final_norm):
    t_rows, d = x2.shape
    rows = OUT_PROJ_ROWS
    assert t_rows % rows == 0
    row = lambda width: pl.BlockSpec((rows, width), lambda t: (t, 0))
    const = lambda shape: pl.BlockSpec(shape, lambda t: (0, 0), pipeline_mode=pl.Buffered(1))
    kern = functools.partial(_out_proj_kernel, final_norm=final_norm)
    return pl.pallas_call(
        kern,
        grid=(t_rows // rows,),
        in_specs=[row(ym.shape[1]), row(yf.shape[1]), row(d), const(w_out.shape), const((1, d))],
        out_specs=row(d),
        out_shape=jax.ShapeDtypeStruct((t_rows, d), F32),
        scratch_shapes=[pltpu.VMEM(w_out.shape, BF16)],
        compiler_params=pltpu.CompilerParams(
            dimension_semantics=("arbitrary",), vmem_limit_bytes=VMEM_LIMIT),
        name="out_proj",
    )(ym, yf, x2, w_out, gain)


def _block_diag(w, dh):
    nb, bs, _ = w.shape
    d = nb * bs
    tiled = jnp.broadcast_to(w.reshape(d, 1, bs), (d, dh // bs, bs)).reshape(d, dh)
    r = lax.broadcasted_iota(jnp.int32, (d, dh), 0)
    c = lax.broadcasted_iota(jnp.int32, (d, dh), 1)
    keep = (r % dh) // bs == c // bs
    return jnp.where(keep, tiled, 0.0).reshape(d // dh, dh, dh)


def kernel(x, norm_gain, w_in, conv_w, conv_b, w_q_m, w_k_m, w_v_m, w_igate, b_igate, w_fgate,
           b_fgate, mlstm_norm_gain, mlstm_skip, fox_forget_bias, fox_norm_gain, w_out,
           final_norm_gain):
    b, s, d = x.shape
    depth = norm_gain.shape[0]
    d_m = conv_w.shape[-1]
    d_f = fox_norm_gain.shape[-1]
    dh_m = d_m // N_MLSTM_HEADS
    dh_f = d_f // N_FOX_HEADS
    n_main = 2 * d_m + 4 * d_f
    assert d_m == d and d_f == d

    x2 = x.reshape(b * s, d)
    for l in range(depth):
        assert w_in.shape[2] == n_main + N_FOX_HEADS
        f_bias = jnp.pad(fox_forget_bias[l], (0, LANES - N_FOX_HEADS)).reshape(1, LANES)
        xm, gm, qf, kf, vf, gf, c_keys = _in_proj(
            x2, norm_gain[l].reshape(1, d), w_in[l].T, f_bias, s, dh_f ** -0.5 * LOG2E)

        lane_pad = ((0, 0), (0, LANES - N_MLSTM_HEADS))
        wg = jnp.concatenate([jnp.pad(w_igate[l], lane_pad), jnp.pad(w_fgate[l], lane_pad)], axis=1)
        gb = jnp.concatenate([jnp.pad(b_igate[l][None], lane_pad),
                              jnp.pad(b_fgate[l][None], lane_pad)], axis=1)
        y_m = _mlstm(xm.reshape(b, s, d_m), gm.reshape(b, s, d_m), conv_w[l],
                     conv_b[l].reshape(1, d_m),
                     _block_diag(w_q_m[l], dh_m).astype(BF16),
                     _block_diag(w_k_m[l], dh_m).astype(BF16),
                     _block_diag(w_v_m[l], dh_m).astype(BF16),
                     wg.astype(BF16), gb,
                     mlstm_norm_gain[l].reshape(1, d_m), mlstm_skip[l].reshape(1, d_m))

        y_f = _fox(qf.reshape(b, s, d_f), kf.reshape(b, s, d_f), vf.reshape(b, s, d_f), c_keys,
                   gf.reshape(b, s, d_f), fox_norm_gain[l].reshape(1, d_f))

        x2 = _out_proj(y_m.reshape(b * s, d_m), y_f.reshape(b * s, d_f), x2, w_out[l],
                       final_norm_gain.reshape(1, d), final_norm=(l == depth - 1))
    return x2.reshape(b, s, d)
```

```python
import functools
import math

import jax
import jax.numpy as jnp
from jax import lax
from jax.experimental import pallas as pl
from jax.experimental.pallas import tpu as pltpu

EPS = 1e-6
N_MLSTM_HEADS = 4
N_FOX_HEADS = 8
QKV_BLOCK = 4
CONV_WIDTH = 4

LANES = 128
SUBLANES = 8
VMEM_LIMIT = 56 * 1024 * 1024

IN_PROJ_ROWS = 256
OUT_PROJ_ROWS = 1024
OUT_PROJ_SUB = 256
MLSTM_CHUNK = 256
MLSTM_SEQS = 2
FOX_Q = 512
LOG2E = math.log2(math.e)

BF16 = jnp.bfloat16
F32 = jnp.float32
NEG_INF = float("-inf")


def _log_sigmoid(z):
    return jnp.minimum(z, 0.0) - jnp.log(1.0 + jnp.exp(-jnp.abs(z)))


def _sigmoid(z):
    return 1.0 / (1.0 + jnp.exp(-z))


def _silu(z):
    return z * _sigmoid(z)


def _split3(v):
    hi = v.astype(BF16)
    r = v - hi.astype(F32)
    mid = r.astype(BF16)
    lo = (r - mid.astype(F32)).astype(BF16)
    return hi, mid, lo


def _cumsum_rows(tri, v):
    w = v.shape[1]
    cat = jnp.concatenate(_split3(v), axis=1)
    cs = jnp.dot(tri, cat, preferred_element_type=F32)
    return cs[:, :w] + cs[:, w:2 * w] + cs[:, 2 * w:]


def _tril_mask(n):
    row = lax.broadcasted_iota(jnp.int32, (n, n), 0)
    col = lax.broadcasted_iota(jnp.int32, (n, n), 1)
    return col <= row


def _in_proj_kernel(x_ref, g_ref, w_ref, fb_ref,
                    xm_ref, gm_ref, q_ref, k_ref, v_ref, gf_ref, c_ref,
                    wb_ref, wfb_ref, tri_ref, carry_ref, *, tiles_per_seq, q_scale):
    t = pl.program_id(0)
    rows = x_ref.shape[0]
    d = x_ref.shape[1]
    n_main = wb_ref.shape[1]
    n_gate = w_ref.shape[0] - n_main

    @pl.when(t == 0)
    def _():
        tri_ref[...] = jnp.where(_tril_mask(rows), 1.0, 0.0).astype(BF16)
        for c0 in range(0, n_main, d):
            wb_ref[:, c0:c0 + d] = w_ref[c0:c0 + d, :].T.astype(BF16)
        w_gate = jnp.concatenate(
            [w_ref[n_main:n_main + n_gate, :], jnp.zeros((LANES - n_gate, d), F32)], axis=0)
        wfb_ref[...] = w_gate.T.astype(BF16)

    @pl.when(t % tiles_per_seq == 0)
    def _():
        carry_ref[...] = jnp.zeros_like(carry_ref)

    x = x_ref[...]
    ms = jnp.mean(x * x, axis=-1, keepdims=True)
    hn = (x * lax.rsqrt(ms + EPS) * g_ref[...]).astype(BF16)

    def proj(idx):
        return jnp.dot(hn, wb_ref[:, idx * d:(idx + 1) * d], preferred_element_type=F32)

    half = rows // 2
    f = jnp.concatenate(
        [jnp.dot(hn[r0:r0 + half, :], wfb_ref[...], preferred_element_type=F32)
         for r0 in (0, half)], axis=0) + fb_ref[...]
    c = _cumsum_rows(tri_ref[...], _log_sigmoid(f)) + carry_ref[...]
    c_ref[...] = (c * LOG2E).T[0:c_ref.shape[0], :]
    carry_ref[...] = c[rows - 1:rows, :]

    xm_ref[...] = proj(0).astype(xm_ref.dtype)
    gm_ref[...] = _silu(proj(1)).astype(gm_ref.dtype)
    q_ref[...] = (proj(2) * q_scale).astype(q_ref.dtype)
    k_ref[...] = proj(3).astype(k_ref.dtype)
    v_ref[...] = proj(4).astype(v_ref.dtype)
    gf_ref[...] = _silu(proj(5)).astype(gf_ref.dtype)


def _in_proj(x2, gain, w_in_t, f_bias, seq_len, q_scale):
    t_rows, d = x2.shape
    rows = IN_PROJ_ROWS
    assert t_rows % rows == 0 and seq_len % rows == 0
    n_main = 6 * d
    n_gate = w_in_t.shape[0] - n_main
    assert 0 < n_gate <= SUBLANES and w_in_t.shape[1] == d
    row_spec = pl.BlockSpec((rows, d), lambda t: (t, 0))
    const = lambda shape: pl.BlockSpec(shape, lambda t: (0, 0), pipeline_mode=pl.Buffered(1))
    kern = functools.partial(_in_proj_kernel, tiles_per_seq=seq_len // rows, q_scale=q_scale)
    return pl.pallas_call(
        kern,
        grid=(t_rows // rows,),
        in_specs=[row_spec, const((1, d)), const(w_in_t.shape), const((1, LANES))],
        out_specs=[row_spec] * 6 + [pl.BlockSpec((SUBLANES, rows), lambda t: (0, t))],
        out_shape=[jax.ShapeDtypeStruct((t_rows, d), BF16)] * 6
        + [jax.ShapeDtypeStruct((SUBLANES, t_rows), F32)],
        scratch_shapes=[pltpu.VMEM((d, n_main), BF16), pltpu.VMEM((d, LANES), BF16),
                        pltpu.VMEM((rows, rows), BF16), pltpu.VMEM((1, LANES), F32)],
        compiler_params=pltpu.CompilerParams(
            dimension_semantics=("arbitrary",), vmem_limit_bytes=VMEM_LIMIT),
        name="in_proj",
    )(x2, gain, w_in_t, f_bias)


def _mlstm_kernel(xm_ref, gm_ref, cw_ref, cb_ref, wq_ref, wk_ref, wv_ref, wg_ref, gb_ref,
                  ng_ref, sk_ref, y_ref,
                  tail_ref, c_sc, n_sc, m_sc, tri_ref, shift_ref, gw_ref, *, k_scale):
    ci = pl.program_id(1)
    n_seqs, L, d_m = xm_ref.shape
    n_heads, dh, _ = wq_ref.shape

    @pl.when(ci == 0)
    def _():
        for h in range(n_heads):
            sl = slice(h * dh, (h + 1) * dh)
            g_q = jnp.dot(wq_ref[h], wg_ref[h * dh:(h + 1) * dh, :], preferred_element_type=F32)
            g_k = jnp.dot(wk_ref[h], wg_ref[d_m + h * dh:d_m + (h + 1) * dh, :],
                          preferred_element_type=F32)
            g_v = jnp.dot(wv_ref[h], wg_ref[2 * d_m + h * dh:2 * d_m + (h + 1) * dh, :],
                          preferred_element_type=F32)
            gw_ref[0, sl, :] = (g_q + g_k * k_scale).astype(BF16)
            gw_ref[1, sl, :] = g_v.astype(BF16)
        c_sc[...] = jnp.zeros_like(c_sc)
        n_sc[...] = jnp.zeros_like(n_sc)
        m_sc[...] = jnp.zeros_like(m_sc)
        tail_ref[...] = jnp.zeros_like(tail_ref)
        row = lax.broadcasted_iota(jnp.int32, (L, L), 0)
        col = lax.broadcasted_iota(jnp.int32, (L, L), 1)
        tri_ref[...] = jnp.where(col <= row, 1.0, 0.0).astype(BF16)
        for j in range(1, CONV_WIDTH):
            shift_ref[j - 1] = jnp.where(col == row - j, 1.0, 0.0).astype(BF16)

    seqs = range(n_seqs)
    fronts = [_mlstm_front(xm_ref.at[i], cw_ref, cb_ref, wq_ref, wk_ref, wv_ref, gw_ref, gb_ref,
                           tail_ref.at[i], tri_ref, shift_ref, k_scale) for i in seqs]
    mixed = [_mlstm_mix(*fronts[i][1:], c_sc.at[i], n_sc.at[i], m_sc.at[i]) for i in seqs]
    for i in seqs:
        _mlstm_out(mixed[i], fronts[i][0], gm_ref.at[i], ng_ref, sk_ref, y_ref.at[i])


def _mlstm_front(xm_ref, cw_ref, cb_ref, wq_ref, wk_ref, wv_ref, gw_ref, gb_ref,
                 tail_ref, tri_ref, shift_ref, k_scale):
    L, d_m = xm_ref.shape
    dh = wq_ref.shape[1]
    n_heads = wq_ref.shape[0]
    halo = SUBLANES

    x_b = xm_ref[...]
    x_f = x_b.astype(F32)
    conv = cb_ref[...] + x_f * cw_ref[CONV_WIDTH - 1:CONV_WIDTH, :]
    edge = jnp.concatenate([tail_ref[...], jnp.zeros((halo, d_m), F32)], axis=0)
    head = jnp.zeros((halo, d_m), F32)
    for j in range(1, CONV_WIDTH):
        w_j = cw_ref[CONV_WIDTH - 1 - j:CONV_WIDTH - j, :]
        conv = conv + jnp.dot(shift_ref[j - 1], x_b, preferred_element_type=F32) * w_j
        head = head + edge[halo - j:2 * halo - j, :] * w_j
    conv = jnp.concatenate([conv[:halo] + head, conv[halo:]], axis=0)
    tail_ref[...] = x_f[L - halo:L, :]
    xc = _silu(conv)
    xc_b = xc.astype(BF16)

    gates = (jnp.dot(xc_b, gw_ref[0], preferred_element_type=F32)
             + jnp.dot(x_b, gw_ref[1], preferred_element_type=F32) + gb_ref[...])
    qs, ks, vs = [], [], []
    for h in range(n_heads):
        sl = slice(h * dh, (h + 1) * dh)
        q = jnp.dot(xc_b[:, sl], wq_ref[h], preferred_element_type=F32)
        k = jnp.dot(xc_b[:, sl], wk_ref[h], preferred_element_type=F32) * k_scale
        v = jnp.dot(x_b[:, sl], wv_ref[h], preferred_element_type=F32)
        qs.append(q.astype(BF16))
        ks.append((k, k.astype(BF16)))
        vs.append(v.astype(BF16))

    ig = gates[:, :LANES]
    bcum = _cumsum_rows(tri_ref[...], _log_sigmoid(gates[:, LANES:]))
    a_col = ig - bcum
    return xc, qs, ks, vs, a_col, bcum


def _mlstm_mix(qs, ks, vs, a_col, bcum, c_sc, n_sc, m_sc):
    n_heads = len(qs)
    L, dh = qs[0].shape
    a_rows = a_col.T
    causal = _tril_mask(L)
    ones_blk = jnp.ones((L, LANES), BF16)

    heads = range(n_heads)
    nt = (((1,), (1,)), ((), ()))
    m_prev = [m_sc[h][:, 0:1] for h in heads]
    dm = [jnp.where(causal, a_rows[h:h + 1, :], NEG_INF) for h in heads]
    m_run = [jnp.maximum(m_prev[h], jnp.max(dm[h], axis=1, keepdims=True)) for h in heads]
    w_inter = [jnp.exp(m_prev[h] - m_run[h]) for h in heads]
    s_b = [(lax.dot_general(qs[h], ks[h][1], nt, preferred_element_type=F32)
            * jnp.exp(dm[h] - m_run[h])).astype(BF16) for h in heads]
    num = [w_inter[h] * jnp.dot(qs[h], c_sc[h].astype(BF16), preferred_element_type=F32)
           + jnp.dot(s_b[h], vs[h], preferred_element_type=F32) for h in heads]
    qn = [lax.dot_general(qs[h], jnp.broadcast_to(n_sc[h], (LANES, dh)).astype(BF16), nt,
                          preferred_element_type=F32) for h in heads]
    den = [w_inter[h] * qn[h] + jnp.dot(s_b[h], ones_blk, preferred_element_type=F32)
           for h in heads]
    m_t = [bcum[:, h:h + 1] + m_run[h] for h in heads]
    inv = [1.0 / jnp.maximum(jnp.abs(den[h]), jnp.exp(-m_t[h])) for h in heads]
    hh = [num[h] * jnp.concatenate([inv[h]] * (dh // LANES), axis=1) for h in heads]

    for h in heads:
        m_last = m_run[h][L - 1:L, :]
        decay = jnp.exp(m_prev[h] - m_last)
        kw = ks[h][0] * jnp.exp(a_col[:, h:h + 1] - m_last)
        c_sc[h] = decay * c_sc[h] + lax.dot_general(
            kw.astype(BF16), vs[h], (((0,), (0,)), ((), ())), preferred_element_type=F32)
        n_sc[h] = decay * n_sc[h] + jnp.sum(kw, axis=0, keepdims=True)
        m_sc[h] = jnp.broadcast_to(m_t[h][L - 1:L, :], m_sc.shape[1:])
    return hh


def _mlstm_out(hh, xc, gm_ref, ng_ref, sk_ref, y_ref):
    heads = range(len(hh))
    dh = hh[0].shape[1]
    mu = [jnp.mean(hh[h], axis=1, keepdims=True) for h in heads]
    cen = [hh[h] - mu[h] for h in heads]
    var = [jnp.mean(cen[h] * cen[h], axis=1, keepdims=True) for h in heads]
    for h in heads:
        sl = slice(h * dh, (h + 1) * dh)
        hn = cen[h] * lax.rsqrt(var[h] + EPS) * ng_ref[:, sl] + sk_ref[:, sl] * xc[:, sl]
        y_ref[:, sl] = (hn * gm_ref[:, sl].astype(F32)).astype(y_ref.dtype)


def _mlstm(xm, gm, conv_w, conv_b, wq, wk, wv, wg, gb, ngain, skip):
    b, s, d = xm.shape
    L = MLSTM_CHUNK
    assert s % L == 0
    n_heads, dh, _ = wq.shape
    n_seqs = MLSTM_SEQS
    assert b % n_seqs == 0
    seq_spec = pl.BlockSpec((n_seqs, L, d), lambda bi, ci: (bi, ci, 0))

    def const(shape):
        nd = len(shape)
        return pl.BlockSpec(shape, lambda bi, ci: (0,) * nd)

    kern = functools.partial(_mlstm_kernel, k_scale=dh ** -0.5)
    return pl.pallas_call(
        kern,
        grid=(b // n_seqs, s // L),
        in_specs=[seq_spec, seq_spec, const(conv_w.shape), const(conv_b.shape), const(wq.shape),
                  const(wk.shape), const(wv.shape), const(wg.shape), const(gb.shape),
                  const(ngain.shape), const(skip.shape)],
        out_specs=seq_spec,
        out_shape=jax.ShapeDtypeStruct((b, s, d), BF16),
        scratch_shapes=[
            pltpu.VMEM((n_seqs, SUBLANES, d), F32),
            pltpu.VMEM((n_seqs, n_heads, dh, dh), F32),
            pltpu.VMEM((n_seqs, n_heads, 1, dh), F32),
            pltpu.VMEM((n_seqs, n_heads, 1, LANES), F32),
            pltpu.VMEM((L, L), BF16),
            pltpu.VMEM((CONV_WIDTH - 1, L, L), BF16),
            pltpu.VMEM((2, d, wg.shape[1]), BF16),
        ],
        compiler_params=pltpu.CompilerParams(
            dimension_semantics=("arbitrary", "arbitrary"), vmem_limit_bytes=VMEM_LIMIT),
        name="mlstm",
    )(xm, gm, conv_w, conv_b, wq, wk, wv, wg, gb, ngain, skip)


def _fox_kernel(q_ref, k_ref, v_ref, cr_ref, gf_ref, g_ref, o_ref):
    s_len, dh = q_ref.shape
    tq = FOX_Q
    causal = _tril_mask(tq)
    gain = g_ref[...]
    c_keys = cr_ref[pl.ds(pl.program_id(1), 1), :]

    for qi in reversed(range(s_len // tq)):
        r0 = qi * tq
        kv = r0 + tq
        s = lax.dot_general(q_ref[r0:kv, :], k_ref[0:kv, :], (((1,), (1,)), ((), ())),
                            preferred_element_type=F32)
        s = s - c_keys[:, 0:kv]
        s_diag = jnp.where(causal, s[:, r0:], NEG_INF)
        s = s_diag if qi == 0 else jnp.concatenate([s[:, :r0], s_diag], axis=1)
        m = jnp.max(s, axis=1, keepdims=True)
        p = jnp.exp2(s - m).astype(BF16)
        v_ext = jnp.concatenate([v_ref[0:kv, :], jnp.ones((kv, dh), BF16)], axis=1)
        acc = jnp.dot(p, v_ext, preferred_element_type=F32)

        out = acc[:, :dh] / acc[:, dh:]
        ms = jnp.mean(out * out, axis=1, keepdims=True)
        o_ref[r0:kv, :] = (out * lax.rsqrt(ms + EPS) * gain
                           * gf_ref[r0:kv, :].astype(F32)).astype(o_ref.dtype)


def _fox(q, k, v, c_row, gf, gain):
    b, s, d = q.shape
    n_heads = N_FOX_HEADS
    dh = d // n_heads
    assert dh == LANES and s % FOX_Q == 0
    head_spec = pl.BlockSpec((None, s, dh), lambda bi, hi: (bi, 0, hi))
    return pl.pallas_call(
        _fox_kernel,
        grid=(b, n_heads),
        in_specs=[head_spec, head_spec, head_spec,
                  pl.BlockSpec((c_row.shape[0], s), lambda bi, hi: (0, bi)),
                  head_spec,
                  pl.BlockSpec((1, dh), lambda bi, hi: (0, hi))],
        out_specs=head_spec,
        out_shape=jax.ShapeDtypeStruct((b, s, d), BF16),
        compiler_params=pltpu.CompilerParams(
            dimension_semantics=("arbitrary", "arbitrary"), vmem_limit_bytes=VMEM_LIMIT),
        name="fox",
    )(q, k, v, c_row, gf, gain)


def _out_proj_kernel(ym_ref, yf_ref, x_ref, w_ref, g_ref, o_ref, wb_ref, *, final_norm):
    d_m = ym_ref.shape[1]

    @pl.when(pl.program_id(0) == 0)
    def _():
        wb_ref[...] = w_ref[...].astype(BF16)

    rows = x_ref.shape[0]
    for r0 in range(0, rows, OUT_PROJ_SUB):
        sl = slice(r0, r0 + OUT_PROJ_SUB)
        y = jnp.dot(ym_ref[sl, :], wb_ref[0:d_m, :], preferred_element_type=F32)
        y = y + jnp.dot(yf_ref[sl, :], wb_ref[d_m:, :], preferred_element_type=F32)
        r = x_ref[sl, :] + y
        if final_norm:
            ms = jnp.mean(r * r, axis=-1, keepdims=True)
            r = r * lax.rsqrt(ms + EPS) * g_ref[...]
        o_ref[sl, :] = r


def _out_proj(ym, yf, x2, w_out, gain, final_norm):
    t_rows, d = x2.shape
    rows = OUT_PROJ_ROWS
    assert t_rows % rows == 0
    row = lambda width: pl.BlockSpec((rows, width), lambda t: (t, 0))
    const = lambda shape: pl.BlockSpec(shape, lambda t: (0, 0), pipeline_mode=pl.Buffered(1))
    kern = functools.partial(_out_proj_kernel, final_norm=final_norm)
    return pl.pallas_call(
        kern,
        grid=(t_rows // rows,),
        in_specs=[row(ym.shape[1]), row(yf.shape[1]), row(d), const(w_out.shape), const((1, d))],
        out_specs=row(d),
        out_shape=jax.ShapeDtypeStruct((t_rows, d), F32),
        scratch_shapes=[pltpu.VMEM(w_out.shape, BF16)],
        compiler_params=pltpu.CompilerParams(
            dimension_semantics=("arbitrary",), vmem_limit_bytes=VMEM_LIMIT),
        name="out_proj",
    )(ym, yf, x2, w_out, gain)


def _block_diag(w, dh):
    nb, bs, _ = w.shape
    d = nb * bs
    tiled = jnp.broadcast_to(w.reshape(d, 1, bs), (d, dh // bs, bs)).reshape(d, dh)
    r = lax.broadcasted_iota(jnp.int32, (d, dh), 0)
    c = lax.broadcasted_iota(jnp.int32, (d, dh), 1)
    keep = (r % dh) // bs == c // bs
    return jnp.where(keep, tiled, 0.0).reshape(d // dh, dh, dh)


def kernel(x, norm_gain, w_in, conv_w, conv_b, w_q_m, w_k_m, w_v_m, w_igate, b_igate, w_fgate,
           b_fgate, mlstm_norm_gain, mlstm_skip, fox_forget_bias, fox_norm_gain, w_out,
           final_norm_gain):
    b, s, d = x.shape
    depth = norm_gain.shape[0]
    d_m = conv_w.shape[-1]
    d_f = fox_norm_gain.shape[-1]
    dh_m = d_m // N_MLSTM_HEADS
    dh_f = d_f // N_FOX_HEADS
    n_main = 2 * d_m + 4 * d_f
    assert d_m == d and d_f == d

    x2 = x.reshape(b * s, d)
    for l in range(depth):
        assert w_in.shape[2] == n_main + N_FOX_HEADS
        f_bias = jnp.pad(fox_forget_bias[l], (0, LANES - N_FOX_HEADS)).reshape(1, LANES)
        xm, gm, qf, kf, vf, gf, c_keys = _in_proj(
            x2, norm_gain[l].reshape(1, d), w_in[l].T, f_bias, s, dh_f ** -0.5 * LOG2E)

        lane_pad = ((0, 0), (0, LANES - N_MLSTM_HEADS))
        wg = jnp.concatenate([jnp.pad(w_igate[l], lane_pad), jnp.pad(w_fgate[l], lane_pad)], axis=1)
        gb = jnp.concatenate([jnp.pad(b_igate[l][None], lane_pad),
                              jnp.pad(b_fgate[l][None], lane_pad)], axis=1)
        y_m = _mlstm(xm.reshape(b, s, d_m), gm.reshape(b, s, d_m), conv_w[l],
                     conv_b[l].reshape(1, d_m),
                     _block_diag(w_q_m[l], dh_m).astype(BF16),
                     _block_diag(w_k_m[l], dh_m).astype(BF16),
                     _block_diag(w_v_m[l], dh_m).astype(BF16),
                     wg.astype(BF16), gb,
                     mlstm_norm_gain[l].reshape(1, d_m), mlstm_skip[l].reshape(1, d_m))

        y_f = _fox(qf.reshape(b, s, d_f), kf.reshape(b, s, d_f), vf.reshape(b, s, d_f), c_keys,
                   gf.reshape(b, s, d_f), fox_norm_gain[l].reshape(1, d_f))

        x2 = _out_proj(y_m.reshape(b * s, d_m), y_f.reshape(b * s, d_f), x2, w_out[l],
                       final_norm_gain.reshape(1, d), final_norm=(l == depth - 1))
    return x2.reshape(b, s, d)
```

```python
import functools
import math

import jax
import jax.numpy as jnp
from jax import lax
from jax.experimental import pallas as pl
from jax.experimental.pallas import tpu as pltpu

EPS = 1e-6
N_MLSTM_HEADS = 4
N_FOX_HEADS = 8
QKV_BLOCK = 4
CONV_WIDTH = 4

LANES = 128
SUBLANES = 8
VMEM_LIMIT = 56 * 1024 * 1024

IN_PROJ_ROWS = 512
IN_PROJ_SUB = 256
OUT_PROJ_ROWS = 1024
OUT_PROJ_SUB = 256
MLSTM_CHUNK = 256
MLSTM_SEQS = 2
FOX_Q = 512
LOG2E = math.log2(math.e)

BF16 = jnp.bfloat16
F32 = jnp.float32
NEG_INF = float("-inf")


def _log_sigmoid(z):
    return jnp.minimum(z, 0.0) - jnp.log(1.0 + jnp.exp(-jnp.abs(z)))


def _sigmoid(z):
    return 1.0 / (1.0 + jnp.exp(-z))


def _silu(z):
    return z * _sigmoid(z)


def _split3(v):
    hi = v.astype(BF16)
    r = v - hi.astype(F32)
    mid = r.astype(BF16)
    lo = (r - mid.astype(F32)).astype(BF16)
    return hi, mid, lo


def _cumsum_rows(tri, v):
    w = v.shape[1]
    cat = jnp.concatenate(_split3(v), axis=1)
    cs = jnp.dot(tri, cat, preferred_element_type=F32)
    return cs[:, :w] + cs[:, w:2 * w] + cs[:, 2 * w:]


def _tril_mask(n):
    row = lax.broadcasted_iota(jnp.int32, (n, n), 0)
    col = lax.broadcasted_iota(jnp.int32, (n, n), 1)
    return col <= row


def _in_proj_kernel(x_ref, g_ref, w_hbm, fb_ref,
                    xm_ref, gm_ref, q_ref, k_ref, v_ref, gf_ref, c_ref,
                    wb_ref, wfb_ref, tri_ref, carry_ref, stage_ref, gstage_ref, sems,
                    *, tiles_per_seq, q_scale):
    t = pl.program_id(0)
    rows, d = x_ref.shape
    sub = tri_ref.shape[0]
    n_main = wb_ref.shape[1]
    n_gate = gstage_ref.shape[0]
    n_chunks = n_main // d

    def chunk_copy(c):
        return pltpu.make_async_copy(w_hbm.at[pl.ds(c * d, d), :], stage_ref.at[c % 2], sems.at[c % 2])

    def gate_copy():
        return pltpu.make_async_copy(w_hbm.at[pl.ds(n_main, n_gate), :], gstage_ref, sems.at[2])

    @pl.when(t == 0)
    def _():
        gate_copy().start()
        chunk_copy(0).start()
        tri_ref[...] = jnp.where(_tril_mask(sub), 1.0, 0.0).astype(BF16)
        for c in range(n_chunks):
            if c + 1 < n_chunks:
                chunk_copy(c + 1).start()
            chunk_copy(c).wait()
            wb_ref[:, c * d:(c + 1) * d] = stage_ref[c % 2].T.astype(BF16)
        gate_copy().wait()
        w_gate = jnp.concatenate([gstage_ref[...], jnp.zeros((LANES - n_gate, d), F32)], axis=0)
        wfb_ref[...] = w_gate.T.astype(BF16)

    @pl.when(t % tiles_per_seq == 0)
    def _():
        carry_ref[...] = jnp.zeros_like(carry_ref)

    blocks = [slice(r0, r0 + sub) for r0 in range(0, rows, sub)]
    hns = []
    for sl in blocks:
        x = x_ref[sl, :]
        ms = jnp.mean(x * x, axis=-1, keepdims=True)
        hns.append((x * lax.rsqrt(ms + EPS) * g_ref[...]).astype(BF16))

    carry = carry_ref[...]
    for sl, hn in zip(blocks, hns):
        def proj(idx):
            return jnp.dot(hn, wb_ref[:, idx * d:(idx + 1) * d], preferred_element_type=F32)

        xm_ref[sl, :] = proj(0).astype(xm_ref.dtype)
        gm_ref[sl, :] = _silu(proj(1)).astype(gm_ref.dtype)
        q_ref[sl, :] = (proj(2) * q_scale).astype(q_ref.dtype)
        k_ref[sl, :] = proj(3).astype(k_ref.dtype)
        v_ref[sl, :] = proj(4).astype(v_ref.dtype)
        gf_ref[sl, :] = _silu(proj(5)).astype(gf_ref.dtype)

        f = jnp.dot(hn, wfb_ref[...], preferred_element_type=F32) + fb_ref[...]
        c = _cumsum_rows(tri_ref[...], _log_sigmoid(f)) + carry
        c_ref[:, sl] = (c * LOG2E).T[0:c_ref.shape[0], :]
        carry = c[sub - 1:sub, :]
    carry_ref[...] = carry


def _in_proj(x2, gain, w_in_t, f_bias, seq_len, q_scale):
    t_rows, d = x2.shape
    rows, sub = IN_PROJ_ROWS, IN_PROJ_SUB
    assert t_rows % rows == 0 and seq_len % rows == 0 and rows % sub == 0
    n_main = 6 * d
    n_gate = w_in_t.shape[0] - n_main
    assert 0 < n_gate <= SUBLANES and w_in_t.shape[1] == d
    row_spec = pl.BlockSpec((rows, d), lambda t: (t, 0))
    const = lambda shape: pl.BlockSpec(shape, lambda t: (0, 0), pipeline_mode=pl.Buffered(1))
    kern = functools.partial(_in_proj_kernel, tiles_per_seq=seq_len // rows, q_scale=q_scale)
    return pl.pallas_call(
        kern,
        grid=(t_rows // rows,),
        in_specs=[row_spec, const((1, d)), pl.BlockSpec(memory_space=pl.ANY), const((1, LANES))],
        out_specs=[row_spec] * 6 + [pl.BlockSpec((SUBLANES, rows), lambda t: (0, t))],
        out_shape=[jax.ShapeDtypeStruct((t_rows, d), BF16)] * 6
        + [jax.ShapeDtypeStruct((SUBLANES, t_rows), F32)],
        scratch_shapes=[pltpu.VMEM((d, n_main), BF16), pltpu.VMEM((d, LANES), BF16),
                        pltpu.VMEM((sub, sub), BF16), pltpu.VMEM((1, LANES), F32),
                        pltpu.VMEM((2, d, d), F32), pltpu.VMEM((n_gate, d), F32),
                        pltpu.SemaphoreType.DMA((3,))],
        compiler_params=pltpu.CompilerParams(
            dimension_semantics=("arbitrary",), vmem_limit_bytes=VMEM_LIMIT),
        name="in_proj",
    )(x2, gain, w_in_t, f_bias)


def _mlstm_kernel(xm_ref, gm_ref, cw_ref, cb_ref, wq_ref, wk_ref, wv_ref, wg_ref, gb_ref,
                  ng_ref, sk_ref, y_ref,
                  tail_ref, c_sc, n_sc, m_sc, tri_ref, shift_ref, gw_ref, *, k_scale):
    ci = pl.program_id(1)
    n_seqs, L, d_m = xm_ref.shape
    n_heads, dh, _ = wq_ref.shape

    @pl.when(ci == 0)
    def _():
        for h in range(n_heads):
            sl = slice(h * dh, (h + 1) * dh)
            g_q = jnp.dot(wq_ref[h], wg_ref[h * dh:(h + 1) * dh, :], preferred_element_type=F32)
            g_k = jnp.dot(wk_ref[h], wg_ref[d_m + h * dh:d_m + (h + 1) * dh, :],
                          preferred_element_type=F32)
            g_v = jnp.dot(wv_ref[h], wg_ref[2 * d_m + h * dh:2 * d_m + (h + 1) * dh, :],
                          preferred_element_type=F32)
            gw_ref[0, sl, :] = (g_q + g_k * k_scale).astype(BF16)
            gw_ref[1, sl, :] = g_v.astype(BF16)
        c_sc[...] = jnp.zeros_like(c_sc)
        n_sc[...] = jnp.zeros_like(n_sc)
        m_sc[...] = jnp.zeros_like(m_sc)
        tail_ref[...] = jnp.zeros_like(tail_ref)
        row = lax.broadcasted_iota(jnp.int32, (L, L), 0)
        col = lax.broadcasted_iota(jnp.int32, (L, L), 1)
        tri_ref[...] = jnp.where(col <= row, 1.0, 0.0).astype(BF16)
        for j in range(1, CONV_WIDTH):
            shift_ref[j - 1] = jnp.where(col == row - j, 1.0, 0.0).astype(BF16)

    seqs = range(n_seqs)
    fronts = [_mlstm_front(xm_ref.at[i], cw_ref, cb_ref, wq_ref, wk_ref, wv_ref, gw_ref, gb_ref,
                           tail_ref.at[i], tri_ref, shift_ref, k_scale) for i in seqs]
    mixed = [_mlstm_mix(*fronts[i][1:], c_sc.at[i], n_sc.at[i], m_sc.at[i]) for i in seqs]
    for i in seqs:
        _mlstm_out(mixed[i], fronts[i][0], gm_ref.at[i], ng_ref, sk_ref, y_ref.at[i])


def _mlstm_front(xm_ref, cw_ref, cb_ref, wq_ref, wk_ref, wv_ref, gw_ref, gb_ref,
                 tail_ref, tri_ref, shift_ref, k_scale):
    L, d_m = xm_ref.shape
    dh = wq_ref.shape[1]
    n_heads = wq_ref.shape[0]
    halo = SUBLANES

    x_b = xm_ref[...]
    x_f = x_b.astype(F32)
    conv = cb_ref[...] + x_f * cw_ref[CONV_WIDTH - 1:CONV_WIDTH, :]
    edge = jnp.concatenate([tail_ref[...], jnp.zeros((halo, d_m), F32)], axis=0)
    head = jnp.zeros((halo, d_m), F32)
    for j in range(1, CONV_WIDTH):
        w_j = cw_ref[CONV_WIDTH - 1 - j:CONV_WIDTH - j, :]
        conv = conv + jnp.dot(shift_ref[j - 1], x_b, preferred_element_type=F32) * w_j
        head = head + edge[halo - j:2 * halo - j, :] * w_j
    conv = jnp.concatenate([conv[:halo] + head, conv[halo:]], axis=0)
    tail_ref[...] = x_f[L - halo:L, :]
    xc = _silu(conv)
    xc_b = xc.astype(BF16)

    gates = (jnp.dot(xc_b, gw_ref[0], preferred_element_type=F32)
             + jnp.dot(x_b, gw_ref[1], preferred_element_type=F32) + gb_ref[...])
    qs, ks, vs = [], [], []
    for h in range(n_heads):
        sl = slice(h * dh, (h + 1) * dh)
        q = jnp.dot(xc_b[:, sl], wq_ref[h], preferred_element_type=F32)
        k = jnp.dot(xc_b[:, sl], wk_ref[h], preferred_element_type=F32) * k_scale
        v = jnp.dot(x_b[:, sl], wv_ref[h], preferred_element_type=F32)
        qs.append(q.astype(BF16))
        ks.append((k, k.astype(BF16)))
        vs.append(v.astype(BF16))

    ig = gates[:, :LANES]
    bcum = _cumsum_rows(tri_ref[...], _log_sigmoid(gates[:, LANES:]))
    a_col = ig - bcum
    return xc, qs, ks, vs, a_col, bcum


def _mlstm_mix(qs, ks, vs, a_col, bcum, c_sc, n_sc, m_sc):
    n_heads = len(qs)
    L, dh = qs[0].shape
    a_rows = a_col.T
    causal = _tril_mask(L)
    ones_blk = jnp.ones((L, LANES), BF16)

    heads = range(n_heads)
    nt = (((1,), (1,)), ((), ()))
    m_prev = [m_sc[h][:, 0:1] for h in heads]
    dm = [jnp.where(causal, a_rows[h:h + 1, :], NEG_INF) for h in heads]
    m_run = [jnp.maximum(m_prev[h], jnp.max(dm[h], axis=1, keepdims=True)) for h in heads]
    w_inter = [jnp.exp(m_prev[h] - m_run[h]) for h in heads]
    s_b = [(lax.dot_general(qs[h], ks[h][1], nt, preferred_element_type=F32)
            * jnp.exp(dm[h] - m_run[h])).astype(BF16) for h in heads]
    num = [w_inter[h] * jnp.dot(qs[h], c_sc[h].astype(BF16), preferred_element_type=F32)
           + jnp.dot(s_b[h], vs[h], preferred_element_type=F32) for h in heads]
    qn = [lax.dot_general(qs[h], jnp.broadcast_to(n_sc[h], (LANES, dh)).astype(BF16), nt,
                          preferred_element_type=F32) for h in heads]
    den = [w_inter[h] * qn[h] + jnp.dot(s_b[h], ones_blk, preferred_element_type=F32)
           for h in heads]
    m_t = [bcum[:, h:h + 1] + m_run[h] for h in heads]
    inv = [1.0 / jnp.maximum(jnp.abs(den[h]), jnp.exp(-m_t[h])) for h in heads]
    hh = [num[h] * jnp.concatenate([inv[h]] * (dh // LANES), axis=1) for h in heads]

    for h in heads:
        m_last = m_run[h][L - 1:L, :]
        decay = jnp.exp(m_prev[h] - m_last)
        kw = ks[h][0] * jnp.exp(a_col[:, h:h + 1] - m_last)
        c_sc[h] = decay * c_sc[h] + lax.dot_general(
            kw.astype(BF16), vs[h], (((0,), (0,)), ((), ())), preferred_element_type=F32)
        n_sc[h] = decay * n_sc[h] + jnp.sum(kw, axis=0, keepdims=True)
        m_sc[h] = jnp.broadcast_to(m_t[h][L - 1:L, :], m_sc.shape[1:])
    return hh


def _mlstm_out(hh, xc, gm_ref, ng_ref, sk_ref, y_ref):
    heads = range(len(hh))
    dh = hh[0].shape[1]
    mu = [jnp.mean(hh[h], axis=1, keepdims=True) for h in heads]
    cen = [hh[h] - mu[h] for h in heads]
    var = [jnp.mean(cen[h] * cen[h], axis=1, keepdims=True) for h in heads]
    for h in heads:
        sl = slice(h * dh, (h + 1) * dh)
        hn = cen[h] * lax.rsqrt(var[h] + EPS) * ng_ref[:, sl] + sk_ref[:, sl] * xc[:, sl]
        y_ref[:, sl] = (hn * gm_ref[:, sl].astype(F32)).astype(y_ref.dtype)


def _mlstm(xm, gm, conv_w, conv_b, wq, wk, wv, wg, gb, ngain, skip):
    b, s, d = xm.shape
    L = MLSTM_CHUNK
    assert s % L == 0
    n_heads, dh, _ = wq.shape
    n_seqs = MLSTM_SEQS
    assert b % n_seqs == 0
    seq_spec = pl.BlockSpec((n_seqs, L, d), lambda bi, ci: (bi, ci, 0))

    def const(shape):
        nd = len(shape)
        return pl.BlockSpec(shape, lambda bi, ci: (0,) * nd)

    kern = functools.partial(_mlstm_kernel, k_scale=dh ** -0.5)
    return pl.pallas_call(
        kern,
        grid=(b // n_seqs, s // L),
        in_specs=[seq_spec, seq_spec, const(conv_w.shape), const(conv_b.shape), const(wq.shape),
                  const(wk.shape), const(wv.shape), const(wg.shape), const(gb.shape),
                  const(ngain.shape), const(skip.shape)],
        out_specs=seq_spec,
        out_shape=jax.ShapeDtypeStruct((b, s, d), BF16),
        scratch_shapes=[
            pltpu.VMEM((n_seqs, SUBLANES, d), F32),
            pltpu.VMEM((n_seqs, n_heads, dh, dh), F32),
            pltpu.VMEM((n_seqs, n_heads, 1, dh), F32),
            pltpu.VMEM((n_seqs, n_heads, 1, LANES), F32),
            pltpu.VMEM((L, L), BF16),
            pltpu.VMEM((CONV_WIDTH - 1, L, L), BF16),
            pltpu.VMEM((2, d, wg.shape[1]), BF16),
        ],
        compiler_params=pltpu.CompilerParams(
            dimension_semantics=("arbitrary", "arbitrary"), vmem_limit_bytes=VMEM_LIMIT),
        name="mlstm",
    )(xm, gm, conv_w, conv_b, wq, wk, wv, wg, gb, ngain, skip)


def _fox_kernel(q_ref, k_ref, v_ref, cr_ref, gf_ref, g_ref, o_ref):
    s_len, dh = q_ref.shape
    tq = FOX_Q
    causal = _tril_mask(tq)
    gain = g_ref[...]
    c_keys = cr_ref[pl.ds(pl.program_id(1), 1), :]

    for qi in reversed(range(s_len // tq)):
        r0 = qi * tq
        kv = r0 + tq
        s = lax.dot_general(q_ref[r0:kv, :], k_ref[0:kv, :], (((1,), (1,)), ((), ())),
                            preferred_element_type=F32)
        s = s - c_keys[:, 0:kv]
        s_diag = jnp.where(causal, s[:, r0:], NEG_INF)
        s = s_diag if qi == 0 else jnp.concatenate([s[:, :r0], s_diag], axis=1)
        m = jnp.max(s, axis=1, keepdims=True)
        p = jnp.exp2(s - m).astype(BF16)
        v_ext = jnp.concatenate([v_ref[0:kv, :], jnp.ones((kv, dh), BF16)], axis=1)
        acc = jnp.dot(p, v_ext, preferred_element_type=F32)

        out = acc[:, :dh] / acc[:, dh:]
        ms = jnp.mean(out * out, axis=1, keepdims=True)
        o_ref[r0:kv, :] = (out * lax.rsqrt(ms + EPS) * gain
                           * gf_ref[r0:kv, :].astype(F32)).astype(o_ref.dtype)


def _fox(q, k, v, c_row, gf, gain):
    b, s, d = q.shape
    n_heads = N_FOX_HEADS
    dh = d // n_heads
    assert dh == LANES and s % FOX_Q == 0
    head_spec = pl.BlockSpec((None, s, dh), lambda bi, hi: (bi, 0, hi))
    return pl.pallas_call(
        _fox_kernel,
        grid=(b, n_heads),
        in_specs=[head_spec, head_spec, head_spec,
                  pl.BlockSpec((c_row.shape[0], s), lambda bi, hi: (0, bi)),
                  head_spec,
                  pl.BlockSpec((1, dh), lambda bi, hi: (0, hi))],
        out_specs=head_spec,
        out_shape=jax.ShapeDtypeStruct((b, s, d), BF16),
        compiler_params=pltpu.CompilerParams(
            dimension_semantics=("arbitrary", "arbitrary"), vmem_limit_bytes=VMEM_LIMIT),
        name="fox",
    )(q, k, v, c_row, gf, gain)


def _out_proj_kernel(ym_ref, yf_ref, x_ref, w_ref, g_ref, o_ref, wb_ref, *, final_norm):
    d_m = ym_ref.shape[1]

    @pl.when(pl.program_id(0) == 0)
    def _():
        wb_ref[...] = w_ref[...].astype(BF16)

    rows = x_ref.shape[0]
    for r0 in range(0, rows, OUT_PROJ_SUB):
        sl = slice(r0, r0 + OUT_PROJ_SUB)
        y = jnp.dot(ym_ref[sl, :], wb_ref[0:d_m, :], preferred_element_type=F32)
        y = y + jnp.dot(yf_ref[sl, :], wb_ref[d_m:, :], preferred_element_type=F32)
        r = x_ref[sl, :] + y
        if final_norm:
            ms = jnp.mean(r * r, axis=-1, keepdims=True)
            r = r * lax.rsqrt(ms + EPS) * g_ref[...]
        o_ref[sl, :] = r


def _out_proj(ym, yf, x2, w_out, gain, final_norm):
    t_rows, d = x2.shape
    rows = OUT_PROJ_ROWS
    assert t_rows % rows == 0
    row = lambda width: pl.BlockSpec((rows, width), lambda t: (t, 0))
    const = lambda shape: pl.BlockSpec(shape, lambda t: (0, 0), pipeline_mode=pl.Buffered(1))
    kern = functools.partial(_out_proj_kernel, final_norm=final_norm)
    return pl.pallas_call(
        kern,
        grid=(t_rows // rows,),
        in_specs=[row(ym.shape[1]), row(yf.shape[1]), row(d), const(w_out.shape), const((1, d))],
        out_specs=row(d),
        out_shape=jax.ShapeDtypeStruct((t_rows, d), F32),
        scratch_shapes=[pltpu.VMEM(w_out.shape, BF16)],
        compiler_params=pltpu.CompilerParams(
            dimension_semantics=("arbitrary",), vmem_limit_bytes=VMEM_LIMIT),
        name="out_proj",
    )(ym, yf, x2, w_out, gain)


def _block_diag(w, dh):
    nb, bs, _ = w.shape
    d = nb * bs
    tiled = jnp.broadcast_to(w.reshape(d, 1, bs), (d, dh // bs, bs)).reshape(d, dh)
    r = lax.broadcasted_iota(jnp.int32, (d, dh), 0)
    c = lax.broadcasted_iota(jnp.int32, (d, dh), 1)
    keep = (r % dh) // bs == c // bs
    return jnp.where(keep, tiled, 0.0).reshape(d // dh, dh, dh)


def kernel(x, norm_gain, w_in, conv_w, conv_b, w_q_m, w_k_m, w_v_m, w_igate, b_igate, w_fgate,
           b_fgate, mlstm_norm_gain, mlstm_skip, fox_forget_bias, fox_norm_gain, w_out,
           final_norm_gain):
    b, s, d = x.shape
    depth = norm_gain.shape[0]
    d_m = conv_w.shape[-1]
    d_f = fox_norm_gain.shape[-1]
    dh_m = d_m // N_MLSTM_HEADS
    dh_f = d_f // N_FOX_HEADS
    n_main = 2 * d_m + 4 * d_f
    assert d_m == d and d_f == d

    x2 = x.reshape(b * s, d)
    for l in range(depth):
        assert w_in.shape[2] == n_main + N_FOX_HEADS
        f_bias = jnp.pad(fox_forget_bias[l], (0, LANES - N_FOX_HEADS)).reshape(1, LANES)
        xm, gm, qf, kf, vf, gf, c_keys = _in_proj(
            x2, norm_gain[l].reshape(1, d), w_in[l].T, f_bias, s, dh_f ** -0.5 * LOG2E)

        lane_pad = ((0, 0), (0, LANES - N_MLSTM_HEADS))
        wg = jnp.concatenate([jnp.pad(w_igate[l], lane_pad), jnp.pad(w_fgate[l], lane_pad)], axis=1)
        gb = jnp.concatenate([jnp.pad(b_igate[l][None], lane_pad),
                              jnp.pad(b_fgate[l][None], lane_pad)], axis=1)
        y_m = _mlstm(xm.reshape(b, s, d_m), gm.reshape(b, s, d_m), conv_w[l],
                     conv_b[l].reshape(1, d_m),
                     _block_diag(w_q_m[l], dh_m).astype(BF16),
                     _block_diag(w_k_m[l], dh_m).astype(BF16),
                     _block_diag(w_v_m[l], dh_m).astype(BF16),
                     wg.astype(BF16), gb,
                     mlstm_norm_gain[l].reshape(1, d_m), mlstm_skip[l].reshape(1, d_m))

        y_f = _fox(qf.reshape(b, s, d_f), kf.reshape(b, s, d_f), vf.reshape(b, s, d_f), c_keys,
                   gf.reshape(b, s, d_f), fox_norm_gain[l].reshape(1, d_f))

        x2 = _out_proj(y_m.reshape(b * s, d_m), y_f.reshape(b * s, d_f), x2, w_out[l],
                       final_norm_gain.reshape(1, d), final_norm=(l == depth - 1))
    return x2.reshape(b, s, d)
```

```python
import functools
import math

import jax
import jax.numpy as jnp
from jax import lax
from jax.experimental import pallas as pl
from jax.experimental.pallas import tpu as pltpu

EPS = 1e-6
N_MLSTM_HEADS = 4
N_FOX_HEADS = 8
QKV_BLOCK = 4
CONV_WIDTH = 4

LANES = 128
SUBLANES = 8
VMEM_LIMIT = 56 * 1024 * 1024

IN_PROJ_ROWS = 512
IN_PROJ_SUB = 256
OUT_PROJ_ROWS = 1024
OUT_PROJ_SUB = 256
MLSTM_CHUNK = 256
MLSTM_SEQS = 2
FOX_LOOKAHEAD = 2
FOX_Q = 256
LOG2E = math.log2(math.e)

BF16 = jnp.bfloat16
F32 = jnp.float32
NEG_INF = float("-inf")


def _log_sigmoid(z):
    return jnp.minimum(z, 0.0) - jnp.log(1.0 + jnp.exp(-jnp.abs(z)))


def _sigmoid(z):
    return 1.0 / (1.0 + jnp.exp(-z))


def _silu(z):
    return z * _sigmoid(z)


def _split3(v):
    hi = v.astype(BF16)
    r = v - hi.astype(F32)
    mid = r.astype(BF16)
    lo = (r - mid.astype(F32)).astype(BF16)
    return hi, mid, lo


def _cumsum_rows(tri, v):
    w = v.shape[1]
    cat = jnp.concatenate(_split3(v), axis=1)
    cs = jnp.dot(tri, cat, preferred_element_type=F32)
    return cs[:, :w] + cs[:, w:2 * w] + cs[:, 2 * w:]


def _tril_mask(n):
    row = lax.broadcasted_iota(jnp.int32, (n, n), 0)
    col = lax.broadcasted_iota(jnp.int32, (n, n), 1)
    return col <= row


def _in_proj_kernel(x_ref, g_ref, w_hbm, fb_ref,
                    xm_ref, gm_ref, q_ref, k_ref, v_ref, gf_ref, c_ref,
                    wb_ref, wfb_ref, tri_ref, carry_ref, stage_ref, gstage_ref, sems,
                    *, tiles_per_seq, q_scale):
    t = pl.program_id(0)
    rows, d = x_ref.shape
    sub = tri_ref.shape[0]
    n_main = wb_ref.shape[1]
    n_gate = gstage_ref.shape[0]
    n_chunks = n_main // d

    def chunk_copy(c):
        return pltpu.make_async_copy(w_hbm.at[pl.ds(c * d, d), :], stage_ref.at[c % 2], sems.at[c % 2])

    def gate_copy():
        return pltpu.make_async_copy(w_hbm.at[pl.ds(n_main, n_gate), :], gstage_ref, sems.at[2])

    @pl.when(t == 0)
    def _():
        gate_copy().start()
        chunk_copy(0).start()
        tri_ref[...] = jnp.where(_tril_mask(sub), 1.0, 0.0).astype(BF16)
        for c in range(n_chunks):
            if c + 1 < n_chunks:
                chunk_copy(c + 1).start()
            chunk_copy(c).wait()
            wb_ref[:, c * d:(c + 1) * d] = stage_ref[c % 2].T.astype(BF16)
        gate_copy().wait()
        w_gate = jnp.concatenate([gstage_ref[...], jnp.zeros((LANES - n_gate, d), F32)], axis=0)
        wfb_ref[...] = w_gate.T.astype(BF16)

    @pl.when(t % tiles_per_seq == 0)
    def _():
        carry_ref[...] = jnp.zeros_like(carry_ref)

    blocks = [slice(r0, r0 + sub) for r0 in range(0, rows, sub)]
    hns = []
    for sl in blocks:
        x = x_ref[sl, :]
        ms = jnp.mean(x * x, axis=-1, keepdims=True)
        hns.append((x * lax.rsqrt(ms + EPS) * g_ref[...]).astype(BF16))

    carry = carry_ref[...]
    for sl, hn in zip(blocks, hns):
        def proj(idx):
            return jnp.dot(hn, wb_ref[:, idx * d:(idx + 1) * d], preferred_element_type=F32)

        xm_ref[sl, :] = proj(0).astype(xm_ref.dtype)
        gm_ref[sl, :] = _silu(proj(1)).astype(gm_ref.dtype)
        q_ref[sl, :] = (proj(2) * q_scale).astype(q_ref.dtype)
        k_ref[sl, :] = proj(3).astype(k_ref.dtype)
        v_ref[sl, :] = proj(4).astype(v_ref.dtype)
        gf_ref[sl, :] = _silu(proj(5)).astype(gf_ref.dtype)

        f = jnp.dot(hn, wfb_ref[...], preferred_element_type=F32) + fb_ref[...]
        c = _cumsum_rows(tri_ref[...], _log_sigmoid(f)) + carry
        c_ref[:, sl] = (c * LOG2E).T[0:c_ref.shape[0], :]
        carry = c[sub - 1:sub, :]
    carry_ref[...] = carry


def _in_proj(x2, gain, w_in_t, f_bias, seq_len, q_scale):
    t_rows, d = x2.shape
    rows, sub = IN_PROJ_ROWS, IN_PROJ_SUB
    assert t_rows % rows == 0 and seq_len % rows == 0 and rows % sub == 0
    n_main = 6 * d
    n_gate = w_in_t.shape[0] - n_main
    assert 0 < n_gate <= SUBLANES and w_in_t.shape[1] == d
    row_spec = pl.BlockSpec((rows, d), lambda t: (t, 0))
    const = lambda shape: pl.BlockSpec(shape, lambda t: (0, 0), pipeline_mode=pl.Buffered(1))
    kern = functools.partial(_in_proj_kernel, tiles_per_seq=seq_len // rows, q_scale=q_scale)
    return pl.pallas_call(
        kern,
        grid=(t_rows // rows,),
        in_specs=[row_spec, const((1, d)), pl.BlockSpec(memory_space=pl.ANY), const((1, LANES))],
        out_specs=[row_spec] * 6 + [pl.BlockSpec((SUBLANES, rows), lambda t: (0, t))],
        out_shape=[jax.ShapeDtypeStruct((t_rows, d), BF16)] * 6
        + [jax.ShapeDtypeStruct((SUBLANES, t_rows), F32)],
        scratch_shapes=[pltpu.VMEM((d, n_main), BF16), pltpu.VMEM((d, LANES), BF16),
                        pltpu.VMEM((sub, sub), BF16), pltpu.VMEM((1, LANES), F32),
                        pltpu.VMEM((2, d, d), F32), pltpu.VMEM((n_gate, d), F32),
                        pltpu.SemaphoreType.DMA((3,))],
        compiler_params=pltpu.CompilerParams(
            dimension_semantics=("arbitrary",), vmem_limit_bytes=VMEM_LIMIT),
        name="in_proj",
    )(x2, gain, w_in_t, f_bias)


def _mlstm_kernel(xm_ref, gm_ref, cw_ref, cb_ref, wq_ref, wk_ref, wv_ref, wg_ref, gb_ref,
                  ng_ref, sk_ref, y_ref,
                  tail_ref, c_sc, n_sc, m_sc, tri_ref, shift_ref, gw_ref, *, k_scale):
    ci = pl.program_id(1)
    n_seqs, L, d_m = xm_ref.shape
    n_heads, dh, _ = wq_ref.shape

    @pl.when(ci == 0)
    def _():
        for h in range(n_heads):
            sl = slice(h * dh, (h + 1) * dh)
            g_q = jnp.dot(wq_ref[h], wg_ref[h * dh:(h + 1) * dh, :], preferred_element_type=F32)
            g_k = jnp.dot(wk_ref[h], wg_ref[d_m + h * dh:d_m + (h + 1) * dh, :],
                          preferred_element_type=F32)
            g_v = jnp.dot(wv_ref[h], wg_ref[2 * d_m + h * dh:2 * d_m + (h + 1) * dh, :],
                          preferred_element_type=F32)
            gw_ref[0, sl, :] = (g_q + g_k * k_scale).astype(BF16)
            gw_ref[1, sl, :] = g_v.astype(BF16)
        c_sc[...] = jnp.zeros_like(c_sc)
        n_sc[...] = jnp.zeros_like(n_sc)
        m_sc[...] = jnp.zeros_like(m_sc)
        tail_ref[...] = jnp.zeros_like(tail_ref)
        row = lax.broadcasted_iota(jnp.int32, (L, L), 0)
        col = lax.broadcasted_iota(jnp.int32, (L, L), 1)
        tri_ref[...] = jnp.where(col <= row, 1.0, 0.0).astype(BF16)
        for j in range(1, CONV_WIDTH):
            shift_ref[j - 1] = jnp.where(col == row - j, 1.0, 0.0).astype(BF16)

    seqs = range(n_seqs)
    fronts = [_mlstm_front(xm_ref.at[i], cw_ref, cb_ref, wq_ref, wk_ref, wv_ref, gw_ref, gb_ref,
                           tail_ref.at[i], tri_ref, shift_ref, k_scale) for i in seqs]
    for i in seqs:
        mixed = _mlstm_mix(*fronts[i][1:], c_sc.at[i], n_sc.at[i], m_sc.at[i])
        _mlstm_out(mixed, fronts[i][0], gm_ref.at[i], ng_ref, sk_ref, y_ref.at[i])


def _mlstm_front(xm_ref, cw_ref, cb_ref, wq_ref, wk_ref, wv_ref, gw_ref, gb_ref,
                 tail_ref, tri_ref, shift_ref, k_scale):
    L, d_m = xm_ref.shape
    dh = wq_ref.shape[1]
    n_heads = wq_ref.shape[0]
    halo = SUBLANES

    x_b = xm_ref[...]
    x_f = x_b.astype(F32)
    conv = cb_ref[...] + x_f * cw_ref[CONV_WIDTH - 1:CONV_WIDTH, :]
    edge = jnp.concatenate([tail_ref[...], jnp.zeros((halo, d_m), F32)], axis=0)
    head = jnp.zeros((halo, d_m), F32)
    for j in range(1, CONV_WIDTH):
        w_j = cw_ref[CONV_WIDTH - 1 - j:CONV_WIDTH - j, :]
        conv = conv + jnp.dot(shift_ref[j - 1], x_b, preferred_element_type=F32) * w_j
        head = head + edge[halo - j:2 * halo - j, :] * w_j
    conv = jnp.concatenate([conv[:halo] + head, conv[halo:]], axis=0)
    tail_ref[...] = x_f[L - halo:L, :]
    xc = _silu(conv)
    xc_b = xc.astype(BF16)

    gates = (jnp.dot(xc_b, gw_ref[0], preferred_element_type=F32)
             + jnp.dot(x_b, gw_ref[1], preferred_element_type=F32) + gb_ref[...])
    qs, ks, vs = [], [], []
    for h in range(n_heads):
        sl = slice(h * dh, (h + 1) * dh)
        q = jnp.dot(xc_b[:, sl], wq_ref[h], preferred_element_type=F32)
        k = jnp.dot(xc_b[:, sl], wk_ref[h], preferred_element_type=F32) * k_scale
        v = jnp.dot(x_b[:, sl], wv_ref[h], preferred_element_type=F32)
        qs.append(q.astype(BF16))
        ks.append((k, k.astype(BF16)))
        vs.append(v.astype(BF16))

    ig = gates[:, :LANES]
    bcum = _cumsum_rows(tri_ref[...], _log_sigmoid(gates[:, LANES:]))
    a_col = ig - bcum
    return xc, qs, ks, vs, a_col, bcum


def _mlstm_mix(qs, ks, vs, a_col, bcum, c_sc, n_sc, m_sc):
    n_heads = len(qs)
    L, dh = qs[0].shape
    a_rows = a_col.T
    causal = _tril_mask(L)
    ones_blk = jnp.ones((L, LANES), BF16)

    heads = range(n_heads)
    nt = (((1,), (1,)), ((), ()))
    m_prev = [m_sc[h][:, 0:1] for h in heads]
    dm = [jnp.where(causal, a_rows[h:h + 1, :], NEG_INF) for h in heads]
    m_run = [jnp.maximum(m_prev[h], jnp.max(dm[h], axis=1, keepdims=True)) for h in heads]
    w_inter = [jnp.exp(m_prev[h] - m_run[h]) for h in heads]
    s_b = [(lax.dot_general(qs[h], ks[h][1], nt, preferred_element_type=F32)
            * jnp.exp(dm[h] - m_run[h])).astype(BF16) for h in heads]
    num = [w_inter[h] * jnp.dot(qs[h], c_sc[h].astype(BF16), preferred_element_type=F32)
           + jnp.dot(s_b[h], vs[h], preferred_element_type=F32) for h in heads]
    qn = [lax.dot_general(qs[h], jnp.broadcast_to(n_sc[h], (LANES, dh)).astype(BF16), nt,
                          preferred_element_type=F32) for h in heads]
    den = [w_inter[h] * qn[h] + jnp.dot(s_b[h], ones_blk, preferred_element_type=F32)
           for h in heads]
    m_t = [bcum[:, h:h + 1] + m_run[h] for h in heads]
    inv = [1.0 / jnp.maximum(jnp.abs(den[h]), jnp.exp(-m_t[h])) for h in heads]
    hh = [num[h] * jnp.concatenate([inv[h]] * (dh // LANES), axis=1) for h in heads]

    for h in heads:
        m_last = m_run[h][L - 1:L, :]
        decay = jnp.exp(m_prev[h] - m_last)
        kw = ks[h][0] * jnp.exp(a_col[:, h:h + 1] - m_last)
        c_sc[h] = decay * c_sc[h] + lax.dot_general(
            kw.astype(BF16), vs[h], (((0,), (0,)), ((), ())), preferred_element_type=F32)
        n_sc[h] = decay * n_sc[h] + jnp.sum(kw, axis=0, keepdims=True)
        m_sc[h] = jnp.broadcast_to(m_t[h][L - 1:L, :], m_sc.shape[1:])
    return hh


def _mlstm_out(hh, xc, gm_ref, ng_ref, sk_ref, y_ref):
    heads = range(len(hh))
    dh = hh[0].shape[1]
    mu = [jnp.mean(hh[h], axis=1, keepdims=True) for h in heads]
    cen = [hh[h] - mu[h] for h in heads]
    var = [jnp.mean(cen[h] * cen[h], axis=1, keepdims=True) for h in heads]
    for h in heads:
        sl = slice(h * dh, (h + 1) * dh)
        hn = cen[h] * lax.rsqrt(var[h] + EPS) * ng_ref[:, sl] + sk_ref[:, sl] * xc[:, sl]
        y_ref[:, sl] = (hn * gm_ref[:, sl].astype(F32)).astype(y_ref.dtype)


def _mlstm(xm, gm, conv_w, conv_b, wq, wk, wv, wg, gb, ngain, skip):
    b, s, d = xm.shape
    L = MLSTM_CHUNK
    assert s % L == 0
    n_heads, dh, _ = wq.shape
    n_seqs = MLSTM_SEQS
    assert b % n_seqs == 0
    seq_spec = pl.BlockSpec((n_seqs, L, d), lambda bi, ci: (bi, ci, 0))

    def const(shape):
        nd = len(shape)
        return pl.BlockSpec(shape, lambda bi, ci: (0,) * nd)

    kern = functools.partial(_mlstm_kernel, k_scale=dh ** -0.5)
    return pl.pallas_call(
        kern,
        grid=(b // n_seqs, s // L),
        in_specs=[seq_spec, seq_spec, const(conv_w.shape), const(conv_b.shape), const(wq.shape),
                  const(wk.shape), const(wv.shape), const(wg.shape), const(gb.shape),
                  const(ngain.shape), const(skip.shape)],
        out_specs=seq_spec,
        out_shape=jax.ShapeDtypeStruct((b, s, d), BF16),
        scratch_shapes=[
            pltpu.VMEM((n_seqs, SUBLANES, d), F32),
            pltpu.VMEM((n_seqs, n_heads, dh, dh), F32),
            pltpu.VMEM((n_seqs, n_heads, 1, dh), F32),
            pltpu.VMEM((n_seqs, n_heads, 1, LANES), F32),
            pltpu.VMEM((L, L), BF16),
            pltpu.VMEM((CONV_WIDTH - 1, L, L), BF16),
            pltpu.VMEM((2, d, wg.shape[1]), BF16),
        ],
        compiler_params=pltpu.CompilerParams(
            dimension_semantics=("arbitrary", "arbitrary"), vmem_limit_bytes=VMEM_LIMIT),
        name="mlstm",
    )(xm, gm, conv_w, conv_b, wq, wk, wv, wg, gb, ngain, skip)


def _fox_kernel(q_ref, k_ref, v_ref, cr_ref, gf_ref, g_ref, o_ref):
    s_len, dh = q_ref.shape
    tq = FOX_Q
    causal = _tril_mask(tq)
    gain = g_ref[...]
    c_keys = cr_ref[pl.ds(pl.program_id(1), 1), :]

    def scores(qi):
        r0 = qi * tq
        return lax.dot_general(q_ref[r0:r0 + tq, :], k_ref[0:r0 + tq, :], (((1,), (1,)), ((), ())),
                               preferred_element_type=F32)

    order = list(reversed(range(s_len // tq)))
    ahead = [scores(qi) for qi in order[:FOX_LOOKAHEAD]]
    for pos, qi in enumerate(order):
        r0 = qi * tq
        kv = r0 + tq
        s = ahead.pop(0)
        if pos + FOX_LOOKAHEAD < len(order):
            ahead.append(scores(order[pos + FOX_LOOKAHEAD]))
        s = s - c_keys[:, 0:kv]
        s_diag = jnp.where(causal, s[:, r0:], NEG_INF)
        s = s_diag if qi == 0 else jnp.concatenate([s[:, :r0], s_diag], axis=1)
        m = jnp.max(s, axis=1, keepdims=True)
        p = jnp.exp2(s - m).astype(BF16)
        v_ext = jnp.concatenate([v_ref[0:kv, :], jnp.ones((kv, dh), BF16)], axis=1)
        acc = jnp.dot(p, v_ext, preferred_element_type=F32)

        out = acc[:, :dh] / acc[:, dh:]
        ms = jnp.mean(out * out, axis=1, keepdims=True)
        o_ref[r0:kv, :] = (out * lax.rsqrt(ms + EPS) * gain
                           * gf_ref[r0:kv, :].astype(F32)).astype(o_ref.dtype)


def _fox(q, k, v, c_row, gf, gain):
    b, s, d = q.shape
    n_heads = N_FOX_HEADS
    dh = d // n_heads
    assert dh == LANES and s % FOX_Q == 0
    head_spec = pl.BlockSpec((None, s, dh), lambda bi, hi: (bi, 0, hi))
    return pl.pallas_call(
        _fox_kernel,
        grid=(b, n_heads),
        in_specs=[head_spec, head_spec, head_spec,
                  pl.BlockSpec((c_row.shape[0], s), lambda bi, hi: (0, bi)),
                  head_spec,
                  pl.BlockSpec((1, dh), lambda bi, hi: (0, hi))],
        out_specs=head_spec,
        out_shape=jax.ShapeDtypeStruct((b, s, d), BF16),
        compiler_params=pltpu.CompilerParams(
            dimension_semantics=("arbitrary", "arbitrary"), vmem_limit_bytes=VMEM_LIMIT),
        name="fox",
    )(q, k, v, c_row, gf, gain)


def _out_proj_kernel(ym_ref, yf_ref, x_ref, w_ref, g_ref, o_ref, wb_ref, *, final_norm):
    d_m = ym_ref.shape[1]

    @pl.when(pl.program_id(0) == 0)
    def _():
        wb_ref[...] = w_ref[...].astype(BF16)

    rows = x_ref.shape[0]
    for r0 in range(0, rows, OUT_PROJ_SUB):
        sl = slice(r0, r0 + OUT_PROJ_SUB)
        y = jnp.dot(ym_ref[sl, :], wb_ref[0:d_m, :], preferred_element_type=F32)
        y = y + jnp.dot(yf_ref[sl, :], wb_ref[d_m:, :], preferred_element_type=F32)
        r = x_ref[sl, :] + y
        if final_norm:
            ms = jnp.mean(r * r, axis=-1, keepdims=True)
            r = r * lax.rsqrt(ms + EPS) * g_ref[...]
        o_ref[sl, :] = r


def _out_proj(ym, yf, x2, w_out, gain, final_norm):
    t_rows, d = x2.shape
    rows = OUT_PROJ_ROWS
    assert t_rows % rows == 0
    row = lambda width: pl.BlockSpec((rows, width), lambda t: (t, 0))
    const = lambda shape: pl.BlockSpec(shape, lambda t: (0, 0), pipeline_mode=pl.Buffered(1))
    kern = functools.partial(_out_proj_kernel, final_norm=final_norm)
    return pl.pallas_call(
        kern,
        grid=(t_rows // rows,),
        in_specs=[row(ym.shape[1]), row(yf.shape[1]), row(d), const(w_out.shape), const((1, d))],
        out_specs=row(d),
        out_shape=jax.ShapeDtypeStruct((t_rows, d), F32),
        scratch_shapes=[pltpu.VMEM(w_out.shape, BF16)],
        compiler_params=pltpu.CompilerParams(
            dimension_semantics=("arbitrary",), vmem_limit_bytes=VMEM_LIMIT),
        name="out_proj",
    )(ym, yf, x2, w_out, gain)


def _block_diag(w, dh):
    nb, bs, _ = w.shape
    d = nb * bs
    tiled = jnp.broadcast_to(w.reshape(d, 1, bs), (d, dh // bs, bs)).reshape(d, dh)
    r = lax.broadcasted_iota(jnp.int32, (d, dh), 0)
    c = lax.broadcasted_iota(jnp.int32, (d, dh), 1)
    keep = (r % dh) // bs == c // bs
    return jnp.where(keep, tiled, 0.0).reshape(d // dh, dh, dh)


def kernel(x, norm_gain, w_in, conv_w, conv_b, w_q_m, w_k_m, w_v_m, w_igate, b_igate, w_fgate,
           b_fgate, mlstm_norm_gain, mlstm_skip, fox_forget_bias, fox_norm_gain, w_out,
           final_norm_gain):
    b, s, d = x.shape
    depth = norm_gain.shape[0]
    d_m = conv_w.shape[-1]
    d_f = fox_norm_gain.shape[-1]
    dh_m = d_m // N_MLSTM_HEADS
    dh_f = d_f // N_FOX_HEADS
    n_main = 2 * d_m + 4 * d_f
    assert d_m == d and d_f == d

    x2 = x.reshape(b * s, d)
    for l in range(depth):
        assert w_in.shape[2] == n_main + N_FOX_HEADS
        f_bias = jnp.pad(fox_forget_bias[l], (0, LANES - N_FOX_HEADS)).reshape(1, LANES)
        xm, gm, qf, kf, vf, gf, c_keys = _in_proj(
            x2, norm_gain[l].reshape(1, d), w_in[l].T, f_bias, s, dh_f ** -0.5 * LOG2E)

        lane_pad = ((0, 0), (0, LANES - N_MLSTM_HEADS))
        wg = jnp.concatenate([jnp.pad(w_igate[l], lane_pad), jnp.pad(w_fgate[l], lane_pad)], axis=1)
        gb = jnp.concatenate([jnp.pad(b_igate[l][None], lane_pad),
                              jnp.pad(b_fgate[l][None], lane_pad)], axis=1)
        y_m = _mlstm(xm.reshape(b, s, d_m), gm.reshape(b, s, d_m), conv_w[l],
                     conv_b[l].reshape(1, d_m),
                     _block_diag(w_q_m[l], dh_m).astype(BF16),
                     _block_diag(w_k_m[l], dh_m).astype(BF16),
                     _block_diag(w_v_m[l], dh_m).astype(BF16),
                     wg.astype(BF16), gb,
                     mlstm_norm_gain[l].reshape(1, d_m), mlstm_skip[l].reshape(1, d_m))

        y_f = _fox(qf.reshape(b, s, d_f), kf.reshape(b, s, d_f), vf.reshape(b, s, d_f), c_keys,
                   gf.reshape(b, s, d_f), fox_norm_gain[l].reshape(1, d_f))

        x2 = _out_proj(y_m.reshape(b * s, d_m), y_f.reshape(b * s, d_f), x2, w_out[l],
                       final_norm_gain.reshape(1, d), final_norm=(l == depth - 1))
    return x2.reshape(b, s, d)
```

```python
import functools
import math

import jax
import jax.numpy as jnp
from jax import lax
from jax.experimental import pallas as pl
from jax.experimental.pallas import tpu as pltpu

EPS = 1e-6
N_MLSTM_HEADS = 4
N_FOX_HEADS = 8
QKV_BLOCK = 4
CONV_WIDTH = 4

LANES = 128
SUBLANES = 8
VMEM_LIMIT = 56 * 1024 * 1024

IN_PROJ_ROWS = 512
IN_PROJ_SUB = 256
OUT_PROJ_ROWS = 1024
OUT_PROJ_SUB = 256
MLSTM_CHUNK = 256
MLSTM_SEQS = 2
FOX_LOOKAHEAD = 2
FOX_Q = 256
LOG2E = math.log2(math.e)

BF16 = jnp.bfloat16
F32 = jnp.float32
NEG_INF = float("-inf")


def _log_sigmoid(z):
    return jnp.minimum(z, 0.0) - jnp.log(1.0 + jnp.exp(-jnp.abs(z)))


def _sigmoid(z):
    return 1.0 / (1.0 + jnp.exp(-z))


def _silu(z):
    return z * _sigmoid(z)


def _split3(v):
    hi = v.astype(BF16)
    r = v - hi.astype(F32)
    mid = r.astype(BF16)
    lo = (r - mid.astype(F32)).astype(BF16)
    return hi, mid, lo


def _cumsum_rows(tri, v):
    w = v.shape[1]
    cat = jnp.concatenate(_split3(v), axis=1)
    cs = jnp.dot(tri, cat, preferred_element_type=F32)
    return cs[:, :w] + cs[:, w:2 * w] + cs[:, 2 * w:]


def _tril_mask(n):
    row = lax.broadcasted_iota(jnp.int32, (n, n), 0)
    col = lax.broadcasted_iota(jnp.int32, (n, n), 1)
    return col <= row


def _in_proj_kernel(x_ref, g_ref, w_hbm, fb_ref,
                    xm_ref, gm_ref, q_ref, k_ref, v_ref, gf_ref, c_ref,
                    wb_ref, wfb_ref, tri_ref, carry_ref, stage_ref, gstage_ref, sems,
                    *, tiles_per_seq, q_scale):
    t = pl.program_id(0)
    rows, d = x_ref.shape
    sub = tri_ref.shape[0]
    n_main = wb_ref.shape[1]
    n_gate = gstage_ref.shape[0]
    n_chunks = n_main // d

    def chunk_copy(c):
        return pltpu.make_async_copy(w_hbm.at[pl.ds(c * d, d), :], stage_ref.at[c % 2], sems.at[c % 2])

    def gate_copy():
        return pltpu.make_async_copy(w_hbm.at[pl.ds(n_main, n_gate), :], gstage_ref, sems.at[2])

    @pl.when(t == 0)
    def _():
        gate_copy().start()
        chunk_copy(0).start()
        tri_ref[...] = jnp.where(_tril_mask(sub), 1.0, 0.0).astype(BF16)
        for c in range(n_chunks):
            if c + 1 < n_chunks:
                chunk_copy(c + 1).start()
            chunk_copy(c).wait()
            wb_ref[:, c * d:(c + 1) * d] = stage_ref[c % 2].T.astype(BF16)
        gate_copy().wait()
        w_gate = jnp.concatenate([gstage_ref[...], jnp.zeros((LANES - n_gate, d), F32)], axis=0)
        wfb_ref[...] = w_gate.T.astype(BF16)

    @pl.when(t % tiles_per_seq == 0)
    def _():
        carry_ref[...] = jnp.zeros_like(carry_ref)

    blocks = [slice(r0, r0 + sub) for r0 in range(0, rows, sub)]
    hns = []
    for sl in blocks:
        x = x_ref[sl, :]
        ms = jnp.mean(x * x, axis=-1, keepdims=True)
        hns.append((x * lax.rsqrt(ms + EPS) * g_ref[...]).astype(BF16))

    def proj(hn, idx):
        return jnp.dot(hn, wb_ref[:, idx * d:(idx + 1) * d], preferred_element_type=F32)

    for sl, hn in zip(blocks, hns):
        xm_ref[sl, :] = proj(hn, 0).astype(xm_ref.dtype)
    log_f = [_log_sigmoid(jnp.dot(hn, wfb_ref[...], preferred_element_type=F32) + fb_ref[...])
             for hn in hns]
    for sl, hn in zip(blocks, hns):
        gm_ref[sl, :] = _silu(proj(hn, 1)).astype(gm_ref.dtype)
    for sl, hn in zip(blocks, hns):
        gf_ref[sl, :] = _silu(proj(hn, 5)).astype(gf_ref.dtype)
    carry = carry_ref[...]
    for sl, lf in zip(blocks, log_f):
        c = _cumsum_rows(tri_ref[...], lf) + carry
        c_ref[:, sl] = (c * LOG2E).T[0:c_ref.shape[0], :]
        carry = c[sub - 1:sub, :]
    carry_ref[...] = carry
    for sl, hn in zip(blocks, hns):
        q_ref[sl, :] = (proj(hn, 2) * q_scale).astype(q_ref.dtype)
    for sl, hn in zip(blocks, hns):
        k_ref[sl, :] = proj(hn, 3).astype(k_ref.dtype)
    for sl, hn in zip(blocks, hns):
        v_ref[sl, :] = proj(hn, 4).astype(v_ref.dtype)


def _in_proj(x2, gain, w_in_t, f_bias, seq_len, q_scale):
    t_rows, d = x2.shape
    rows, sub = IN_PROJ_ROWS, IN_PROJ_SUB
    assert t_rows % rows == 0 and seq_len % rows == 0 and rows % sub == 0
    n_main = 6 * d
    n_gate = w_in_t.shape[0] - n_main
    assert 0 < n_gate <= SUBLANES and w_in_t.shape[1] == d
    row_spec = pl.BlockSpec((rows, d), lambda t: (t, 0))
    const = lambda shape: pl.BlockSpec(shape, lambda t: (0, 0), pipeline_mode=pl.Buffered(1))
    kern = functools.partial(_in_proj_kernel, tiles_per_seq=seq_len // rows, q_scale=q_scale)
    return pl.pallas_call(
        kern,
        grid=(t_rows // rows,),
        in_specs=[row_spec, const((1, d)), pl.BlockSpec(memory_space=pl.ANY), const((1, LANES))],
        out_specs=[row_spec] * 6 + [pl.BlockSpec((SUBLANES, rows), lambda t: (0, t))],
        out_shape=[jax.ShapeDtypeStruct((t_rows, d), BF16)] * 6
        + [jax.ShapeDtypeStruct((SUBLANES, t_rows), F32)],
        scratch_shapes=[pltpu.VMEM((d, n_main), BF16), pltpu.VMEM((d, LANES), BF16),
                        pltpu.VMEM((sub, sub), BF16), pltpu.VMEM((1, LANES), F32),
                        pltpu.VMEM((2, d, d), F32), pltpu.VMEM((n_gate, d), F32),
                        pltpu.SemaphoreType.DMA((3,))],
        compiler_params=pltpu.CompilerParams(
            dimension_semantics=("arbitrary",), vmem_limit_bytes=VMEM_LIMIT),
        name="in_proj",
    )(x2, gain, w_in_t, f_bias)


def _mlstm_kernel(xm_ref, gm_ref, cw_ref, cb_ref, wq_ref, wk_ref, wv_ref, wg_ref, gb_ref,
                  ng_ref, sk_ref, y_ref,
                  tail_ref, c_sc, n_sc, m_sc, tri_ref, shift_ref, gw_ref, *, k_scale):
    ci = pl.program_id(1)
    n_seqs, L, d_m = xm_ref.shape
    n_heads, dh, _ = wq_ref.shape

    @pl.when(ci == 0)
    def _():
        for h in range(n_heads):
            sl = slice(h * dh, (h + 1) * dh)
            g_q = jnp.dot(wq_ref[h], wg_ref[h * dh:(h + 1) * dh, :], preferred_element_type=F32)
            g_k = jnp.dot(wk_ref[h], wg_ref[d_m + h * dh:d_m + (h + 1) * dh, :],
                          preferred_element_type=F32)
            g_v = jnp.dot(wv_ref[h], wg_ref[2 * d_m + h * dh:2 * d_m + (h + 1) * dh, :],
                          preferred_element_type=F32)
            gw_ref[0, sl, :] = (g_q + g_k * k_scale).astype(BF16)
            gw_ref[1, sl, :] = g_v.astype(BF16)
        c_sc[...] = jnp.zeros_like(c_sc)
        n_sc[...] = jnp.zeros_like(n_sc)
        m_sc[...] = jnp.zeros_like(m_sc)
        tail_ref[...] = jnp.zeros_like(tail_ref)
        row = lax.broadcasted_iota(jnp.int32, (L, L), 0)
        col = lax.broadcasted_iota(jnp.int32, (L, L), 1)
        tri_ref[...] = jnp.where(col <= row, 1.0, 0.0).astype(BF16)
        for j in range(1, CONV_WIDTH):
            shift_ref[j - 1] = jnp.where(col == row - j, 1.0, 0.0).astype(BF16)

    seqs = range(n_seqs)
    fronts = [_mlstm_front(xm_ref.at[i], cw_ref, cb_ref, wq_ref, wk_ref, wv_ref, gw_ref, gb_ref,
                           tail_ref.at[i], tri_ref, shift_ref, k_scale) for i in seqs]
    for i in seqs:
        mixed = _mlstm_mix(*fronts[i][1:], c_sc.at[i], n_sc.at[i], m_sc.at[i])
        _mlstm_out(mixed, fronts[i][0], gm_ref.at[i], ng_ref, sk_ref, y_ref.at[i])


def _mlstm_front(xm_ref, cw_ref, cb_ref, wq_ref, wk_ref, wv_ref, gw_ref, gb_ref,
                 tail_ref, tri_ref, shift_ref, k_scale):
    L, d_m = xm_ref.shape
    dh = wq_ref.shape[1]
    n_heads = wq_ref.shape[0]
    halo = SUBLANES

    x_b = xm_ref[...]
    x_f = x_b.astype(F32)
    conv = cb_ref[...] + x_f * cw_ref[CONV_WIDTH - 1:CONV_WIDTH, :]
    edge = jnp.concatenate([tail_ref[...], jnp.zeros((halo, d_m), F32)], axis=0)
    head = jnp.zeros((halo, d_m), F32)
    for j in range(1, CONV_WIDTH):
        w_j = cw_ref[CONV_WIDTH - 1 - j:CONV_WIDTH - j, :]
        conv = conv + jnp.dot(shift_ref[j - 1], x_b, preferred_element_type=F32) * w_j
        head = head + edge[halo - j:2 * halo - j, :] * w_j
    conv = jnp.concatenate([conv[:halo] + head, conv[halo:]], axis=0)
    tail_ref[...] = x_f[L - halo:L, :]
    xc = _silu(conv)
    xc_b = xc.astype(BF16)

    gates = (jnp.dot(xc_b, gw_ref[0], preferred_element_type=F32)
             + jnp.dot(x_b, gw_ref[1], preferred_element_type=F32) + gb_ref[...])
    qs, ks, vs = [], [], []
    for h in range(n_heads):
        sl = slice(h * dh, (h + 1) * dh)
        q = jnp.dot(xc_b[:, sl], wq_ref[h], preferred_element_type=F32)
        k = jnp.dot(xc_b[:, sl], wk_ref[h], preferred_element_type=F32) * k_scale
        v = jnp.dot(x_b[:, sl], wv_ref[h], preferred_element_type=F32)
        qs.append(q.astype(BF16))
        ks.append((k, k.astype(BF16)))
        vs.append(v.astype(BF16))

    ig = gates[:, :LANES]
    bcum = _cumsum_rows(tri_ref[...], _log_sigmoid(gates[:, LANES:]))
    a_col = ig - bcum
    return xc, qs, ks, vs, a_col, bcum


def _mlstm_mix(qs, ks, vs, a_col, bcum, c_sc, n_sc, m_sc):
    n_heads = len(qs)
    L, dh = qs[0].shape
    a_rows = a_col.T
    causal = _tril_mask(L)
    ones_blk = jnp.ones((L, LANES), BF16)

    heads = range(n_heads)
    nt = (((1,), (1,)), ((), ()))
    m_prev = {h: m_sc[h][:, 0:1] for h in heads}
    dm = {h: jnp.where(causal, a_rows[h:h + 1, :], NEG_INF) for h in heads}
    m_run = {h: jnp.maximum(m_prev[h], jnp.max(dm[h], axis=1, keepdims=True)) for h in heads}
    w_inter = {h: jnp.exp(m_prev[h] - m_run[h]) for h in heads}
    s_b = {h: (lax.dot_general(qs[h], ks[h][1], nt, preferred_element_type=F32)
               * jnp.exp(dm[h] - m_run[h])).astype(BF16) for h in heads}
    num = {h: w_inter[h] * jnp.dot(qs[h], c_sc[h].astype(BF16), preferred_element_type=F32)
           + jnp.dot(s_b[h], vs[h], preferred_element_type=F32) for h in heads}
    qn = {h: lax.dot_general(qs[h], jnp.broadcast_to(n_sc[h], (LANES, dh)).astype(BF16), nt,
                             preferred_element_type=F32) for h in heads}
    den = {h: w_inter[h] * qn[h] + jnp.dot(s_b[h], ones_blk, preferred_element_type=F32)
           for h in heads}
    m_t = {h: bcum[:, h:h + 1] + m_run[h] for h in heads}
    inv = {h: 1.0 / jnp.maximum(jnp.abs(den[h]), jnp.exp(-m_t[h])) for h in heads}
    hh = [num[h] * jnp.concatenate([inv[h]] * (dh // LANES), axis=1) for h in heads]

    for h in heads:
        m_last = m_run[h][L - 1:L, :]
        decay = jnp.exp(m_prev[h] - m_last)
        kw = ks[h][0] * jnp.exp(a_col[:, h:h + 1] - m_last)
        c_sc[h] = decay * c_sc[h] + lax.dot_general(
            kw.astype(BF16), vs[h], (((0,), (0,)), ((), ())), preferred_element_type=F32)
        n_sc[h] = decay * n_sc[h] + jnp.sum(kw, axis=0, keepdims=True)
        m_sc[h] = jnp.broadcast_to(m_t[h][L - 1:L, :], m_sc.shape[1:])
    return hh


def _mlstm_out(hh, xc, gm_ref, ng_ref, sk_ref, y_ref):
    heads = range(len(hh))
    dh = hh[0].shape[1]
    mu = [jnp.mean(hh[h], axis=1, keepdims=True) for h in heads]
    cen = [hh[h] - mu[h] for h in heads]
    var = [jnp.mean(cen[h] * cen[h], axis=1, keepdims=True) for h in heads]
    for h in heads:
        sl = slice(h * dh, (h + 1) * dh)
        hn = cen[h] * lax.rsqrt(var[h] + EPS) * ng_ref[:, sl] + sk_ref[:, sl] * xc[:, sl]
        y_ref[:, sl] = (hn * gm_ref[:, sl].astype(F32)).astype(y_ref.dtype)


def _mlstm(xm, gm, conv_w, conv_b, wq, wk, wv, wg, gb, ngain, skip):
    b, s, d = xm.shape
    L = MLSTM_CHUNK
    assert s % L == 0
    n_heads, dh, _ = wq.shape
    n_seqs = MLSTM_SEQS
    assert b % n_seqs == 0
    seq_spec = pl.BlockSpec((n_seqs, L, d), lambda bi, ci: (bi, ci, 0))

    def const(shape):
        nd = len(shape)
        return pl.BlockSpec(shape, lambda bi, ci: (0,) * nd)

    kern = functools.partial(_mlstm_kernel, k_scale=dh ** -0.5)
    return pl.pallas_call(
        kern,
        grid=(b // n_seqs, s // L),
        in_specs=[seq_spec, seq_spec, const(conv_w.shape), const(conv_b.shape), const(wq.shape),
                  const(wk.shape), const(wv.shape), const(wg.shape), const(gb.shape),
                  const(ngain.shape), const(skip.shape)],
        out_specs=seq_spec,
        out_shape=jax.ShapeDtypeStruct((b, s, d), BF16),
        scratch_shapes=[
            pltpu.VMEM((n_seqs, SUBLANES, d), F32),
            pltpu.VMEM((n_seqs, n_heads, dh, dh), F32),
            pltpu.VMEM((n_seqs, n_heads, 1, dh), F32),
            pltpu.VMEM((n_seqs, n_heads, 1, LANES), F32),
            pltpu.VMEM((L, L), BF16),
            pltpu.VMEM((CONV_WIDTH - 1, L, L), BF16),
            pltpu.VMEM((2, d, wg.shape[1]), BF16),
        ],
        compiler_params=pltpu.CompilerParams(
            dimension_semantics=("arbitrary", "arbitrary"), vmem_limit_bytes=VMEM_LIMIT),
        name="mlstm",
    )(xm, gm, conv_w, conv_b, wq, wk, wv, wg, gb, ngain, skip)


def _fox_kernel(q_ref, k_ref, v_ref, cr_ref, gf_ref, g_ref, o_ref):
    s_len, dh = q_ref.shape
    tq = FOX_Q
    causal = _tril_mask(tq)
    gain = g_ref[...]
    c_keys = cr_ref[pl.ds(pl.program_id(1), 1), :]

    def scores(qi):
        r0 = qi * tq
        return lax.dot_general(q_ref[r0:r0 + tq, :], k_ref[0:r0 + tq, :], (((1,), (1,)), ((), ())),
                               preferred_element_type=F32)

    order = list(reversed(range(s_len // tq)))
    ahead = [scores(qi) for qi in order[:FOX_LOOKAHEAD]]
    for pos, qi in enumerate(order):
        r0 = qi * tq
        kv = r0 + tq
        s = ahead.pop(0)
        if pos + FOX_LOOKAHEAD < len(order):
            ahead.append(scores(order[pos + FOX_LOOKAHEAD]))
        s = s - c_keys[:, 0:kv]
        s_diag = jnp.where(causal, s[:, r0:], NEG_INF)
        s = s_diag if qi == 0 else jnp.concatenate([s[:, :r0], s_diag], axis=1)
        m = jnp.max(s, axis=1, keepdims=True)
        p = jnp.exp2(s - m).astype(BF16)
        v_ext = jnp.concatenate([v_ref[0:kv, :], jnp.ones((kv, dh), BF16)], axis=1)
        acc = jnp.dot(p, v_ext, preferred_element_type=F32)

        out = acc[:, :dh] / acc[:, dh:]
        ms = jnp.mean(out * out, axis=1, keepdims=True)
        o_ref[r0:kv, :] = (out * lax.rsqrt(ms + EPS) * gain
                           * gf_ref[r0:kv, :].astype(F32)).astype(o_ref.dtype)


def _fox(q, k, v, c_row, gf, gain):
    b, s, d = q.shape
    n_heads = N_FOX_HEADS
    dh = d // n_heads
    assert dh == LANES and s % FOX_Q == 0
    head_spec = pl.BlockSpec((None, s, dh), lambda bi, hi: (bi, 0, hi))
    return pl.pallas_call(
        _fox_kernel,
        grid=(b, n_heads),
        in_specs=[head_spec, head_spec, head_spec,
                  pl.BlockSpec((c_row.shape[0], s), lambda bi, hi: (0, bi)),
                  head_spec,
                  pl.BlockSpec((1, dh), lambda bi, hi: (0, hi))],
        out_specs=head_spec,
        out_shape=jax.ShapeDtypeStruct((b, s, d), BF16),
        compiler_params=pltpu.CompilerParams(
            dimension_semantics=("arbitrary", "arbitrary"), vmem_limit_bytes=VMEM_LIMIT),
        name="fox",
    )(q, k, v, c_row, gf, gain)


def _out_proj_kernel(ym_ref, yf_ref, x_ref, w_ref, g_ref, o_ref, wb_ref, *, final_norm):
    d_m = ym_ref.shape[1]

    @pl.when(pl.program_id(0) == 0)
    def _():
        wb_ref[...] = w_ref[...].astype(BF16)

    rows = x_ref.shape[0]
    for r0 in range(0, rows, OUT_PROJ_SUB):
        sl = slice(r0, r0 + OUT_PROJ_SUB)
        y = jnp.dot(ym_ref[sl, :], wb_ref[0:d_m, :], preferred_element_type=F32)
        y = y + jnp.dot(yf_ref[sl, :], wb_ref[d_m:, :], preferred_element_type=F32)
        r = x_ref[sl, :] + y
        if final_norm:
            ms = jnp.mean(r * r, axis=-1, keepdims=True)
            r = r * lax.rsqrt(ms + EPS) * g_ref[...]
        o_ref[sl, :] = r


def _out_proj(ym, yf, x2, w_out, gain, final_norm):
    t_rows, d = x2.shape
    rows = OUT_PROJ_ROWS
    assert t_rows % rows == 0
    row = lambda width: pl.BlockSpec((rows, width), lambda t: (t, 0))
    const = lambda shape: pl.BlockSpec(shape, lambda t: (0, 0), pipeline_mode=pl.Buffered(1))
    kern = functools.partial(_out_proj_kernel, final_norm=final_norm)
    return pl.pallas_call(
        kern,
        grid=(t_rows // rows,),
        in_specs=[row(ym.shape[1]), row(yf.shape[1]), row(d), const(w_out.shape), const((1, d))],
        out_specs=row(d),
        out_shape=jax.ShapeDtypeStruct((t_rows, d), F32),
        scratch_shapes=[pltpu.VMEM(w_out.shape, BF16)],
        compiler_params=pltpu.CompilerParams(
            dimension_semantics=("arbitrary",), vmem_limit_bytes=VMEM_LIMIT),
        name="out_proj",
    )(ym, yf, x2, w_out, gain)


def _block_diag(w, dh):
    nb, bs, _ = w.shape
    d = nb * bs
    tiled = jnp.broadcast_to(w.reshape(d, 1, bs), (d, dh // bs, bs)).reshape(d, dh)
    r = lax.broadcasted_iota(jnp.int32, (d, dh), 0)
    c = lax.broadcasted_iota(jnp.int32, (d, dh), 1)
    keep = (r % dh) // bs == c // bs
    return jnp.where(keep, tiled, 0.0).reshape(d // dh, dh, dh)


def kernel(x, norm_gain, w_in, conv_w, conv_b, w_q_m, w_k_m, w_v_m, w_igate, b_igate, w_fgate,
           b_fgate, mlstm_norm_gain, mlstm_skip, fox_forget_bias, fox_norm_gain, w_out,
           final_norm_gain):
    b, s, d = x.shape
    depth = norm_gain.shape[0]
    d_m = conv_w.shape[-1]
    d_f = fox_norm_gain.shape[-1]
    dh_m = d_m // N_MLSTM_HEADS
    dh_f = d_f // N_FOX_HEADS
    n_main = 2 * d_m + 4 * d_f
    assert d_m == d and d_f == d

    x2 = x.reshape(b * s, d)
    for l in range(depth):
        assert w_in.shape[2] == n_main + N_FOX_HEADS
        f_bias = jnp.pad(fox_forget_bias[l], (0, LANES - N_FOX_HEADS)).reshape(1, LANES)
        xm, gm, qf, kf, vf, gf, c_keys = _in_proj(
            x2, norm_gain[l].reshape(1, d), w_in[l].T, f_bias, s, dh_f ** -0.5 * LOG2E)

        lane_pad = ((0, 0), (0, LANES - N_MLSTM_HEADS))
        wg = jnp.concatenate([jnp.pad(w_igate[l], lane_pad), jnp.pad(w_fgate[l], lane_pad)], axis=1)
        gb = jnp.concatenate([jnp.pad(b_igate[l][None], lane_pad),
                              jnp.pad(b_fgate[l][None], lane_pad)], axis=1)
        y_m = _mlstm(xm.reshape(b, s, d_m), gm.reshape(b, s, d_m), conv_w[l],
                     conv_b[l].reshape(1, d_m),
                     _block_diag(w_q_m[l], dh_m).astype(BF16),
                     _block_diag(w_k_m[l], dh_m).astype(BF16),
                     _block_diag(w_v_m[l], dh_m).astype(BF16),
                     wg.astype(BF16), gb,
                     mlstm_norm_gain[l].reshape(1, d_m), mlstm_skip[l].reshape(1, d_m))

        y_f = _fox(qf.reshape(b, s, d_f), kf.reshape(b, s, d_f), vf.reshape(b, s, d_f), c_keys,
                   gf.reshape(b, s, d_f), fox_norm_gain[l].reshape(1, d_f))

        x2 = _out_proj(y_m.reshape(b * s, d_m), y_f.reshape(b * s, d_f), x2, w_out[l],
                       final_norm_gain.reshape(1, d), final_norm=(l == depth - 1))
    return x2.reshape(b, s, d)
```

```python
import functools
import math

import jax
import jax.numpy as jnp
from jax import lax
from jax.experimental import pallas as pl
from jax.experimental.pallas import tpu as pltpu

EPS = 1e-6
N_MLSTM_HEADS = 4
N_FOX_HEADS = 8
QKV_BLOCK = 4
CONV_WIDTH = 4

LANES = 128
SUBLANES = 8
VMEM_LIMIT = 56 * 1024 * 1024

IN_PROJ_BLOCKS = (256, 256)
OUT_PROJ_ROWS = 1024
OUT_PROJ_SUB = 256
MLSTM_CHUNK = 256
MLSTM_SEQS = 2
FOX_HEADS = 2
FOX_LOOKAHEAD = 2
FOX_Q = 256
LOG2E = math.log2(math.e)

BF16 = jnp.bfloat16
F32 = jnp.float32
NEG_INF = float("-inf")


def _log_sigmoid(z):
    return jnp.minimum(z, 0.0) - jnp.log(1.0 + jnp.exp(-jnp.abs(z)))


def _sigmoid(z):
    return 1.0 / (1.0 + jnp.exp(-z))


def _silu(z):
    return z * _sigmoid(z)


def _split3(v):
    hi = v.astype(BF16)
    r = v - hi.astype(F32)
    mid = r.astype(BF16)
    lo = (r - mid.astype(F32)).astype(BF16)
    return hi, mid, lo


def _cumsum_rows(tri, v):
    w = v.shape[1]
    cat = jnp.concatenate(_split3(v), axis=1)
    cs = jnp.dot(tri, cat, preferred_element_type=F32)
    return cs[:, :w] + cs[:, w:2 * w] + cs[:, 2 * w:]


def _tril_mask(n):
    row = lax.broadcasted_iota(jnp.int32, (n, n), 0)
    col = lax.broadcasted_iota(jnp.int32, (n, n), 1)
    return col <= row


def _in_proj_kernel(x_ref, g_ref, w_hbm, fb_ref,
                    xm_ref, gm_ref, q_ref, k_ref, v_ref, gf_ref, c_ref,
                    wb_ref, wfb_ref, tri_ref, carry_ref, stage_ref, gstage_ref, sems,
                    *, tiles_per_seq, q_scale):
    t = pl.program_id(0)
    rows, d = x_ref.shape
    sub = tri_ref.shape[0]
    n_main = wb_ref.shape[1]
    n_gate = gstage_ref.shape[0]
    n_chunks = n_main // d

    def chunk_copy(c):
        return pltpu.make_async_copy(w_hbm.at[pl.ds(c * d, d), :], stage_ref.at[c % 2], sems.at[c % 2])

    def gate_copy():
        return pltpu.make_async_copy(w_hbm.at[pl.ds(n_main, n_gate), :], gstage_ref, sems.at[2])

    @pl.when(t == 0)
    def _():
        gate_copy().start()
        chunk_copy(0).start()
        tri_ref[...] = jnp.where(_tril_mask(sub), 1.0, 0.0).astype(BF16)
        for c in range(n_chunks):
            if c + 1 < n_chunks:
                chunk_copy(c + 1).start()
            chunk_copy(c).wait()
            wb_ref[:, c * d:(c + 1) * d] = stage_ref[c % 2].T.astype(BF16)
        gate_copy().wait()
        w_gate = jnp.concatenate([gstage_ref[...], jnp.zeros((LANES - n_gate, d), F32)], axis=0)
        wfb_ref[...] = w_gate.T.astype(BF16)

    @pl.when(t % tiles_per_seq == 0)
    def _():
        carry_ref[...] = jnp.zeros_like(carry_ref)

    starts = [sum(IN_PROJ_BLOCKS[:i]) for i in range(len(IN_PROJ_BLOCKS))]
    blocks = [slice(r0, r0 + n) for r0, n in zip(starts, IN_PROJ_BLOCKS)]
    hns = []
    for sl in blocks:
        x = x_ref[sl, :]
        ms = jnp.mean(x * x, axis=-1, keepdims=True)
        hns.append((x * lax.rsqrt(ms + EPS) * g_ref[...]).astype(BF16))

    def proj(hn, idx):
        return jnp.dot(hn, wb_ref[:, idx * d:(idx + 1) * d], preferred_element_type=F32)

    for sl, hn in zip(blocks, hns):
        xm_ref[sl, :] = proj(hn, 0).astype(xm_ref.dtype)
    log_f = [_log_sigmoid(jnp.dot(hn, wfb_ref[...], preferred_element_type=F32) + fb_ref[...])
             for hn in hns]
    for sl, hn in zip(blocks, hns):
        gm_ref[sl, :] = _silu(proj(hn, 1)).astype(gm_ref.dtype)
    for sl, hn in zip(blocks, hns):
        gf_ref[sl, :] = _silu(proj(hn, 5)).astype(gf_ref.dtype)
    carry = carry_ref[...]
    for sl, lf in zip(blocks, log_f):
        n = lf.shape[0]
        c = _cumsum_rows(tri_ref[0:n, 0:n], lf) + carry
        c_ref[:, sl] = (c * LOG2E).T[0:c_ref.shape[0], :]
        carry = c[n - 1:n, :]
    carry_ref[...] = carry
    for sl, hn in zip(blocks, hns):
        q_ref[sl, :] = (proj(hn, 2) * q_scale).astype(q_ref.dtype)
    for sl, hn in zip(blocks, hns):
        k_ref[sl, :] = proj(hn, 3).astype(k_ref.dtype)
    for sl, hn in zip(blocks, hns):
        v_ref[sl, :] = proj(hn, 4).astype(v_ref.dtype)


def _in_proj(x2, gain, w_in_t, f_bias, seq_len, q_scale):
    t_rows, d = x2.shape
    rows, sub = sum(IN_PROJ_BLOCKS), max(IN_PROJ_BLOCKS)
    assert t_rows % rows == 0 and seq_len % rows == 0
    n_main = 6 * d
    n_gate = w_in_t.shape[0] - n_main
    assert 0 < n_gate <= SUBLANES and w_in_t.shape[1] == d
    row_spec = pl.BlockSpec((rows, d), lambda t: (t, 0))
    const = lambda shape: pl.BlockSpec(shape, lambda t: (0, 0), pipeline_mode=pl.Buffered(1))
    kern = functools.partial(_in_proj_kernel, tiles_per_seq=seq_len // rows, q_scale=q_scale)
    return pl.pallas_call(
        kern,
        grid=(t_rows // rows,),
        in_specs=[row_spec, const((1, d)), pl.BlockSpec(memory_space=pl.ANY), const((1, LANES))],
        out_specs=[row_spec] * 6 + [pl.BlockSpec((SUBLANES, rows), lambda t: (0, t))],
        out_shape=[jax.ShapeDtypeStruct((t_rows, d), BF16)] * 6
        + [jax.ShapeDtypeStruct((SUBLANES, t_rows), F32)],
        scratch_shapes=[pltpu.VMEM((d, n_main), BF16), pltpu.VMEM((d, LANES), BF16),
                        pltpu.VMEM((sub, sub), BF16), pltpu.VMEM((1, LANES), F32),
                        pltpu.VMEM((2, d, d), F32), pltpu.VMEM((n_gate, d), F32),
                        pltpu.SemaphoreType.DMA((3,))],
        compiler_params=pltpu.CompilerParams(
            dimension_semantics=("arbitrary",), vmem_limit_bytes=VMEM_LIMIT),
        name="in_proj",
    )(x2, gain, w_in_t, f_bias)


def _mlstm_kernel(xm_ref, gm_ref, cw_ref, cb_ref, wq_ref, wk_ref, wv_ref, wg_ref, gb_ref,
                  ng_ref, sk_ref, y_ref,
                  tail_ref, c_sc, n_sc, m_sc, tri_ref, shift_ref, gw_ref, *, k_scale):
    ci = pl.program_id(1)
    n_seqs, L, d_m = xm_ref.shape
    n_heads, dh, _ = wq_ref.shape

    @pl.when(ci == 0)
    def _():
        for h in range(n_heads):
            sl = slice(h * dh, (h + 1) * dh)
            g_q = jnp.dot(wq_ref[h], wg_ref[h * dh:(h + 1) * dh, :], preferred_element_type=F32)
            g_k = jnp.dot(wk_ref[h], wg_ref[d_m + h * dh:d_m + (h + 1) * dh, :],
                          preferred_element_type=F32)
            g_v = jnp.dot(wv_ref[h], wg_ref[2 * d_m + h * dh:2 * d_m + (h + 1) * dh, :],
                          preferred_element_type=F32)
            gw_ref[0, sl, :] = (g_q + g_k * k_scale).astype(BF16)
            gw_ref[1, sl, :] = g_v.astype(BF16)
        c_sc[...] = jnp.zeros_like(c_sc)
        n_sc[...] = jnp.zeros_like(n_sc)
        m_sc[...] = jnp.zeros_like(m_sc)
        tail_ref[...] = jnp.zeros_like(tail_ref)
        row = lax.broadcasted_iota(jnp.int32, (L, L), 0)
        col = lax.broadcasted_iota(jnp.int32, (L, L), 1)
        tri_ref[...] = jnp.where(col <= row, 1.0, 0.0).astype(BF16)
        for j in range(1, CONV_WIDTH):
            shift_ref[j - 1] = jnp.where(col == row - j, 1.0, 0.0).astype(BF16)

    seqs = range(n_seqs)
    fronts = [_mlstm_front(xm_ref.at[i], cw_ref, cb_ref, wq_ref, wk_ref, wv_ref, gw_ref, gb_ref,
                           tail_ref.at[i], tri_ref, shift_ref, k_scale) for i in seqs]
    for i in seqs:
        mixed = _mlstm_mix(*fronts[i][1:], c_sc.at[i], n_sc.at[i], m_sc.at[i])
        _mlstm_out(mixed, fronts[i][0], gm_ref.at[i], ng_ref, sk_ref, y_ref.at[i])


def _mlstm_front(xm_ref, cw_ref, cb_ref, wq_ref, wk_ref, wv_ref, gw_ref, gb_ref,
                 tail_ref, tri_ref, shift_ref, k_scale):
    L, d_m = xm_ref.shape
    dh = wq_ref.shape[1]
    n_heads = wq_ref.shape[0]
    halo = SUBLANES

    x_b = xm_ref[...]
    x_f = x_b.astype(F32)
    conv = cb_ref[...] + x_f * cw_ref[CONV_WIDTH - 1:CONV_WIDTH, :]
    edge = jnp.concatenate([tail_ref[...], jnp.zeros((halo, d_m), F32)], axis=0)
    head = jnp.zeros((halo, d_m), F32)
    for j in range(1, CONV_WIDTH):
        w_j = cw_ref[CONV_WIDTH - 1 - j:CONV_WIDTH - j, :]
        conv = conv + jnp.dot(shift_ref[j - 1], x_b, preferred_element_type=F32) * w_j
        head = head + edge[halo - j:2 * halo - j, :] * w_j
    conv = jnp.concatenate([conv[:halo] + head, conv[halo:]], axis=0)
    tail_ref[...] = x_f[L - halo:L, :]
    xc = _silu(conv)
    xc_b = xc.astype(BF16)

    gates = (jnp.dot(xc_b, gw_ref[0], preferred_element_type=F32)
             + jnp.dot(x_b, gw_ref[1], preferred_element_type=F32) + gb_ref[...])
    qs, ks, vs = [], [], []
    for h in range(n_heads):
        sl = slice(h * dh, (h + 1) * dh)
        q = jnp.dot(xc_b[:, sl], wq_ref[h], preferred_element_type=F32)
        k = jnp.dot(xc_b[:, sl], wk_ref[h], preferred_element_type=F32) * k_scale
        v = jnp.dot(x_b[:, sl], wv_ref[h], preferred_element_type=F32)
        qs.append(q.astype(BF16))
        ks.append((k, k.astype(BF16)))
        vs.append(v.astype(BF16))

    ig = gates[:, :LANES]
    bcum = _cumsum_rows(tri_ref[...], _log_sigmoid(gates[:, LANES:]))
    a_col = ig - bcum
    return xc, qs, ks, vs, a_col, bcum


def _mlstm_mix(qs, ks, vs, a_col, bcum, c_sc, n_sc, m_sc):
    n_heads = len(qs)
    L, dh = qs[0].shape
    a_rows = a_col.T
    causal = _tril_mask(L)
    ones_blk = jnp.ones((L, LANES), BF16)

    heads = range(n_heads)
    nt = (((1,), (1,)), ((), ()))
    m_prev = {h: m_sc[h][:, 0:1] for h in heads}
    dm = {h: jnp.where(causal, a_rows[h:h + 1, :], NEG_INF) for h in heads}
    m_run = {h: jnp.maximum(m_prev[h], jnp.max(dm[h], axis=1, keepdims=True)) for h in heads}
    w_inter = {h: jnp.exp(m_prev[h] - m_run[h]) for h in heads}
    s_b = {h: (lax.dot_general(qs[h], ks[h][1], nt, preferred_element_type=F32)
               * jnp.exp(dm[h] - m_run[h])).astype(BF16) for h in heads}
    num = {h: w_inter[h] * jnp.dot(qs[h], c_sc[h].astype(BF16), preferred_element_type=F32)
           + jnp.dot(s_b[h], vs[h], preferred_element_type=F32) for h in heads}
    qn = {h: lax.dot_general(qs[h], jnp.broadcast_to(n_sc[h], (LANES, dh)).astype(BF16), nt,
                             preferred_element_type=F32) for h in heads}
    den = {h: w_inter[h] * qn[h] + jnp.dot(s_b[h], ones_blk, preferred_element_type=F32)
           for h in heads}
    m_t = {h: bcum[:, h:h + 1] + m_run[h] for h in heads}
    inv = {h: 1.0 / jnp.maximum(jnp.abs(den[h]), jnp.exp(-m_t[h])) for h in heads}
    hh = [num[h] * jnp.concatenate([inv[h]] * (dh // LANES), axis=1) for h in heads]

    for h in heads:
        m_last = m_run[h][L - 1:L, :]
        decay = jnp.exp(m_prev[h] - m_last)
        kw = ks[h][0] * jnp.exp(a_col[:, h:h + 1] - m_last)
        c_sc[h] = decay * c_sc[h] + lax.dot_general(
            kw.astype(BF16), vs[h], (((0,), (0,)), ((), ())), preferred_element_type=F32)
        n_sc[h] = decay * n_sc[h] + jnp.sum(kw, axis=0, keepdims=True)
        m_sc[h] = jnp.broadcast_to(m_t[h][L - 1:L, :], m_sc.shape[1:])
    return hh


def _mlstm_out(hh, xc, gm_ref, ng_ref, sk_ref, y_ref):
    heads = range(len(hh))
    dh = hh[0].shape[1]
    mu = [jnp.mean(hh[h], axis=1, keepdims=True) for h in heads]
    cen = [hh[h] - mu[h] for h in heads]
    var = [jnp.mean(cen[h] * cen[h], axis=1, keepdims=True) for h in heads]
    for h in heads:
        sl = slice(h * dh, (h + 1) * dh)
        hn = cen[h] * lax.rsqrt(var[h] + EPS) * ng_ref[:, sl] + sk_ref[:, sl] * xc[:, sl]
        y_ref[:, sl] = (hn * gm_ref[:, sl].astype(F32)).astype(y_ref.dtype)


def _mlstm(xm, gm, conv_w, conv_b, wq, wk, wv, wg, gb, ngain, skip):
    b, s, d = xm.shape
    L = MLSTM_CHUNK
    assert s % L == 0
    n_heads, dh, _ = wq.shape
    n_seqs = MLSTM_SEQS
    assert b % n_seqs == 0
    seq_spec = pl.BlockSpec((n_seqs, L, d), lambda bi, ci: (bi, ci, 0))

    def const(shape):
        nd = len(shape)
        return pl.BlockSpec(shape, lambda bi, ci: (0,) * nd)

    kern = functools.partial(_mlstm_kernel, k_scale=dh ** -0.5)
    return pl.pallas_call(
        kern,
        grid=(b // n_seqs, s // L),
        in_specs=[seq_spec, seq_spec, const(conv_w.shape), const(conv_b.shape), const(wq.shape),
                  const(wk.shape), const(wv.shape), const(wg.shape), const(gb.shape),
                  const(ngain.shape), const(skip.shape)],
        out_specs=seq_spec,
        out_shape=jax.ShapeDtypeStruct((b, s, d), BF16),
        scratch_shapes=[
            pltpu.VMEM((n_seqs, SUBLANES, d), F32),
            pltpu.VMEM((n_seqs, n_heads, dh, dh), F32),
            pltpu.VMEM((n_seqs, n_heads, 1, dh), F32),
            pltpu.VMEM((n_seqs, n_heads, 1, LANES), F32),
            pltpu.VMEM((L, L), BF16),
            pltpu.VMEM((CONV_WIDTH - 1, L, L), BF16),
            pltpu.VMEM((2, d, wg.shape[1]), BF16),
        ],
        compiler_params=pltpu.CompilerParams(
            dimension_semantics=("arbitrary", "arbitrary"), vmem_limit_bytes=VMEM_LIMIT),
        name="mlstm",
    )(xm, gm, conv_w, conv_b, wq, wk, wv, wg, gb, ngain, skip)


def _fox_kernel(q_ref, k_ref, v_ref, cr_ref, gf_ref, g_ref, o_ref):
    s_len = q_ref.shape[0]
    dh = LANES
    n_local = q_ref.shape[1] // dh
    tq = FOX_Q
    causal = _tril_mask(tq)
    head0 = pl.program_id(1) * n_local

    def scores(unit):
        hl, qi = unit
        r0 = qi * tq
        cols = slice(hl * dh, (hl + 1) * dh)
        return lax.dot_general(q_ref[r0:r0 + tq, cols], k_ref[0:r0 + tq, cols],
                               (((1,), (1,)), ((), ())), preferred_element_type=F32)

    order = [(hl, qi) for hl in range(n_local) for qi in reversed(range(s_len // tq))]
    ahead = [scores(u) for u in order[:FOX_LOOKAHEAD]]
    for pos, (hl, qi) in enumerate(order):
        r0 = qi * tq
        kv = r0 + tq
        cols = slice(hl * dh, (hl + 1) * dh)
        s = ahead.pop(0)
        if pos + FOX_LOOKAHEAD < len(order):
            ahead.append(scores(order[pos + FOX_LOOKAHEAD]))
        s = s - cr_ref[pl.ds(head0 + hl, 1), 0:kv]
        s_diag = jnp.where(causal, s[:, r0:], NEG_INF)
        s = s_diag if qi == 0 else jnp.concatenate([s[:, :r0], s_diag], axis=1)
        m = jnp.max(s, axis=1, keepdims=True)
        p = jnp.exp2(s - m).astype(BF16)
        v_ext = jnp.concatenate([v_ref[0:kv, cols], jnp.ones((kv, dh), BF16)], axis=1)
        acc = jnp.dot(p, v_ext, preferred_element_type=F32)

        out = acc[:, :dh] / acc[:, dh:]
        ms = jnp.mean(out * out, axis=1, keepdims=True)
        o_ref[r0:kv, cols] = (out * lax.rsqrt(ms + EPS) * g_ref[:, cols]
                              * gf_ref[r0:kv, cols].astype(F32)).astype(o_ref.dtype)


def _fox(q, k, v, c_row, gf, gain):
    b, s, d = q.shape
    n_heads = N_FOX_HEADS
    dh = d // n_heads
    assert dh == LANES and s % FOX_Q == 0 and n_heads % FOX_HEADS == 0
    width = FOX_HEADS * dh
    head_spec = pl.BlockSpec((None, s, width), lambda bi, hi: (bi, 0, hi))
    return pl.pallas_call(
        _fox_kernel,
        grid=(b, n_heads // FOX_HEADS),
        in_specs=[head_spec, head_spec, head_spec,
                  pl.BlockSpec((c_row.shape[0], s), lambda bi, hi: (0, bi)),
                  head_spec,
                  pl.BlockSpec((1, width), lambda bi, hi: (0, hi))],
        out_specs=head_spec,
        out_shape=jax.ShapeDtypeStruct((b, s, d), BF16),
        compiler_params=pltpu.CompilerParams(
            dimension_semantics=("arbitrary", "arbitrary"), vmem_limit_bytes=VMEM_LIMIT),
        name="fox",
    )(q, k, v, c_row, gf, gain)


def _out_proj_kernel(ym_ref, yf_ref, x_ref, w_ref, g_ref, o_ref, wb_ref, *, final_norm):
    d_m = ym_ref.shape[1]

    @pl.when(pl.program_id(0) == 0)
    def _():
        wb_ref[...] = w_ref[...].astype(BF16)

    rows = x_ref.shape[0]
    for r0 in range(0, rows, OUT_PROJ_SUB):
        sl = slice(r0, r0 + OUT_PROJ_SUB)
        y = jnp.dot(ym_ref[sl, :], wb_ref[0:d_m, :], preferred_element_type=F32)
        y = y + jnp.dot(yf_ref[sl, :], wb_ref[d_m:, :], preferred_element_type=F32)
        r = x_ref[sl, :] + y
        if final_norm:
            ms = jnp.mean(r * r, axis=-1, keepdims=True)
            r = r * lax.rsqrt(ms + EPS) * g_ref[...]
        o_ref[sl, :] = r


def _out_proj(ym, yf, x2, w_out, gain, final_norm):
    t_rows, d = x2.shape
    rows = OUT_PROJ_ROWS
    assert t_rows % rows == 0
    row = lambda width: pl.BlockSpec((rows, width), lambda t: (t, 0))
    const = lambda shape: pl.BlockSpec(shape, lambda t: (0, 0), pipeline_mode=pl.Buffered(1))
    kern = functools.partial(_out_proj_kernel, final_norm=final_norm)
    return pl.pallas_call(
        kern,
        grid=(t_rows // rows,),
        in_specs=[row(ym.shape[1]), row(yf.shape[1]), row(d), const(w_out.shape), const((1, d))],
        out_specs=row(d),
        out_shape=jax.ShapeDtypeStruct((t_rows, d), F32),
        scratch_shapes=[pltpu.VMEM(w_out.shape, BF16)],
        compiler_params=pltpu.CompilerParams(
            dimension_semantics=("arbitrary",), vmem_limit_bytes=VMEM_LIMIT),
        name="out_proj",
    )(ym, yf, x2, w_out, gain)


def _block_diag(w, dh):
    nb, bs, _ = w.shape
    d = nb * bs
    tiled = jnp.broadcast_to(w.reshape(d, 1, bs), (d, dh // bs, bs)).reshape(d, dh)
    r = lax.broadcasted_iota(jnp.int32, (d, dh), 0)
    c = lax.broadcasted_iota(jnp.int32, (d, dh), 1)
    keep = (r % dh) // bs == c // bs
    return jnp.where(keep, tiled, 0.0).reshape(d // dh, dh, dh)


def kernel(x, norm_gain, w_in, conv_w, conv_b, w_q_m, w_k_m, w_v_m, w_igate, b_igate, w_fgate,
           b_fgate, mlstm_norm_gain, mlstm_skip, fox_forget_bias, fox_norm_gain, w_out,
           final_norm_gain):
    b, s, d = x.shape
    depth = norm_gain.shape[0]
    d_m = conv_w.shape[-1]
    d_f = fox_norm_gain.shape[-1]
    dh_m = d_m // N_MLSTM_HEADS
    dh_f = d_f // N_FOX_HEADS
    n_main = 2 * d_m + 4 * d_f
    assert d_m == d and d_f == d

    x2 = x.reshape(b * s, d)
    for l in range(depth):
        assert w_in.shape[2] == n_main + N_FOX_HEADS
        f_bias = jnp.pad(fox_forget_bias[l], (0, LANES - N_FOX_HEADS)).reshape(1, LANES)
        xm, gm, qf, kf, vf, gf, c_keys = _in_proj(
            x2, norm_gain[l].reshape(1, d), w_in[l].T, f_bias, s, dh_f ** -0.5 * LOG2E)

        lane_pad = ((0, 0), (0, LANES - N_MLSTM_HEADS))
        wg = jnp.concatenate([jnp.pad(w_igate[l], lane_pad), jnp.pad(w_fgate[l], lane_pad)], axis=1)
        gb = jnp.concatenate([jnp.pad(b_igate[l][None], lane_pad),
                              jnp.pad(b_fgate[l][None], lane_pad)], axis=1)
        y_m = _mlstm(xm.reshape(b, s, d_m), gm.reshape(b, s, d_m), conv_w[l],
                     conv_b[l].reshape(1, d_m),
                     _block_diag(w_q_m[l], dh_m).astype(BF16),
                     _block_diag(w_k_m[l], dh_m).astype(BF16),
                     _block_diag(w_v_m[l], dh_m).astype(BF16),
                     wg.astype(BF16), gb,
                     mlstm_norm_gain[l].reshape(1, d_m), mlstm_skip[l].reshape(1, d_m))

        y_f = _fox(qf.reshape(b, s, d_f), kf.reshape(b, s, d_f), vf.reshape(b, s, d_f), c_keys,
                   gf.reshape(b, s, d_f), fox_norm_gain[l].reshape(1, d_f))

        x2 = _out_proj(y_m.reshape(b * s, d_m), y_f.reshape(b * s, d_f), x2, w_out[l],
                       final_norm_gain.reshape(1, d), final_norm=(l == depth - 1))
    return x2.reshape(b, s, d)
```

```python
import functools
import math

import jax
import jax.numpy as jnp
from jax import lax
from jax.experimental import pallas as pl
from jax.experimental.pallas import tpu as pltpu

EPS = 1e-6
N_MLSTM_HEADS = 4
N_FOX_HEADS = 8
QKV_BLOCK = 4
CONV_WIDTH = 4

LANES = 128
SUBLANES = 8
VMEM_LIMIT = 56 * 1024 * 1024

IN_PROJ_BLOCKS = (256, 256, 256, 256)
IN_PROJ_STAGE_ROWS = 512
OUT_PROJ_ROWS = 1024
OUT_PROJ_SUB = 256
MLSTM_CHUNK = 256
MLSTM_SEQS = 2
FOX_HEADS = 2
FOX_LOOKAHEAD = 2
FOX_Q = 256
LOG2E = math.log2(math.e)

BF16 = jnp.bfloat16
F32 = jnp.float32
NEG_INF = float("-inf")


def _log_sigmoid(z):
    return jnp.minimum(z, 0.0) - jnp.log(1.0 + jnp.exp(-jnp.abs(z)))


def _sigmoid(z):
    return 1.0 / (1.0 + jnp.exp(-z))


def _silu(z):
    return z * _sigmoid(z)


def _split3(v):
    hi = v.astype(BF16)
    r = v - hi.astype(F32)
    mid = r.astype(BF16)
    lo = (r - mid.astype(F32)).astype(BF16)
    return hi, mid, lo


def _cumsum_rows(tri, v):
    w = v.shape[1]
    cat = jnp.concatenate(_split3(v), axis=1)
    cs = jnp.dot(tri, cat, preferred_element_type=F32)
    return cs[:, :w] + cs[:, w:2 * w] + cs[:, 2 * w:]


def _tril_mask(n):
    row = lax.broadcasted_iota(jnp.int32, (n, n), 0)
    col = lax.broadcasted_iota(jnp.int32, (n, n), 1)
    return col <= row


def _in_proj_kernel(x_ref, g_ref, w_hbm, fb_ref,
                    xm_ref, gm_ref, q_ref, k_ref, v_ref, gf_ref, c_ref,
                    wb_ref, wfb_ref, tri_ref, carry_ref, stage_ref, gstage_ref, sems,
                    *, tiles_per_seq, q_scale):
    t = pl.program_id(0)
    rows, d = x_ref.shape
    sub = tri_ref.shape[0]
    n_main = wb_ref.shape[1]
    n_gate = gstage_ref.shape[0]
    stage_rows = stage_ref.shape[1]
    n_chunks = n_main // stage_rows

    def chunk_copy(c):
        return pltpu.make_async_copy(w_hbm.at[pl.ds(c * stage_rows, stage_rows), :], stage_ref.at[c % 2],
                                     sems.at[c % 2])

    def gate_copy():
        return pltpu.make_async_copy(w_hbm.at[pl.ds(n_main, n_gate), :], gstage_ref, sems.at[2])

    @pl.when(t == 0)
    def _():
        gate_copy().start()
        chunk_copy(0).start()
        tri_ref[...] = jnp.where(_tril_mask(sub), 1.0, 0.0).astype(BF16)
        for c in range(n_chunks):
            if c + 1 < n_chunks:
                chunk_copy(c + 1).start()
            chunk_copy(c).wait()
            wb_ref[:, c * stage_rows:(c + 1) * stage_rows] = stage_ref[c % 2].T.astype(BF16)
        gate_copy().wait()
        w_gate = jnp.concatenate([gstage_ref[...], jnp.zeros((LANES - n_gate, d), F32)], axis=0)
        wfb_ref[...] = w_gate.T.astype(BF16)

    @pl.when(t % tiles_per_seq == 0)
    def _():
        carry_ref[...] = jnp.zeros_like(carry_ref)

    starts = [sum(IN_PROJ_BLOCKS[:i]) for i in range(len(IN_PROJ_BLOCKS))]
    blocks = [slice(r0, r0 + n) for r0, n in zip(starts, IN_PROJ_BLOCKS)]
    hns = []
    for sl in blocks:
        x = x_ref[sl, :]
        ms = jnp.mean(x * x, axis=-1, keepdims=True)
        hns.append((x * lax.rsqrt(ms + EPS) * g_ref[...]).astype(BF16))

    def proj(hn, idx):
        return jnp.dot(hn, wb_ref[:, idx * d:(idx + 1) * d], preferred_element_type=F32)

    for sl, hn in zip(blocks, hns):
        xm_ref[sl, :] = proj(hn, 0).astype(xm_ref.dtype)
    log_f = [_log_sigmoid(jnp.dot(hn, wfb_ref[...], preferred_element_type=F32) + fb_ref[...])
             for hn in hns]
    for sl, hn in zip(blocks, hns):
        gm_ref[sl, :] = _silu(proj(hn, 1)).astype(gm_ref.dtype)
    for sl, hn in zip(blocks, hns):
        gf_ref[sl, :] = _silu(proj(hn, 5)).astype(gf_ref.dtype)
    carry = carry_ref[...]
    for sl, lf in zip(blocks, log_f):
        n = lf.shape[0]
        c = _cumsum_rows(tri_ref[0:n, 0:n], lf) + carry
        c_ref[:, sl] = (c * LOG2E).T[0:c_ref.shape[0], :]
        carry = c[n - 1:n, :]
    carry_ref[...] = carry
    for sl, hn in zip(blocks, hns):
        q_ref[sl, :] = (proj(hn, 2) * q_scale).astype(q_ref.dtype)
    for sl, hn in zip(blocks, hns):
        k_ref[sl, :] = proj(hn, 3).astype(k_ref.dtype)
    for sl, hn in zip(blocks, hns):
        v_ref[sl, :] = proj(hn, 4).astype(v_ref.dtype)


def _in_proj(x2, gain, w_in_t, f_bias, seq_len, q_scale):
    t_rows, d = x2.shape
    rows, sub = sum(IN_PROJ_BLOCKS), max(IN_PROJ_BLOCKS)
    assert t_rows % rows == 0 and seq_len % rows == 0
    n_main = 6 * d
    n_gate = w_in_t.shape[0] - n_main
    assert 0 < n_gate <= SUBLANES and w_in_t.shape[1] == d
    row_spec = pl.BlockSpec((rows, d), lambda t: (t, 0))
    const = lambda shape: pl.BlockSpec(shape, lambda t: (0, 0), pipeline_mode=pl.Buffered(1))
    kern = functools.partial(_in_proj_kernel, tiles_per_seq=seq_len // rows, q_scale=q_scale)
    return pl.pallas_call(
        kern,
        grid=(t_rows // rows,),
        in_specs=[row_spec, const((1, d)), pl.BlockSpec(memory_space=pl.ANY), const((1, LANES))],
        out_specs=[row_spec] * 6 + [pl.BlockSpec((SUBLANES, rows), lambda t: (0, t))],
        out_shape=[jax.ShapeDtypeStruct((t_rows, d), BF16)] * 6
        + [jax.ShapeDtypeStruct((SUBLANES, t_rows), F32)],
        scratch_shapes=[pltpu.VMEM((d, n_main), BF16), pltpu.VMEM((d, LANES), BF16),
                        pltpu.VMEM((sub, sub), BF16), pltpu.VMEM((1, LANES), F32),
                        pltpu.VMEM((2, IN_PROJ_STAGE_ROWS, d), F32), pltpu.VMEM((n_gate, d), F32),
                        pltpu.SemaphoreType.DMA((3,))],
        compiler_params=pltpu.CompilerParams(
            dimension_semantics=("arbitrary",), vmem_limit_bytes=VMEM_LIMIT),
        name="in_proj",
    )(x2, gain, w_in_t, f_bias)


def _mlstm_kernel(xm_ref, gm_ref, cw_ref, cb_ref, wq_ref, wk_ref, wv_ref, wg_ref, gb_ref,
                  ng_ref, sk_ref, y_ref,
                  tail_ref, c_sc, n_sc, m_sc, tri_ref, shift_ref, gw_ref, *, k_scale):
    ci = pl.program_id(1)
    n_seqs, L, d_m = xm_ref.shape
    n_heads, dh, _ = wq_ref.shape

    @pl.when(ci == 0)
    def _():
        for h in range(n_heads):
            sl = slice(h * dh, (h + 1) * dh)
            g_q = jnp.dot(wq_ref[h], wg_ref[h * dh:(h + 1) * dh, :], preferred_element_type=F32)
            g_k = jnp.dot(wk_ref[h], wg_ref[d_m + h * dh:d_m + (h + 1) * dh, :],
                          preferred_element_type=F32)
            g_v = jnp.dot(wv_ref[h], wg_ref[2 * d_m + h * dh:2 * d_m + (h + 1) * dh, :],
                          preferred_element_type=F32)
            gw_ref[0, sl, :] = (g_q + g_k * k_scale).astype(BF16)
            gw_ref[1, sl, :] = g_v.astype(BF16)
        c_sc[...] = jnp.zeros_like(c_sc)
        n_sc[...] = jnp.zeros_like(n_sc)
        m_sc[...] = jnp.zeros_like(m_sc)
        tail_ref[...] = jnp.zeros_like(tail_ref)
        row = lax.broadcasted_iota(jnp.int32, (L, L), 0)
        col = lax.broadcasted_iota(jnp.int32, (L, L), 1)
        tri_ref[...] = jnp.where(col <= row, 1.0, 0.0).astype(BF16)
        for j in range(1, CONV_WIDTH):
            shift_ref[j - 1] = jnp.where(col == row - j, 1.0, 0.0).astype(BF16)

    seqs = range(n_seqs)
    fronts = [_mlstm_front(xm_ref.at[i], cw_ref, cb_ref, wq_ref, wk_ref, wv_ref, gw_ref, gb_ref,
                           tail_ref.at[i], tri_ref, shift_ref, k_scale) for i in seqs]
    for i in seqs:
        mixed = _mlstm_mix(*fronts[i][1:], c_sc.at[i], n_sc.at[i], m_sc.at[i])
        _mlstm_out(mixed, fronts[i][0], gm_ref.at[i], ng_ref, sk_ref, y_ref.at[i])


def _mlstm_front(xm_ref, cw_ref, cb_ref, wq_ref, wk_ref, wv_ref, gw_ref, gb_ref,
                 tail_ref, tri_ref, shift_ref, k_scale):
    L, d_m = xm_ref.shape
    dh = wq_ref.shape[1]
    n_heads = wq_ref.shape[0]
    halo = SUBLANES

    x_b = xm_ref[...]
    x_f = x_b.astype(F32)
    conv = cb_ref[...] + x_f * cw_ref[CONV_WIDTH - 1:CONV_WIDTH, :]
    edge = jnp.concatenate([tail_ref[...], jnp.zeros((halo, d_m), F32)], axis=0)
    head = jnp.zeros((halo, d_m), F32)
    for j in range(1, CONV_WIDTH):
        w_j = cw_ref[CONV_WIDTH - 1 - j:CONV_WIDTH - j, :]
        conv = conv + jnp.dot(shift_ref[j - 1], x_b, preferred_element_type=F32) * w_j
        head = head + edge[halo - j:2 * halo - j, :] * w_j
    conv = jnp.concatenate([conv[:halo] + head, conv[halo:]], axis=0)
    tail_ref[...] = x_f[L - halo:L, :]
    xc = _silu(conv)
    xc_b = xc.astype(BF16)

    gates = (jnp.dot(xc_b, gw_ref[0], preferred_element_type=F32)
             + jnp.dot(x_b, gw_ref[1], preferred_element_type=F32) + gb_ref[...])
    qs, ks, vs = [], [], []
    for h in range(n_heads):
        sl = slice(h * dh, (h + 1) * dh)
        q = jnp.dot(xc_b[:, sl], wq_ref[h], preferred_element_type=F32)
        k = jnp.dot(xc_b[:, sl], wk_ref[h], preferred_element_type=F32) * k_scale
        v = jnp.dot(x_b[:, sl], wv_ref[h], preferred_element_type=F32)
        qs.append(q.astype(BF16))
        ks.append((k, k.astype(BF16)))
        vs.append(v.astype(BF16))

    ig = gates[:, :LANES]
    bcum = _cumsum_rows(tri_ref[...], _log_sigmoid(gates[:, LANES:]))
    a_col = ig - bcum
    return xc, qs, ks, vs, a_col, bcum


def _mlstm_mix(qs, ks, vs, a_col, bcum, c_sc, n_sc, m_sc):
    n_heads = len(qs)
    L, dh = qs[0].shape
    a_rows = a_col.T
    causal = _tril_mask(L)
    ones_blk = jnp.ones((L, LANES), BF16)

    heads = range(n_heads)
    nt = (((1,), (1,)), ((), ()))
    m_prev = {h: m_sc[h][:, 0:1] for h in heads}
    dm = {h: jnp.where(causal, a_rows[h:h + 1, :], NEG_INF) for h in heads}
    m_run = {h: jnp.maximum(m_prev[h], jnp.max(dm[h], axis=1, keepdims=True)) for h in heads}
    w_inter = {h: jnp.exp(m_prev[h] - m_run[h]) for h in heads}
    s_b = {h: (lax.dot_general(qs[h], ks[h][1], nt, preferred_element_type=F32)
               * jnp.exp(dm[h] - m_run[h])).astype(BF16) for h in heads}
    num = {h: w_inter[h] * jnp.dot(qs[h], c_sc[h].astype(BF16), preferred_element_type=F32)
           + jnp.dot(s_b[h], vs[h], preferred_element_type=F32) for h in heads}
    qn = {h: lax.dot_general(qs[h], jnp.broadcast_to(n_sc[h], (LANES, dh)).astype(BF16), nt,
                             preferred_element_type=F32) for h in heads}
    den = {h: w_inter[h] * qn[h] + jnp.dot(s_b[h], ones_blk, preferred_element_type=F32)
           for h in heads}
    m_t = {h: bcum[:, h:h + 1] + m_run[h] for h in heads}
    inv = {h: 1.0 / jnp.maximum(jnp.abs(den[h]), jnp.exp(-m_t[h])) for h in heads}
    hh = [num[h] * jnp.concatenate([inv[h]] * (dh // LANES), axis=1) for h in heads]

    for h in heads:
        m_last = m_run[h][L - 1:L, :]
        decay = jnp.exp(m_prev[h] - m_last)
        kw = ks[h][0] * jnp.exp(a_col[:, h:h + 1] - m_last)
        c_sc[h] = decay * c_sc[h] + lax.dot_general(
            kw.astype(BF16), vs[h], (((0,), (0,)), ((), ())), preferred_element_type=F32)
        n_sc[h] = decay * n_sc[h] + jnp.sum(kw, axis=0, keepdims=True)
        m_sc[h] = jnp.broadcast_to(m_t[h][L - 1:L, :], m_sc.shape[1:])
    return hh


def _mlstm_out(hh, xc, gm_ref, ng_ref, sk_ref, y_ref):
    heads = range(len(hh))
    dh = hh[0].shape[1]
    mu = [jnp.mean(hh[h], axis=1, keepdims=True) for h in heads]
    cen = [hh[h] - mu[h] for h in heads]
    var = [jnp.mean(cen[h] * cen[h], axis=1, keepdims=True) for h in heads]
    for h in heads:
        sl = slice(h * dh, (h + 1) * dh)
        hn = cen[h] * lax.rsqrt(var[h] + EPS) * ng_ref[:, sl] + sk_ref[:, sl] * xc[:, sl]
        y_ref[:, sl] = (hn * gm_ref[:, sl].astype(F32)).astype(y_ref.dtype)


def _mlstm(xm, gm, conv_w, conv_b, wq, wk, wv, wg, gb, ngain, skip):
    b, s, d = xm.shape
    L = MLSTM_CHUNK
    assert s % L == 0
    n_heads, dh, _ = wq.shape
    n_seqs = MLSTM_SEQS
    assert b % n_seqs == 0
    seq_spec = pl.BlockSpec((n_seqs, L, d), lambda bi, ci: (bi, ci, 0))

    def const(shape):
        nd = len(shape)
        return pl.BlockSpec(shape, lambda bi, ci: (0,) * nd)

    kern = functools.partial(_mlstm_kernel, k_scale=dh ** -0.5)
    return pl.pallas_call(
        kern,
        grid=(b // n_seqs, s // L),
        in_specs=[seq_spec, seq_spec, const(conv_w.shape), const(conv_b.shape), const(wq.shape),
                  const(wk.shape), const(wv.shape), const(wg.shape), const(gb.shape),
                  const(ngain.shape), const(skip.shape)],
        out_specs=seq_spec,
        out_shape=jax.ShapeDtypeStruct((b, s, d), BF16),
        scratch_shapes=[
            pltpu.VMEM((n_seqs, SUBLANES, d), F32),
            pltpu.VMEM((n_seqs, n_heads, dh, dh), F32),
            pltpu.VMEM((n_seqs, n_heads, 1, dh), F32),
            pltpu.VMEM((n_seqs, n_heads, 1, LANES), F32),
            pltpu.VMEM((L, L), BF16),
            pltpu.VMEM((CONV_WIDTH - 1, L, L), BF16),
            pltpu.VMEM((2, d, wg.shape[1]), BF16),
        ],
        compiler_params=pltpu.CompilerParams(
            dimension_semantics=("arbitrary", "arbitrary"), vmem_limit_bytes=VMEM_LIMIT),
        name="mlstm",
    )(xm, gm, conv_w, conv_b, wq, wk, wv, wg, gb, ngain, skip)


def _fox_kernel(q_ref, k_ref, v_ref, cr_ref, gf_ref, g_ref, o_ref):
    s_len = q_ref.shape[0]
    dh = LANES
    n_local = q_ref.shape[1] // dh
    tq = FOX_Q
    causal = _tril_mask(tq)
    head0 = pl.program_id(1) * n_local

    def scores(unit):
        hl, qi = unit
        r0 = qi * tq
        cols = slice(hl * dh, (hl + 1) * dh)
        return lax.dot_general(q_ref[r0:r0 + tq, cols], k_ref[0:r0 + tq, cols],
                               (((1,), (1,)), ((), ())), preferred_element_type=F32)

    order = [(hl, qi) for hl in range(n_local) for qi in reversed(range(s_len // tq))]
    ahead = [scores(u) for u in order[:FOX_LOOKAHEAD]]
    for pos, (hl, qi) in enumerate(order):
        r0 = qi * tq
        kv = r0 + tq
        cols = slice(hl * dh, (hl + 1) * dh)
        s = ahead.pop(0)
        if pos + FOX_LOOKAHEAD < len(order):
            ahead.append(scores(order[pos + FOX_LOOKAHEAD]))
        s = s - cr_ref[pl.ds(head0 + hl, 1), 0:kv]
        s_diag = jnp.where(causal, s[:, r0:], NEG_INF)
        s = s_diag if qi == 0 else jnp.concatenate([s[:, :r0], s_diag], axis=1)
        m = jnp.max(s, axis=1, keepdims=True)
        p = jnp.exp2(s - m).astype(BF16)
        v_ext = jnp.concatenate([v_ref[0:kv, cols], jnp.ones((kv, dh), BF16)], axis=1)
        acc = jnp.dot(p, v_ext, preferred_element_type=F32)

        out = acc[:, :dh] / acc[:, dh:]
        ms = jnp.mean(out * out, axis=1, keepdims=True)
        o_ref[r0:kv, cols] = (out * lax.rsqrt(ms + EPS) * g_ref[:, cols]
                              * gf_ref[r0:kv, cols].astype(F32)).astype(o_ref.dtype)


def _fox(q, k, v, c_row, gf, gain):
    b, s, d = q.shape
    n_heads = N_FOX_HEADS
    dh = d // n_heads
    assert dh == LANES and s % FOX_Q == 0 and n_heads % FOX_HEADS == 0
    width = FOX_HEADS * dh
    head_spec = pl.BlockSpec((None, s, width), lambda bi, hi: (bi, 0, hi))
    return pl.pallas_call(
        _fox_kernel,
        grid=(b, n_heads // FOX_HEADS),
        in_specs=[head_spec, head_spec, head_spec,
                  pl.BlockSpec((c_row.shape[0], s), lambda bi, hi: (0, bi)),
                  head_spec,
                  pl.BlockSpec((1, width), lambda bi, hi: (0, hi))],
        out_specs=head_spec,
        out_shape=jax.ShapeDtypeStruct((b, s, d), BF16),
        compiler_params=pltpu.CompilerParams(
            dimension_semantics=("arbitrary", "arbitrary"), vmem_limit_bytes=VMEM_LIMIT),
        name="fox",
    )(q, k, v, c_row, gf, gain)


def _out_proj_kernel(ym_ref, yf_ref, x_ref, w_ref, g_ref, o_ref, wb_ref, *, final_norm):
    d_m = ym_ref.shape[1]

    @pl.when(pl.program_id(0) == 0)
    def _():
        wb_ref[...] = w_ref[...].astype(BF16)

    rows = x_ref.shape[0]
    for r0 in range(0, rows, OUT_PROJ_SUB):
        sl = slice(r0, r0 + OUT_PROJ_SUB)
        y = jnp.dot(ym_ref[sl, :], wb_ref[0:d_m, :], preferred_element_type=F32)
        y = y + jnp.dot(yf_ref[sl, :], wb_ref[d_m:, :], preferred_element_type=F32)
        r = x_ref[sl, :] + y
        if final_norm:
            ms = jnp.mean(r * r, axis=-1, keepdims=True)
            r = r * lax.rsqrt(ms + EPS) * g_ref[...]
        o_ref[sl, :] = r


def _out_proj(ym, yf, x2, w_out, gain, final_norm):
    t_rows, d = x2.shape
    rows = OUT_PROJ_ROWS
    assert t_rows % rows == 0
    row = lambda width: pl.BlockSpec((rows, width), lambda t: (t, 0))
    const = lambda shape: pl.BlockSpec(shape, lambda t: (0, 0), pipeline_mode=pl.Buffered(1))
    kern = functools.partial(_out_proj_kernel, final_norm=final_norm)
    return pl.pallas_call(
        kern,
        grid=(t_rows // rows,),
        in_specs=[row(ym.shape[1]), row(yf.shape[1]), row(d), const(w_out.shape), const((1, d))],
        out_specs=row(d),
        out_shape=jax.ShapeDtypeStruct((t_rows, d), F32),
        scratch_shapes=[pltpu.VMEM(w_out.shape, BF16)],
        compiler_params=pltpu.CompilerParams(
            dimension_semantics=("arbitrary",), vmem_limit_bytes=VMEM_LIMIT),
        name="out_proj",
    )(ym, yf, x2, w_out, gain)


def _block_diag(w, dh):
    nb, bs, _ = w.shape
    d = nb * bs
    tiled = jnp.broadcast_to(w.reshape(d, 1, bs), (d, dh // bs, bs)).reshape(d, dh)
    r = lax.broadcasted_iota(jnp.int32, (d, dh), 0)
    c = lax.broadcasted_iota(jnp.int32, (d, dh), 1)
    keep = (r % dh) // bs == c // bs
    return jnp.where(keep, tiled, 0.0).reshape(d // dh, dh, dh)


def kernel(x, norm_gain, w_in, conv_w, conv_b, w_q_m, w_k_m, w_v_m, w_igate, b_igate, w_fgate,
           b_fgate, mlstm_norm_gain, mlstm_skip, fox_forget_bias, fox_norm_gain, w_out,
           final_norm_gain):
    b, s, d = x.shape
    depth = norm_gain.shape[0]
    d_m = conv_w.shape[-1]
    d_f = fox_norm_gain.shape[-1]
    dh_m = d_m // N_MLSTM_HEADS
    dh_f = d_f // N_FOX_HEADS
    n_main = 2 * d_m + 4 * d_f
    assert d_m == d and d_f == d

    x2 = x.reshape(b * s, d)
    for l in range(depth):
        assert w_in.shape[2] == n_main + N_FOX_HEADS
        f_bias = jnp.pad(fox_forget_bias[l], (0, LANES - N_FOX_HEADS)).reshape(1, LANES)
        xm, gm, qf, kf, vf, gf, c_keys = _in_proj(
            x2, norm_gain[l].reshape(1, d), w_in[l].T, f_bias, s, dh_f ** -0.5 * LOG2E)

        lane_pad = ((0, 0), (0, LANES - N_MLSTM_HEADS))
        wg = jnp.concatenate([jnp.pad(w_igate[l], lane_pad), jnp.pad(w_fgate[l], lane_pad)], axis=1)
        gb = jnp.concatenate([jnp.pad(b_igate[l][None], lane_pad),
                              jnp.pad(b_fgate[l][None], lane_pad)], axis=1)
        y_m = _mlstm(xm.reshape(b, s, d_m), gm.reshape(b, s, d_m), conv_w[l],
                     conv_b[l].reshape(1, d_m),
                     _block_diag(w_q_m[l], dh_m).astype(BF16),
                     _block_diag(w_k_m[l], dh_m).astype(BF16),
                     _block_diag(w_v_m[l], dh_m).astype(BF16),
                     wg.astype(BF16), gb,
                     mlstm_norm_gain[l].reshape(1, d_m), mlstm_skip[l].reshape(1, d_m))

        y_f = _fox(qf.reshape(b, s, d_f), kf.reshape(b, s, d_f), vf.reshape(b, s, d_f), c_keys,
                   gf.reshape(b, s, d_f), fox_norm_gain[l].reshape(1, d_f))

        x2 = _out_proj(y_m.reshape(b * s, d_m), y_f.reshape(b * s, d_f), x2, w_out[l],
                       final_norm_gain.reshape(1, d), final_norm=(l == depth - 1))
    return x2.reshape(b, s, d)
```

```python
import functools
import math

import jax
import jax.numpy as jnp
from jax import lax
from jax.experimental import pallas as pl
from jax.experimental.pallas import tpu as pltpu

EPS = 1e-6
N_MLSTM_HEADS = 4
N_FOX_HEADS = 8
CONV_WIDTH = 4

LANES = 128
SUBLANES = 8
VMEM_LIMIT = 56 * 1024 * 1024

IN_PROJ_BLOCKS = (256, 256, 256, 256)
IN_PROJ_STAGE_ROWS = 512
OUT_PROJ_ROWS = 1024
OUT_PROJ_SUB = 256
MLSTM_CHUNK = 256
MLSTM_SEQS = 2
FOX_HEADS = 2
FOX_LOOKAHEAD = 2
FOX_Q = 256
LOG2E = math.log2(math.e)

BF16 = jnp.bfloat16
F32 = jnp.float32
NEG_INF = float("-inf")


def _log_sigmoid(z):
    return jnp.minimum(z, 0.0) - jnp.log(1.0 + jnp.exp(-jnp.abs(z)))


def _sigmoid(z):
    return 1.0 / (1.0 + jnp.exp(-z))


def _silu(z):
    return z * _sigmoid(z)


def _split3(v):
    hi = v.astype(BF16)
    r = v - hi.astype(F32)
    mid = r.astype(BF16)
    lo = (r - mid.astype(F32)).astype(BF16)
    return hi, mid, lo


def _cumsum_rows(tri, v):
    w = v.shape[1]
    cat = jnp.concatenate(_split3(v), axis=1)
    cs = jnp.dot(tri, cat, preferred_element_type=F32)
    return cs[:, :w] + cs[:, w:2 * w] + cs[:, 2 * w:]


def _tril_mask(n):
    row = lax.broadcasted_iota(jnp.int32, (n, n), 0)
    col = lax.broadcasted_iota(jnp.int32, (n, n), 1)
    return col <= row


def _in_proj_kernel(x_ref, g_ref, w_hbm, fb_ref,
                    xm_ref, gm_ref, q_ref, k_ref, v_ref, gf_ref, c_ref,
                    wb_ref, wfb_ref, tri_ref, carry_ref, stage_ref, gstage_ref, sems,
                    *, tiles_per_seq, q_scale):
    t = pl.program_id(0)
    rows, d = x_ref.shape
    sub = tri_ref.shape[0]
    n_main = wb_ref.shape[1]
    n_gate = gstage_ref.shape[0]
    stage_rows = stage_ref.shape[1]
    n_chunks = n_main // stage_rows

    def chunk_copy(c):
        return pltpu.make_async_copy(w_hbm.at[pl.ds(c * stage_rows, stage_rows), :], stage_ref.at[c % 2],
                                     sems.at[c % 2])

    def gate_copy():
        return pltpu.make_async_copy(w_hbm.at[pl.ds(n_main, n_gate), :], gstage_ref, sems.at[2])

    @pl.when(t == 0)
    def _():
        gate_copy().start()
        chunk_copy(0).start()
        tri_ref[...] = jnp.where(_tril_mask(sub), 1.0, 0.0).astype(BF16)
        for c in range(n_chunks):
            if c + 1 < n_chunks:
                chunk_copy(c + 1).start()
            chunk_copy(c).wait()
            wb_ref[:, c * stage_rows:(c + 1) * stage_rows] = stage_ref[c % 2].T.astype(BF16)
        gate_copy().wait()
        w_gate = jnp.concatenate([gstage_ref[...], jnp.zeros((LANES - n_gate, d), F32)], axis=0)
        wfb_ref[...] = w_gate.T.astype(BF16)

    @pl.when(t % tiles_per_seq == 0)
    def _():
        carry_ref[...] = jnp.zeros_like(carry_ref)

    starts = [sum(IN_PROJ_BLOCKS[:i]) for i in range(len(IN_PROJ_BLOCKS))]
    blocks = [slice(r0, r0 + n) for r0, n in zip(starts, IN_PROJ_BLOCKS)]
    hns = []
    for sl in blocks:
        x = x_ref[sl, :]
        ms = jnp.mean(x * x, axis=-1, keepdims=True)
        hns.append((x * lax.rsqrt(ms + EPS) * g_ref[...]).astype(BF16))

    def proj(hn, idx):
        return jnp.dot(hn, wb_ref[:, idx * d:(idx + 1) * d], preferred_element_type=F32)

    for sl, hn in zip(blocks, hns):
        xm_ref[sl, :] = proj(hn, 0).astype(xm_ref.dtype)
    log_f = [_log_sigmoid(jnp.dot(hn, wfb_ref[...], preferred_element_type=F32) + fb_ref[...])
             for hn in hns]
    for sl, hn in zip(blocks, hns):
        gm_ref[sl, :] = _silu(proj(hn, 1)).astype(gm_ref.dtype)
    for sl, hn in zip(blocks, hns):
        gf_ref[sl, :] = _silu(proj(hn, 5)).astype(gf_ref.dtype)
    carry = carry_ref[...]
    for sl, lf in zip(blocks, log_f):
        n = lf.shape[0]
        c = _cumsum_rows(tri_ref[0:n, 0:n], lf) + carry
        c_ref[:, sl] = (c * LOG2E).T[0:c_ref.shape[0], :]
        carry = c[n - 1:n, :]
    carry_ref[...] = carry
    for sl, hn in zip(blocks, hns):
        q_ref[sl, :] = (proj(hn, 2) * q_scale).astype(q_ref.dtype)
    for sl, hn in zip(blocks, hns):
        k_ref[sl, :] = proj(hn, 3).astype(k_ref.dtype)
    for sl, hn in zip(blocks, hns):
        v_ref[sl, :] = proj(hn, 4).astype(v_ref.dtype)


def _in_proj(x2, gain, w_in_t, f_bias, seq_len, q_scale):
    t_rows, d = x2.shape
    rows, sub = sum(IN_PROJ_BLOCKS), max(IN_PROJ_BLOCKS)
    assert t_rows % rows == 0 and seq_len % rows == 0
    n_main = 6 * d
    n_gate = w_in_t.shape[0] - n_main
    assert 0 < n_gate <= SUBLANES and w_in_t.shape[1] == d
    row_spec = pl.BlockSpec((rows, d), lambda t: (t, 0))
    const = lambda shape: pl.BlockSpec(shape, lambda t: (0, 0), pipeline_mode=pl.Buffered(1))
    kern = functools.partial(_in_proj_kernel, tiles_per_seq=seq_len // rows, q_scale=q_scale)
    return pl.pallas_call(
        kern,
        grid=(t_rows // rows,),
        in_specs=[row_spec, const((1, d)), pl.BlockSpec(memory_space=pl.ANY), const((1, LANES))],
        out_specs=[row_spec] * 6 + [pl.BlockSpec((SUBLANES, rows), lambda t: (0, t))],
        out_shape=[jax.ShapeDtypeStruct((t_rows, d), BF16)] * 6
        + [jax.ShapeDtypeStruct((SUBLANES, t_rows), F32)],
        scratch_shapes=[pltpu.VMEM((d, n_main), BF16), pltpu.VMEM((d, LANES), BF16),
                        pltpu.VMEM((sub, sub), BF16), pltpu.VMEM((1, LANES), F32),
                        pltpu.VMEM((2, IN_PROJ_STAGE_ROWS, d), F32), pltpu.VMEM((n_gate, d), F32),
                        pltpu.SemaphoreType.DMA((3,))],
        compiler_params=pltpu.CompilerParams(
            dimension_semantics=("arbitrary",), vmem_limit_bytes=VMEM_LIMIT),
        name="in_proj",
    )(x2, gain, w_in_t, f_bias)


def _mlstm_kernel(xm_ref, gm_ref, cw_ref, cb_ref, wq_ref, wk_ref, wv_ref, wg_ref, gb_ref,
                  ng_ref, sk_ref, y_ref,
                  tail_ref, c_sc, n_sc, m_sc, tri_ref, shift_ref, gw_ref, *, k_scale):
    ci = pl.program_id(1)
    n_seqs, L, d_m = xm_ref.shape
    n_heads, dh, _ = wq_ref.shape

    @pl.when(ci == 0)
    def _():
        for h in range(n_heads):
            sl = slice(h * dh, (h + 1) * dh)
            g_q = jnp.dot(wq_ref[h], wg_ref[h * dh:(h + 1) * dh, :], preferred_element_type=F32)
            g_k = jnp.dot(wk_ref[h], wg_ref[d_m + h * dh:d_m + (h + 1) * dh, :],
                          preferred_element_type=F32)
            g_v = jnp.dot(wv_ref[h], wg_ref[2 * d_m + h * dh:2 * d_m + (h + 1) * dh, :],
                          preferred_element_type=F32)
            gw_ref[0, sl, :] = (g_q + g_k * k_scale).astype(BF16)
            gw_ref[1, sl, :] = g_v.astype(BF16)
        c_sc[...] = jnp.zeros_like(c_sc)
        n_sc[...] = jnp.zeros_like(n_sc)
        m_sc[...] = jnp.zeros_like(m_sc)
        tail_ref[...] = jnp.zeros_like(tail_ref)
        row = lax.broadcasted_iota(jnp.int32, (L, L), 0)
        col = lax.broadcasted_iota(jnp.int32, (L, L), 1)
        tri_ref[...] = jnp.where(col <= row, 1.0, 0.0).astype(BF16)
        for j in range(1, CONV_WIDTH):
            shift_ref[j - 1] = jnp.where(col == row - j, 1.0, 0.0).astype(BF16)

    seqs = range(n_seqs)
    fronts = [_mlstm_front(xm_ref.at[i], cw_ref, cb_ref, wq_ref, wk_ref, wv_ref, gw_ref, gb_ref,
                           tail_ref.at[i], tri_ref, shift_ref, k_scale) for i in seqs]
    for i in seqs:
        mixed = _mlstm_mix(*fronts[i][1:], c_sc.at[i], n_sc.at[i], m_sc.at[i])
        _mlstm_out(mixed, fronts[i][0], gm_ref.at[i], ng_ref, sk_ref, y_ref.at[i])


def _mlstm_front(xm_ref, cw_ref, cb_ref, wq_ref, wk_ref, wv_ref, gw_ref, gb_ref,
                 tail_ref, tri_ref, shift_ref, k_scale):
    L, d_m = xm_ref.shape
    dh = wq_ref.shape[1]
    n_heads = wq_ref.shape[0]
    halo = SUBLANES

    x_b = xm_ref[...]
    x_f = x_b.astype(F32)
    conv = cb_ref[...] + x_f * cw_ref[CONV_WIDTH - 1:CONV_WIDTH, :]
    edge = jnp.concatenate([tail_ref[...], jnp.zeros((halo, d_m), F32)], axis=0)
    head = jnp.zeros((halo, d_m), F32)
    for j in range(1, CONV_WIDTH):
        w_j = cw_ref[CONV_WIDTH - 1 - j:CONV_WIDTH - j, :]
        conv = conv + jnp.dot(shift_ref[j - 1], x_b, preferred_element_type=F32) * w_j
        head = head + edge[halo - j:2 * halo - j, :] * w_j
    conv = jnp.concatenate([conv[:halo] + head, conv[halo:]], axis=0)
    tail_ref[...] = x_f[L - halo:L, :]
    xc = _silu(conv)
    xc_b = xc.astype(BF16)

    gates = (jnp.dot(xc_b, gw_ref[0], preferred_element_type=F32)
             + jnp.dot(x_b, gw_ref[1], preferred_element_type=F32) + gb_ref[...])
    qs, ks, vs = [], [], []
    for h in range(n_heads):
        sl = slice(h * dh, (h + 1) * dh)
        q = jnp.dot(xc_b[:, sl], wq_ref[h], preferred_element_type=F32)
        k = jnp.dot(xc_b[:, sl], wk_ref[h], preferred_element_type=F32) * k_scale
        v = jnp.dot(x_b[:, sl], wv_ref[h], preferred_element_type=F32)
        qs.append(q.astype(BF16))
        ks.append((k, k.astype(BF16)))
        vs.append(v.astype(BF16))

    ig = gates[:, :LANES]
    bcum = _cumsum_rows(tri_ref[...], _log_sigmoid(gates[:, LANES:]))
    a_col = ig - bcum
    return xc, qs, ks, vs, a_col, bcum


def _mlstm_mix(qs, ks, vs, a_col, bcum, c_sc, n_sc, m_sc):
    n_heads = len(qs)
    L, dh = qs[0].shape
    a_rows = a_col.T
    causal = _tril_mask(L)
    ones_blk = jnp.ones((L, LANES), BF16)

    heads = range(n_heads)
    nt = (((1,), (1,)), ((), ()))
    m_prev = {h: m_sc[h][:, 0:1] for h in heads}
    dm = {h: jnp.where(causal, a_rows[h:h + 1, :], NEG_INF) for h in heads}
    m_run = {h: jnp.maximum(m_prev[h], jnp.max(dm[h], axis=1, keepdims=True)) for h in heads}
    w_inter = {h: jnp.exp(m_prev[h] - m_run[h]) for h in heads}
    s_b = {h: (lax.dot_general(qs[h], ks[h][1], nt, preferred_element_type=F32)
               * jnp.exp(dm[h] - m_run[h])).astype(BF16) for h in heads}
    num = {h: w_inter[h] * jnp.dot(qs[h], c_sc[h].astype(BF16), preferred_element_type=F32)
           + jnp.dot(s_b[h], vs[h], preferred_element_type=F32) for h in heads}
    qn = {h: lax.dot_general(qs[h], jnp.broadcast_to(n_sc[h], (LANES, dh)).astype(BF16), nt,
                             preferred_element_type=F32) for h in heads}
    den = {h: w_inter[h] * qn[h] + jnp.dot(s_b[h], ones_blk, preferred_element_type=F32)
           for h in heads}
    m_t = {h: bcum[:, h:h + 1] + m_run[h] for h in heads}
    inv = {h: 1.0 / jnp.maximum(jnp.abs(den[h]), jnp.exp(-m_t[h])) for h in heads}
    hh = [num[h] * jnp.concatenate([inv[h]] * (dh // LANES), axis=1) for h in heads]

    for h in heads:
        m_last = m_run[h][L - 1:L, :]
        decay = jnp.exp(m_prev[h] - m_last)
        kw = ks[h][0] * jnp.exp(a_col[:, h:h + 1] - m_last)
        c_sc[h] = decay * c_sc[h] + lax.dot_general(
            kw.astype(BF16), vs[h], (((0,), (0,)), ((), ())), preferred_element_type=F32)
        n_sc[h] = decay * n_sc[h] + jnp.sum(kw, axis=0, keepdims=True)
        m_sc[h] = jnp.broadcast_to(m_t[h][L - 1:L, :], m_sc.shape[1:])
    return hh


def _mlstm_out(hh, xc, gm_ref, ng_ref, sk_ref, y_ref):
    heads = range(len(hh))
    dh = hh[0].shape[1]
    mu = [jnp.mean(hh[h], axis=1, keepdims=True) for h in heads]
    cen = [hh[h] - mu[h] for h in heads]
    var = [jnp.mean(cen[h] * cen[h], axis=1, keepdims=True) for h in heads]
    for h in heads:
        sl = slice(h * dh, (h + 1) * dh)
        hn = cen[h] * lax.rsqrt(var[h] + EPS) * ng_ref[:, sl] + sk_ref[:, sl] * xc[:, sl]
        y_ref[:, sl] = (hn * gm_ref[:, sl].astype(F32)).astype(y_ref.dtype)


def _mlstm(xm, gm, conv_w, conv_b, wq, wk, wv, wg, gb, ngain, skip):
    b, s, d = xm.shape
    L = MLSTM_CHUNK
    assert s % L == 0
    n_heads, dh, _ = wq.shape
    n_seqs = MLSTM_SEQS
    assert b % n_seqs == 0
    seq_spec = pl.BlockSpec((n_seqs, L, d), lambda bi, ci: (bi, ci, 0))

    def const(shape):
        nd = len(shape)
        return pl.BlockSpec(shape, lambda bi, ci: (0,) * nd)

    kern = functools.partial(_mlstm_kernel, k_scale=dh ** -0.5)
    return pl.pallas_call(
        kern,
        grid=(b // n_seqs, s // L),
        in_specs=[seq_spec, seq_spec, const(conv_w.shape), const(conv_b.shape), const(wq.shape),
                  const(wk.shape), const(wv.shape), const(wg.shape), const(gb.shape),
                  const(ngain.shape), const(skip.shape)],
        out_specs=seq_spec,
        out_shape=jax.ShapeDtypeStruct((b, s, d), BF16),
        scratch_shapes=[
            pltpu.VMEM((n_seqs, SUBLANES, d), F32),
            pltpu.VMEM((n_seqs, n_heads, dh, dh), F32),
            pltpu.VMEM((n_seqs, n_heads, 1, dh), F32),
            pltpu.VMEM((n_seqs, n_heads, 1, LANES), F32),
            pltpu.VMEM((L, L), BF16),
            pltpu.VMEM((CONV_WIDTH - 1, L, L), BF16),
            pltpu.VMEM((2, d, wg.shape[1]), BF16),
        ],
        compiler_params=pltpu.CompilerParams(
            dimension_semantics=("arbitrary", "arbitrary"), vmem_limit_bytes=VMEM_LIMIT),
        name="mlstm",
    )(xm, gm, conv_w, conv_b, wq, wk, wv, wg, gb, ngain, skip)


def _fox_kernel(q_ref, k_ref, v_ref, cr_ref, gf_ref, g_ref, o_ref):
    s_len = q_ref.shape[0]
    dh = LANES
    n_local = q_ref.shape[1] // dh
    tq = FOX_Q
    causal = _tril_mask(tq)
    head0 = pl.program_id(1) * n_local

    def scores(unit):
        hl, qi = unit
        r0 = qi * tq
        cols = slice(hl * dh, (hl + 1) * dh)
        return lax.dot_general(q_ref[r0:r0 + tq, cols], k_ref[0:r0 + tq, cols],
                               (((1,), (1,)), ((), ())), preferred_element_type=F32)

    order = [(hl, qi) for hl in range(n_local) for qi in reversed(range(s_len // tq))]
    ahead = [scores(u) for u in order[:FOX_LOOKAHEAD]]
    for pos, (hl, qi) in enumerate(order):
        r0 = qi * tq
        kv = r0 + tq
        cols = slice(hl * dh, (hl + 1) * dh)
        s = ahead.pop(0)
        if pos + FOX_LOOKAHEAD < len(order):
            ahead.append(scores(order[pos + FOX_LOOKAHEAD]))
        s = s - cr_ref[pl.ds(head0 + hl, 1), 0:kv]
        s_diag = jnp.where(causal, s[:, r0:], NEG_INF)
        s = s_diag if qi == 0 else jnp.concatenate([s[:, :r0], s_diag], axis=1)
        m = jnp.max(s, axis=1, keepdims=True)
        p = jnp.exp2(s - m).astype(BF16)
        v_ext = jnp.concatenate([v_ref[0:kv, cols], jnp.ones((kv, dh), BF16)], axis=1)
        acc = jnp.dot(p, v_ext, preferred_element_type=F32)

        out = acc[:, :dh] / acc[:, dh:]
        ms = jnp.mean(out * out, axis=1, keepdims=True)
        o_ref[r0:kv, cols] = (out * lax.rsqrt(ms + EPS) * g_ref[:, cols]
                              * gf_ref[r0:kv, cols].astype(F32)).astype(o_ref.dtype)


def _fox(q, k, v, c_row, gf, gain):
    b, s, d = q.shape
    n_heads = N_FOX_HEADS
    dh = d // n_heads
    assert dh == LANES and s % FOX_Q == 0 and n_heads % FOX_HEADS == 0
    width = FOX_HEADS * dh
    head_spec = pl.BlockSpec((None, s, width), lambda bi, hi: (bi, 0, hi))
    return pl.pallas_call(
        _fox_kernel,
        grid=(b, n_heads // FOX_HEADS),
        in_specs=[head_spec, head_spec, head_spec,
                  pl.BlockSpec((c_row.shape[0], s), lambda bi, hi: (0, bi)),
                  head_spec,
                  pl.BlockSpec((1, width), lambda bi, hi: (0, hi))],
        out_specs=head_spec,
        out_shape=jax.ShapeDtypeStruct((b, s, d), BF16),
        compiler_params=pltpu.CompilerParams(
            dimension_semantics=("arbitrary", "arbitrary"), vmem_limit_bytes=VMEM_LIMIT),
        name="fox",
    )(q, k, v, c_row, gf, gain)


def _out_proj_kernel(ym_ref, yf_ref, x_ref, w_ref, g_ref, o_ref, wb_ref, *, final_norm):
    d_m = ym_ref.shape[1]

    @pl.when(pl.program_id(0) == 0)
    def _():
        wb_ref[...] = w_ref[...].astype(BF16)

    rows = x_ref.shape[0]
    for r0 in range(0, rows, OUT_PROJ_SUB):
        sl = slice(r0, r0 + OUT_PROJ_SUB)
        y = jnp.dot(ym_ref[sl, :], wb_ref[0:d_m, :], preferred_element_type=F32)
        y = y + jnp.dot(yf_ref[sl, :], wb_ref[d_m:, :], preferred_element_type=F32)
        r = x_ref[sl, :] + y
        if final_norm:
            ms = jnp.mean(r * r, axis=-1, keepdims=True)
            r = r * lax.rsqrt(ms + EPS) * g_ref[...]
        o_ref[sl, :] = r


def _out_proj(ym, yf, x2, w_out, gain, final_norm):
    t_rows, d = x2.shape
    rows = OUT_PROJ_ROWS
    assert t_rows % rows == 0
    row = lambda width: pl.BlockSpec((rows, width), lambda t: (t, 0))
    const = lambda shape: pl.BlockSpec(shape, lambda t: (0, 0), pipeline_mode=pl.Buffered(1))
    kern = functools.partial(_out_proj_kernel, final_norm=final_norm)
    return pl.pallas_call(
        kern,
        grid=(t_rows // rows,),
        in_specs=[row(ym.shape[1]), row(yf.shape[1]), row(d), const(w_out.shape), const((1, d))],
        out_specs=row(d),
        out_shape=jax.ShapeDtypeStruct((t_rows, d), F32),
        scratch_shapes=[pltpu.VMEM(w_out.shape, BF16)],
        compiler_params=pltpu.CompilerParams(
            dimension_semantics=("arbitrary",), vmem_limit_bytes=VMEM_LIMIT),
        name="out_proj",
    )(ym, yf, x2, w_out, gain)


def _block_diag(w, dh):
    nb, bs, _ = w.shape
    d = nb * bs
    tiled = jnp.broadcast_to(w.reshape(d, 1, bs), (d, dh // bs, bs)).reshape(d, dh)
    r = lax.broadcasted_iota(jnp.int32, (d, dh), 0)
    c = lax.broadcasted_iota(jnp.int32, (d, dh), 1)
    keep = (r % dh) // bs == c // bs
    return jnp.where(keep, tiled, 0.0).reshape(d // dh, dh, dh)


def kernel(x, norm_gain, w_in, conv_w, conv_b, w_q_m, w_k_m, w_v_m, w_igate, b_igate, w_fgate,
           b_fgate, mlstm_norm_gain, mlstm_skip, fox_forget_bias, fox_norm_gain, w_out,
           final_norm_gain):
    b, s, d = x.shape
    depth = norm_gain.shape[0]
    d_m = conv_w.shape[-1]
    d_f = fox_norm_gain.shape[-1]
    dh_m = d_m // N_MLSTM_HEADS
    dh_f = d_f // N_FOX_HEADS
    n_main = 2 * d_m + 4 * d_f
    assert d_m == d and d_f == d

    x2 = x.reshape(b * s, d)
    for l in range(depth):
        assert w_in.shape[2] == n_main + N_FOX_HEADS
        f_bias = jnp.pad(fox_forget_bias[l], (0, LANES - N_FOX_HEADS)).reshape(1, LANES)
        xm, gm, qf, kf, vf, gf, c_keys = _in_proj(
            x2, norm_gain[l].reshape(1, d), w_in[l].T, f_bias, s, dh_f ** -0.5 * LOG2E)

        lane_pad = ((0, 0), (0, LANES - N_MLSTM_HEADS))
        wg = jnp.concatenate([jnp.pad(w_igate[l], lane_pad), jnp.pad(w_fgate[l], lane_pad)], axis=1)
        gb = jnp.concatenate([jnp.pad(b_igate[l][None], lane_pad),
                              jnp.pad(b_fgate[l][None], lane_pad)], axis=1)
        y_m = _mlstm(xm.reshape(b, s, d_m), gm.reshape(b, s, d_m), conv_w[l],
                     conv_b[l].reshape(1, d_m),
                     _block_diag(w_q_m[l], dh_m).astype(BF16),
                     _block_diag(w_k_m[l], dh_m).astype(BF16),
                     _block_diag(w_v_m[l], dh_m).astype(BF16),
                     wg.astype(BF16), gb,
                     mlstm_norm_gain[l].reshape(1, d_m), mlstm_skip[l].reshape(1, d_m))

        y_f = _fox(qf.reshape(b, s, d_f), kf.reshape(b, s, d_f), vf.reshape(b, s, d_f), c_keys,
                   gf.reshape(b, s, d_f), fox_norm_gain[l].reshape(1, d_f))

        x2 = _out_proj(y_m.reshape(b * s, d_m), y_f.reshape(b * s, d_f), x2, w_out[l],
                       final_norm_gain.reshape(1, d), final_norm=(l == depth - 1))
    return x2.reshape(b, s, d)
```

```python
import functools
import math

import jax
import jax.numpy as jnp
from jax import lax
from jax.experimental import pallas as pl
from jax.experimental.pallas import tpu as pltpu

EPS = 1e-6
N_MLSTM_HEADS = 4
N_FOX_HEADS = 8
CONV_WIDTH = 4

LANES = 128
SUBLANES = 8
VMEM_LIMIT = 56 * 1024 * 1024

IN_PROJ_BLOCKS = (256, 256, 256, 256)
IN_PROJ_ORDER = (0, 1, 5, 2, 3, 4)
IN_PROJ_STAGE_ROWS = 512
OUT_PROJ_ROWS = 1024
OUT_PROJ_SUB = 256
MLSTM_CHUNK = 256
MLSTM_SEQS = 2
FOX_HEADS = 4
FOX_LOOKAHEAD = 2
FOX_Q = 256
LOG2E = math.log2(math.e)

BF16 = jnp.bfloat16
F32 = jnp.float32
NEG_INF = float("-inf")


def _log_sigmoid(z):
    return jnp.minimum(z, 0.0) - jnp.log(1.0 + jnp.exp(-jnp.abs(z)))


def _sigmoid(z):
    return 1.0 / (1.0 + jnp.exp(-z))


def _silu(z):
    return z * _sigmoid(z)


def _split3(v):
    hi = v.astype(BF16)
    r = v - hi.astype(F32)
    mid = r.astype(BF16)
    lo = (r - mid.astype(F32)).astype(BF16)
    return hi, mid, lo


def _cumsum_rows(tri, v):
    w = v.shape[1]
    cat = jnp.concatenate(_split3(v), axis=1)
    cs = jnp.dot(tri, cat, preferred_element_type=F32)
    return cs[:, :w] + cs[:, w:2 * w] + cs[:, 2 * w:]


def _tril_mask(n):
    row = lax.broadcasted_iota(jnp.int32, (n, n), 0)
    col = lax.broadcasted_iota(jnp.int32, (n, n), 1)
    return col <= row


def _in_proj_kernel(x_ref, g_ref, w_hbm, fb_ref,
                    xm_ref, gm_ref, q_ref, k_ref, v_ref, gf_ref, c_ref,
                    wb_ref, wfb_ref, tri_ref, carry_ref, stage_ref, gstage_ref, sems,
                    *, tiles_per_seq, q_scale):
    t = pl.program_id(0)
    rows, d = x_ref.shape
    sub = tri_ref.shape[0]
    n_main = wb_ref.shape[1]
    n_gate = gstage_ref.shape[0]
    order = IN_PROJ_ORDER

    def unit_copy(i):
        return pltpu.make_async_copy(w_hbm.at[pl.ds(order[i] * d, d), :], stage_ref.at[i % 2],
                                     sems.at[i % 2])

    def gate_copy():
        return pltpu.make_async_copy(w_hbm.at[pl.ds(n_main, n_gate), :], gstage_ref, sems.at[2])

    def unit_load(i):
        unit_copy(i).wait()
        c0 = order[i] * d
        for r0 in range(0, d, IN_PROJ_STAGE_ROWS):
            r1 = r0 + IN_PROJ_STAGE_ROWS
            wb_ref[:, c0 + r0:c0 + r1] = stage_ref[i % 2, r0:r1, :].T.astype(BF16)
        if i + 2 < len(order):
            unit_copy(i + 2).start()

    @pl.when(t % tiles_per_seq == 0)
    def _():
        carry_ref[...] = jnp.zeros_like(carry_ref)

    def tile(first):
        load = unit_load if first else (lambda i: None)
        if first:
            gate_copy().start()
            unit_copy(0).start()
            unit_copy(1).start()
            tri_ref[...] = jnp.where(_tril_mask(sub), 1.0, 0.0).astype(BF16)

        starts = [sum(IN_PROJ_BLOCKS[:i]) for i in range(len(IN_PROJ_BLOCKS))]
        blocks = [slice(r0, r0 + n) for r0, n in zip(starts, IN_PROJ_BLOCKS)]
        hns = []
        for sl in blocks:
            x = x_ref[sl, :]
            ms = jnp.mean(x * x, axis=-1, keepdims=True)
            hns.append((x * lax.rsqrt(ms + EPS) * g_ref[...]).astype(BF16))

        if first:
            gate_copy().wait()
            w_gate = jnp.concatenate([gstage_ref[...], jnp.zeros((LANES - n_gate, d), F32)], axis=0)
            wfb_ref[...] = w_gate.T.astype(BF16)
            load(0)

        def proj(hn, idx):
            return jnp.dot(hn, wb_ref[:, idx * d:(idx + 1) * d], preferred_element_type=F32)

        load(1)
        for sl, hn in zip(blocks, hns):
            xm_ref[sl, :] = proj(hn, order[0]).astype(xm_ref.dtype)
        log_f = [_log_sigmoid(jnp.dot(hn, wfb_ref[...], preferred_element_type=F32) + fb_ref[...])
                 for hn in hns]
        load(2)
        for sl, hn in zip(blocks, hns):
            gm_ref[sl, :] = _silu(proj(hn, order[1])).astype(gm_ref.dtype)
        load(3)
        for sl, hn in zip(blocks, hns):
            gf_ref[sl, :] = _silu(proj(hn, order[2])).astype(gf_ref.dtype)
        carry = carry_ref[...]
        for sl, lf in zip(blocks, log_f):
            n = lf.shape[0]
            c = _cumsum_rows(tri_ref[0:n, 0:n], lf) + carry
            c_ref[:, sl] = (c * LOG2E).T[0:c_ref.shape[0], :]
            carry = c[n - 1:n, :]
        carry_ref[...] = carry
        load(4)
        for sl, hn in zip(blocks, hns):
            q_ref[sl, :] = (proj(hn, order[3]) * q_scale).astype(q_ref.dtype)
        load(5)
        for sl, hn in zip(blocks, hns):
            k_ref[sl, :] = proj(hn, order[4]).astype(k_ref.dtype)
        for sl, hn in zip(blocks, hns):
            v_ref[sl, :] = proj(hn, order[5]).astype(v_ref.dtype)

    @pl.when(t == 0)
    def _():
        tile(True)

    @pl.when(t != 0)
    def _():
        tile(False)


def _in_proj(x2, gain, w_in_t, f_bias, seq_len, q_scale):
    t_rows, d = x2.shape
    rows, sub = sum(IN_PROJ_BLOCKS), max(IN_PROJ_BLOCKS)
    assert t_rows % rows == 0 and seq_len % rows == 0
    n_main = len(IN_PROJ_ORDER) * d
    assert d % IN_PROJ_STAGE_ROWS == 0
    n_gate = w_in_t.shape[0] - n_main
    assert 0 < n_gate <= SUBLANES and w_in_t.shape[1] == d
    row_spec = pl.BlockSpec((rows, d), lambda t: (t, 0))
    const = lambda shape: pl.BlockSpec(shape, lambda t: (0, 0), pipeline_mode=pl.Buffered(1))
    kern = functools.partial(_in_proj_kernel, tiles_per_seq=seq_len // rows, q_scale=q_scale)
    return pl.pallas_call(
        kern,
        grid=(t_rows // rows,),
        in_specs=[row_spec, const((1, d)), pl.BlockSpec(memory_space=pl.ANY), const((1, LANES))],
        out_specs=[row_spec] * 6 + [pl.BlockSpec((SUBLANES, rows), lambda t: (0, t))],
        out_shape=[jax.ShapeDtypeStruct((t_rows, d), BF16)] * 6
        + [jax.ShapeDtypeStruct((SUBLANES, t_rows), F32)],
        scratch_shapes=[pltpu.VMEM((d, n_main), BF16), pltpu.VMEM((d, LANES), BF16),
                        pltpu.VMEM((sub, sub), BF16), pltpu.VMEM((1, LANES), F32),
                        pltpu.VMEM((2, d, d), F32), pltpu.VMEM((n_gate, d), F32),
                        pltpu.SemaphoreType.DMA((3,))],
        compiler_params=pltpu.CompilerParams(
            dimension_semantics=("arbitrary",), vmem_limit_bytes=VMEM_LIMIT),
        name="in_proj",
    )(x2, gain, w_in_t, f_bias)


def _mlstm_kernel(xm_ref, gm_ref, cw_ref, cb_ref, wq_ref, wk_ref, wv_ref, wg_ref, gb_ref,
                  ng_ref, sk_ref, y_ref,
                  tail_ref, c_sc, n_sc, m_sc, tri_ref, shift_ref, gw_ref, *, k_scale):
    ci = pl.program_id(1)
    n_seqs, L, d_m = xm_ref.shape
    n_heads, dh, _ = wq_ref.shape

    @pl.when(ci == 0)
    def _():
        for h in range(n_heads):
            sl = slice(h * dh, (h + 1) * dh)
            g_q = jnp.dot(wq_ref[h], wg_ref[h * dh:(h + 1) * dh, :], preferred_element_type=F32)
            g_k = jnp.dot(wk_ref[h], wg_ref[d_m + h * dh:d_m + (h + 1) * dh, :],
                          preferred_element_type=F32)
            g_v = jnp.dot(wv_ref[h], wg_ref[2 * d_m + h * dh:2 * d_m + (h + 1) * dh, :],
                          preferred_element_type=F32)
            gw_ref[0, sl, :] = (g_q + g_k * k_scale).astype(BF16)
            gw_ref[1, sl, :] = g_v.astype(BF16)
        c_sc[...] = jnp.zeros_like(c_sc)
        n_sc[...] = jnp.zeros_like(n_sc)
        m_sc[...] = jnp.zeros_like(m_sc)
        tail_ref[...] = jnp.zeros_like(tail_ref)
        row = lax.broadcasted_iota(jnp.int32, (L, L), 0)
        col = lax.broadcasted_iota(jnp.int32, (L, L), 1)
        tri_ref[...] = jnp.where(col <= row, 1.0, 0.0).astype(BF16)
        for j in range(1, CONV_WIDTH):
            shift_ref[j - 1] = jnp.where(col == row - j, 1.0, 0.0).astype(BF16)

    seqs = range(n_seqs)
    fronts = [_mlstm_front(xm_ref.at[i], cw_ref, cb_ref, wq_ref, wk_ref, wv_ref, gw_ref, gb_ref,
                           tail_ref.at[i], tri_ref, shift_ref, k_scale) for i in seqs]
    for i in seqs:
        mixed = _mlstm_mix(*fronts[i][1:], c_sc.at[i], n_sc.at[i], m_sc.at[i])
        _mlstm_out(mixed, fronts[i][0], gm_ref.at[i], ng_ref, sk_ref, y_ref.at[i])


def _mlstm_front(xm_ref, cw_ref, cb_ref, wq_ref, wk_ref, wv_ref, gw_ref, gb_ref,
                 tail_ref, tri_ref, shift_ref, k_scale):
    L, d_m = xm_ref.shape
    dh = wq_ref.shape[1]
    n_heads = wq_ref.shape[0]
    halo = SUBLANES

    x_b = xm_ref[...]
    x_f = x_b.astype(F32)
    conv = cb_ref[...] + x_f * cw_ref[CONV_WIDTH - 1:CONV_WIDTH, :]
    edge = jnp.concatenate([tail_ref[...], jnp.zeros((halo, d_m), F32)], axis=0)
    head = jnp.zeros((halo, d_m), F32)
    for j in range(1, CONV_WIDTH):
        w_j = cw_ref[CONV_WIDTH - 1 - j:CONV_WIDTH - j, :]
        conv = conv + jnp.dot(shift_ref[j - 1], x_b, preferred_element_type=F32) * w_j
        head = head + edge[halo - j:2 * halo - j, :] * w_j
    conv = jnp.concatenate([conv[:halo] + head, conv[halo:]], axis=0)
    tail_ref[...] = x_f[L - halo:L, :]
    xc = _silu(conv)
    xc_b = xc.astype(BF16)

    gates = (jnp.dot(xc_b, gw_ref[0], preferred_element_type=F32)
             + jnp.dot(x_b, gw_ref[1], preferred_element_type=F32) + gb_ref[...])
    qs, ks, vs = [], [], []
    for h in range(n_heads):
        sl = slice(h * dh, (h + 1) * dh)
        q = jnp.dot(xc_b[:, sl], wq_ref[h], preferred_element_type=F32)
        k = jnp.dot(xc_b[:, sl], wk_ref[h], preferred_element_type=F32) * k_scale
        v = jnp.dot(x_b[:, sl], wv_ref[h], preferred_element_type=F32)
        qs.append(q.astype(BF16))
        ks.append((k, k.astype(BF16)))
        vs.append(v.astype(BF16))

    ig = gates[:, :LANES]
    bcum = _cumsum_rows(tri_ref[...], _log_sigmoid(gates[:, LANES:]))
    a_col = ig - bcum
    return xc, qs, ks, vs, a_col, bcum


def _mlstm_mix(qs, ks, vs, a_col, bcum, c_sc, n_sc, m_sc):
    n_heads = len(qs)
    L, dh = qs[0].shape
    a_rows = a_col.T
    causal = _tril_mask(L)
    ones_blk = jnp.ones((L, LANES), BF16)

    heads = range(n_heads)
    nt = (((1,), (1,)), ((), ()))
    m_prev = {h: m_sc[h][:, 0:1] for h in heads}
    dm = {h: jnp.where(causal, a_rows[h:h + 1, :], NEG_INF) for h in heads}
    m_run = {h: jnp.maximum(m_prev[h], jnp.max(dm[h], axis=1, keepdims=True)) for h in heads}
    w_inter = {h: jnp.exp(m_prev[h] - m_run[h]) for h in heads}
    s_b = {h: (lax.dot_general(qs[h], ks[h][1], nt, preferred_element_type=F32)
               * jnp.exp(dm[h] - m_run[h])).astype(BF16) for h in heads}
    num = {h: w_inter[h] * jnp.dot(qs[h], c_sc[h].astype(BF16), preferred_element_type=F32)
           + jnp.dot(s_b[h], vs[h], preferred_element_type=F32) for h in heads}
    qn = {h: lax.dot_general(qs[h], jnp.broadcast_to(n_sc[h], (LANES, dh)).astype(BF16), nt,
                             preferred_element_type=F32) for h in heads}
    den = {h: w_inter[h] * qn[h] + jnp.dot(s_b[h], ones_blk, preferred_element_type=F32)
           for h in heads}
    m_t = {h: bcum[:, h:h + 1] + m_run[h] for h in heads}
    inv = {h: 1.0 / jnp.maximum(jnp.abs(den[h]), jnp.exp(-m_t[h])) for h in heads}
    hh = [num[h] * jnp.concatenate([inv[h]] * (dh // LANES), axis=1) for h in heads]

    for h in heads:
        m_last = m_run[h][L - 1:L, :]
        decay = jnp.exp(m_prev[h] - m_last)
        kw = ks[h][0] * jnp.exp(a_col[:, h:h + 1] - m_last)
        c_sc[h] = decay * c_sc[h] + lax.dot_general(
            kw.astype(BF16), vs[h], (((0,), (0,)), ((), ())), preferred_element_type=F32)
        n_sc[h] = decay * n_sc[h] + jnp.sum(kw, axis=0, keepdims=True)
        m_sc[h] = jnp.broadcast_to(m_t[h][L - 1:L, :], m_sc.shape[1:])
    return hh


def _mlstm_out(hh, xc, gm_ref, ng_ref, sk_ref, y_ref):
    heads = range(len(hh))
    dh = hh[0].shape[1]
    mu = [jnp.mean(hh[h], axis=1, keepdims=True) for h in heads]
    cen = [hh[h] - mu[h] for h in heads]
    var = [jnp.mean(cen[h] * cen[h], axis=1, keepdims=True) for h in heads]
    for h in heads:
        sl = slice(h * dh, (h + 1) * dh)
        hn = cen[h] * lax.rsqrt(var[h] + EPS) * ng_ref[:, sl] + sk_ref[:, sl] * xc[:, sl]
        y_ref[:, sl] = (hn * gm_ref[:, sl].astype(F32)).astype(y_ref.dtype)


def _mlstm(xm, gm, conv_w, conv_b, wq, wk, wv, wg, gb, ngain, skip):
    b, s, d = xm.shape
    L = MLSTM_CHUNK
    assert s % L == 0
    n_heads, dh, _ = wq.shape
    n_seqs = MLSTM_SEQS
    assert b % n_seqs == 0
    seq_spec = pl.BlockSpec((n_seqs, L, d), lambda bi, ci: (bi, ci, 0))

    def const(shape):
        nd = len(shape)
        return pl.BlockSpec(shape, lambda bi, ci: (0,) * nd)

    kern = functools.partial(_mlstm_kernel, k_scale=dh ** -0.5)
    return pl.pallas_call(
        kern,
        grid=(b // n_seqs, s // L),
        in_specs=[seq_spec, seq_spec, const(conv_w.shape), const(conv_b.shape), const(wq.shape),
                  const(wk.shape), const(wv.shape), const(wg.shape), const(gb.shape),
                  const(ngain.shape), const(skip.shape)],
        out_specs=seq_spec,
        out_shape=jax.ShapeDtypeStruct((b, s, d), BF16),
        scratch_shapes=[
            pltpu.VMEM((n_seqs, SUBLANES, d), F32),
            pltpu.VMEM((n_seqs, n_heads, dh, dh), F32),
            pltpu.VMEM((n_seqs, n_heads, 1, dh), F32),
            pltpu.VMEM((n_seqs, n_heads, 1, LANES), F32),
            pltpu.VMEM((L, L), BF16),
            pltpu.VMEM((CONV_WIDTH - 1, L, L), BF16),
            pltpu.VMEM((2, d, wg.shape[1]), BF16),
        ],
        compiler_params=pltpu.CompilerParams(
            dimension_semantics=("arbitrary", "arbitrary"), vmem_limit_bytes=VMEM_LIMIT),
        name="mlstm",
    )(xm, gm, conv_w, conv_b, wq, wk, wv, wg, gb, ngain, skip)


def _fox_kernel(q_ref, k_ref, v_ref, cr_ref, gf_ref, g_ref, o_ref):
    s_len = q_ref.shape[0]
    dh = LANES
    n_local = q_ref.shape[1] // dh
    tq = FOX_Q
    causal = _tril_mask(tq)
    head0 = pl.program_id(1) * n_local

    def scores(unit):
        hl, qi = unit
        r0 = qi * tq
        cols = slice(hl * dh, (hl + 1) * dh)
        return lax.dot_general(q_ref[r0:r0 + tq, cols], k_ref[0:r0 + tq, cols],
                               (((1,), (1,)), ((), ())), preferred_element_type=F32)

    order = [(hl, qi) for hl in range(n_local) for qi in reversed(range(s_len // tq))]
    ahead = [scores(u) for u in order[:FOX_LOOKAHEAD]]
    for pos, (hl, qi) in enumerate(order):
        r0 = qi * tq
        kv = r0 + tq
        cols = slice(hl * dh, (hl + 1) * dh)
        s = ahead.pop(0)
        if pos + FOX_LOOKAHEAD < len(order):
            ahead.append(scores(order[pos + FOX_LOOKAHEAD]))
        s = s - cr_ref[pl.ds(head0 + hl, 1), 0:kv]
        s_diag = jnp.where(causal, s[:, r0:], NEG_INF)
        s = s_diag if qi == 0 else jnp.concatenate([s[:, :r0], s_diag], axis=1)
        m = jnp.max(s, axis=1, keepdims=True)
        p = jnp.exp2(s - m).astype(BF16)
        v_ext = jnp.concatenate([v_ref[0:kv, cols], jnp.ones((kv, dh), BF16)], axis=1)
        acc = jnp.dot(p, v_ext, preferred_element_type=F32)

        out = acc[:, :dh] / acc[:, dh:]
        ms = jnp.mean(out * out, axis=1, keepdims=True)
        o_ref[r0:kv, cols] = (out * lax.rsqrt(ms + EPS) * g_ref[:, cols]
                              * gf_ref[r0:kv, cols].astype(F32)).astype(o_ref.dtype)


def _fox(q, k, v, c_row, gf, gain):
    b, s, d = q.shape
    n_heads = N_FOX_HEADS
    dh = d // n_heads
    assert dh == LANES and s % FOX_Q == 0 and n_heads % FOX_HEADS == 0
    width = FOX_HEADS * dh
    head_spec = pl.BlockSpec((None, s, width), lambda bi, hi: (bi, 0, hi))
    return pl.pallas_call(
        _fox_kernel,
        grid=(b, n_heads // FOX_HEADS),
        in_specs=[head_spec, head_spec, head_spec,
                  pl.BlockSpec((c_row.shape[0], s), lambda bi, hi: (0, bi)),
                  head_spec,
                  pl.BlockSpec((1, width), lambda bi, hi: (0, hi))],
        out_specs=head_spec,
        out_shape=jax.ShapeDtypeStruct((b, s, d), BF16),
        compiler_params=pltpu.CompilerParams(
            dimension_semantics=("arbitrary", "arbitrary"), vmem_limit_bytes=VMEM_LIMIT),
        name="fox",
    )(q, k, v, c_row, gf, gain)


def _out_proj_kernel(ym_ref, yf_ref, x_ref, w_ref, g_ref, o_ref, wb_ref, *, final_norm):
    d_m = ym_ref.shape[1]

    @pl.when(pl.program_id(0) == 0)
    def _():
        wb_ref[...] = w_ref[...].astype(BF16)

    rows = x_ref.shape[0]
    for r0 in range(0, rows, OUT_PROJ_SUB):
        sl = slice(r0, r0 + OUT_PROJ_SUB)
        y = jnp.dot(ym_ref[sl, :], wb_ref[0:d_m, :], preferred_element_type=F32)
        y = y + jnp.dot(yf_ref[sl, :], wb_ref[d_m:, :], preferred_element_type=F32)
        r = x_ref[sl, :] + y
        if final_norm:
            ms = jnp.mean(r * r, axis=-1, keepdims=True)
            r = r * lax.rsqrt(ms + EPS) * g_ref[...]
        o_ref[sl, :] = r


def _out_proj(ym, yf, x2, w_out, gain, final_norm):
    t_rows, d = x2.shape
    rows = OUT_PROJ_ROWS
    assert t_rows % rows == 0
    row = lambda width: pl.BlockSpec((rows, width), lambda t: (t, 0))
    const = lambda shape: pl.BlockSpec(shape, lambda t: (0, 0), pipeline_mode=pl.Buffered(1))
    kern = functools.partial(_out_proj_kernel, final_norm=final_norm)
    return pl.pallas_call(
        kern,
        grid=(t_rows // rows,),
        in_specs=[row(ym.shape[1]), row(yf.shape[1]), row(d), const(w_out.shape), const((1, d))],
        out_specs=row(d),
        out_shape=jax.ShapeDtypeStruct((t_rows, d), F32),
        scratch_shapes=[pltpu.VMEM(w_out.shape, BF16)],
        compiler_params=pltpu.CompilerParams(
            dimension_semantics=("arbitrary",), vmem_limit_bytes=VMEM_LIMIT),
        name="out_proj",
    )(ym, yf, x2, w_out, gain)


def _block_diag(w, dh):
    nb, bs, _ = w.shape
    d = nb * bs
    tiled = jnp.broadcast_to(w.reshape(d, 1, bs), (d, dh // bs, bs)).reshape(d, dh)
    r = lax.broadcasted_iota(jnp.int32, (d, dh), 0)
    c = lax.broadcasted_iota(jnp.int32, (d, dh), 1)
    keep = (r % dh) // bs == c // bs
    return jnp.where(keep, tiled, 0.0).reshape(d // dh, dh, dh)


def kernel(x, norm_gain, w_in, conv_w, conv_b, w_q_m, w_k_m, w_v_m, w_igate, b_igate, w_fgate,
           b_fgate, mlstm_norm_gain, mlstm_skip, fox_forget_bias, fox_norm_gain, w_out,
           final_norm_gain):
    b, s, d = x.shape
    depth = norm_gain.shape[0]
    d_m = conv_w.shape[-1]
    d_f = fox_norm_gain.shape[-1]
    dh_m = d_m // N_MLSTM_HEADS
    dh_f = d_f // N_FOX_HEADS
    n_main = 2 * d_m + 4 * d_f
    assert d_m == d and d_f == d

    x2 = x.reshape(b * s, d)
    for l in range(depth):
        assert w_in.shape[2] == n_main + N_FOX_HEADS
        f_bias = jnp.pad(fox_forget_bias[l], (0, LANES - N_FOX_HEADS)).reshape(1, LANES)
        xm, gm, qf, kf, vf, gf, c_keys = _in_proj(
            x2, norm_gain[l].reshape(1, d), w_in[l].T, f_bias, s, dh_f ** -0.5 * LOG2E)

        lane_pad = ((0, 0), (0, LANES - N_MLSTM_HEADS))
        wg = jnp.concatenate([jnp.pad(w_igate[l], lane_pad), jnp.pad(w_fgate[l], lane_pad)], axis=1)
        gb = jnp.concatenate([jnp.pad(b_igate[l][None], lane_pad),
                              jnp.pad(b_fgate[l][None], lane_pad)], axis=1)
        y_m = _mlstm(xm.reshape(b, s, d_m), gm.reshape(b, s, d_m), conv_w[l],
                     conv_b[l].reshape(1, d_m),
                     _block_diag(w_q_m[l], dh_m).astype(BF16),
                     _block_diag(w_k_m[l], dh_m).astype(BF16),
                     _block_diag(w_v_m[l], dh_m).astype(BF16),
                     wg.astype(BF16), gb,
                     mlstm_norm_gain[l].reshape(1, d_m), mlstm_skip[l].reshape(1, d_m))

        y_f = _fox(qf.reshape(b, s, d_f), kf.reshape(b, s, d_f), vf.reshape(b, s, d_f), c_keys,
                   gf.reshape(b, s, d_f), fox_norm_gain[l].reshape(1, d_f))

        x2 = _out_proj(y_m.reshape(b * s, d_m), y_f.reshape(b * s, d_f), x2, w_out[l],
                       final_norm_gain.reshape(1, d), final_norm=(l == depth - 1))
    return x2.reshape(b, s, d)
```

```python
import functools
import math

import jax
import jax.numpy as jnp
from jax import lax
from jax.experimental import pallas as pl
from jax.experimental.pallas import tpu as pltpu

EPS = 1e-6
N_MLSTM_HEADS = 4
N_FOX_HEADS = 8
CONV_WIDTH = 4

LANES = 128
SUBLANES = 8
VMEM_LIMIT = 56 * 1024 * 1024

IN_PROJ_BLOCKS = (256, 256, 256, 256)
IN_PROJ_STAGE_ROWS = 512
OUT_PROJ_ROWS = 1024
OUT_PROJ_SUB = 256
MLSTM_CHUNK = 256
MLSTM_SEQS = 2
FOX_HEADS = 2
FOX_LOOKAHEAD = 2
FOX_Q = 256
LOG2E = math.log2(math.e)

BF16 = jnp.bfloat16
F32 = jnp.float32
NEG_INF = float("-inf")


def _log_sigmoid(z):
    return jnp.minimum(z, 0.0) - jnp.log(1.0 + jnp.exp(-jnp.abs(z)))


def _sigmoid(z):
    return 1.0 / (1.0 + jnp.exp(-z))


def _silu(z):
    return z * _sigmoid(z)


def _split3(v):
    hi = v.astype(BF16)
    r = v - hi.astype(F32)
    mid = r.astype(BF16)
    lo = (r - mid.astype(F32)).astype(BF16)
    return hi, mid, lo


def _cumsum_rows(tri, v):
    w = v.shape[1]
    cat = jnp.concatenate(_split3(v), axis=1)
    cs = jnp.dot(tri, cat, preferred_element_type=F32)
    return cs[:, :w] + cs[:, w:2 * w] + cs[:, 2 * w:]


def _tril_mask(n):
    row = lax.broadcasted_iota(jnp.int32, (n, n), 0)
    col = lax.broadcasted_iota(jnp.int32, (n, n), 1)
    return col <= row


def _in_proj_kernel(x_ref, g_ref, w_hbm, fb_ref,
                    xm_ref, gm_ref, q_ref, k_ref, v_ref, gf_ref, c_ref,
                    wb_ref, wfb_ref, tri_ref, carry_ref, stage_ref, gstage_ref, sems,
                    *, tiles_per_seq, q_scale):
    t = pl.program_id(0)
    rows, d = x_ref.shape
    sub = tri_ref.shape[0]
    n_main = wb_ref.shape[1]
    n_gate = gstage_ref.shape[0]
    stage_rows = stage_ref.shape[1]
    n_chunks = n_main // stage_rows

    def chunk_copy(c):
        return pltpu.make_async_copy(w_hbm.at[pl.ds(c * stage_rows, stage_rows), :], stage_ref.at[c % 2],
                                     sems.at[c % 2])

    def gate_copy():
        return pltpu.make_async_copy(w_hbm.at[pl.ds(n_main, n_gate), :], gstage_ref, sems.at[2])

    @pl.when(t == 0)
    def _():
        gate_copy().start()
        chunk_copy(0).start()
        tri_ref[...] = jnp.where(_tril_mask(sub), 1.0, 0.0).astype(BF16)
        for c in range(n_chunks):
            if c + 1 < n_chunks:
                chunk_copy(c + 1).start()
            chunk_copy(c).wait()
            wb_ref[:, c * stage_rows:(c + 1) * stage_rows] = stage_ref[c % 2].T.astype(BF16)
        gate_copy().wait()
        w_gate = jnp.concatenate([gstage_ref[...], jnp.zeros((LANES - n_gate, d), F32)], axis=0)
        wfb_ref[...] = w_gate.T.astype(BF16)

    @pl.when(t % tiles_per_seq == 0)
    def _():
        carry_ref[...] = jnp.zeros_like(carry_ref)

    starts = [sum(IN_PROJ_BLOCKS[:i]) for i in range(len(IN_PROJ_BLOCKS))]
    blocks = [slice(r0, r0 + n) for r0, n in zip(starts, IN_PROJ_BLOCKS)]
    hns = []
    for sl in blocks:
        x = x_ref[sl, :]
        ms = jnp.mean(x * x, axis=-1, keepdims=True)
        hns.append((x * lax.rsqrt(ms + EPS) * g_ref[...]).astype(BF16))

    def proj(hn, idx):
        return jnp.dot(hn, wb_ref[:, idx * d:(idx + 1) * d], preferred_element_type=F32)

    for sl, hn in zip(blocks, hns):
        xm_ref[sl, :] = proj(hn, 0).astype(xm_ref.dtype)
    log_f = [_log_sigmoid(jnp.dot(hn, wfb_ref[...], preferred_element_type=F32) + fb_ref[...])
             for hn in hns]
    for sl, hn in zip(blocks, hns):
        gm_ref[sl, :] = _silu(proj(hn, 1)).astype(gm_ref.dtype)
    for sl, hn in zip(blocks, hns):
        gf_ref[sl, :] = _silu(proj(hn, 5)).astype(gf_ref.dtype)
    carry = carry_ref[...]
    for sl, lf in zip(blocks, log_f):
        n = lf.shape[0]
        c = _cumsum_rows(tri_ref[0:n, 0:n], lf) + carry
        c_ref[:, sl] = (c * LOG2E).T[0:c_ref.shape[0], :]
        carry = c[n - 1:n, :]
    carry_ref[...] = carry
    for sl, hn in zip(blocks, hns):
        q_ref[sl, :] = (proj(hn, 2) * q_scale).astype(q_ref.dtype)
    for sl, hn in zip(blocks, hns):
        k_ref[sl, :] = proj(hn, 3).astype(k_ref.dtype)
    for sl, hn in zip(blocks, hns):
        v_ref[sl, :] = proj(hn, 4).astype(v_ref.dtype)


def _in_proj(x2, gain, w_in_t, f_bias, seq_len, q_scale):
    t_rows, d = x2.shape
    rows, sub = sum(IN_PROJ_BLOCKS), max(IN_PROJ_BLOCKS)
    assert t_rows % rows == 0 and seq_len % rows == 0
    n_main = 6 * d
    n_gate = w_in_t.shape[0] - n_main
    assert 0 < n_gate <= SUBLANES and w_in_t.shape[1] == d
    row_spec = pl.BlockSpec((rows, d), lambda t: (t, 0))
    const = lambda shape: pl.BlockSpec(shape, lambda t: (0, 0), pipeline_mode=pl.Buffered(1))
    kern = functools.partial(_in_proj_kernel, tiles_per_seq=seq_len // rows, q_scale=q_scale)
    return pl.pallas_call(
        kern,
        grid=(t_rows // rows,),
        in_specs=[row_spec, const((1, d)), pl.BlockSpec(memory_space=pl.ANY), const((1, LANES))],
        out_specs=[row_spec] * 6 + [pl.BlockSpec((SUBLANES, rows), lambda t: (0, t))],
        out_shape=[jax.ShapeDtypeStruct((t_rows, d), BF16)] * 6
        + [jax.ShapeDtypeStruct((SUBLANES, t_rows), F32)],
        scratch_shapes=[pltpu.VMEM((d, n_main), BF16), pltpu.VMEM((d, LANES), BF16),
                        pltpu.VMEM((sub, sub), BF16), pltpu.VMEM((1, LANES), F32),
                        pltpu.VMEM((2, IN_PROJ_STAGE_ROWS, d), F32), pltpu.VMEM((n_gate, d), F32),
                        pltpu.SemaphoreType.DMA((3,))],
        compiler_params=pltpu.CompilerParams(
            dimension_semantics=("arbitrary",), vmem_limit_bytes=VMEM_LIMIT),
        name="in_proj",
    )(x2, gain, w_in_t, f_bias)


def _mlstm_kernel(xm_ref, gm_ref, cw_ref, cb_ref, wq_ref, wk_ref, wv_ref, wg_ref, gb_ref,
                  ng_ref, sk_ref, y_ref,
                  tail_ref, c_sc, n_sc, m_sc, tri_ref, shift_ref, gw_ref, *, k_scale):
    ci = pl.program_id(1)
    n_seqs, L, d_m = xm_ref.shape
    n_heads, dh, _ = wq_ref.shape

    @pl.when(ci == 0)
    def _():
        for h in range(n_heads):
            sl = slice(h * dh, (h + 1) * dh)
            g_q = jnp.dot(wq_ref[h], wg_ref[h * dh:(h + 1) * dh, :], preferred_element_type=F32)
            g_k = jnp.dot(wk_ref[h], wg_ref[d_m + h * dh:d_m + (h + 1) * dh, :],
                          preferred_element_type=F32)
            g_v = jnp.dot(wv_ref[h], wg_ref[2 * d_m + h * dh:2 * d_m + (h + 1) * dh, :],
                          preferred_element_type=F32)
            gw_ref[0, sl, :] = (g_q + g_k * k_scale).astype(BF16)
            gw_ref[1, sl, :] = g_v.astype(BF16)
        c_sc[...] = jnp.zeros_like(c_sc)
        n_sc[...] = jnp.zeros_like(n_sc)
        m_sc[...] = jnp.zeros_like(m_sc)
        tail_ref[...] = jnp.zeros_like(tail_ref)
        row = lax.broadcasted_iota(jnp.int32, (L, L), 0)
        col = lax.broadcasted_iota(jnp.int32, (L, L), 1)
        tri_ref[...] = jnp.where(col <= row, 1.0, 0.0).astype(BF16)
        for j in range(1, CONV_WIDTH):
            shift_ref[j - 1] = jnp.where(col == row - j, 1.0, 0.0).astype(BF16)

    seqs = range(n_seqs)
    fronts = [_mlstm_front(xm_ref.at[i], cw_ref, cb_ref, wq_ref, wk_ref, wv_ref, gw_ref, gb_ref,
                           tail_ref.at[i], tri_ref, shift_ref, k_scale) for i in seqs]
    for i in seqs:
        mixed = _mlstm_mix(*fronts[i][1:], c_sc.at[i], n_sc.at[i], m_sc.at[i])
        _mlstm_out(mixed, fronts[i][0], gm_ref.at[i], ng_ref, sk_ref, y_ref.at[i])


def _mlstm_front(xm_ref, cw_ref, cb_ref, wq_ref, wk_ref, wv_ref, gw_ref, gb_ref,
                 tail_ref, tri_ref, shift_ref, k_scale):
    L, d_m = xm_ref.shape
    dh = wq_ref.shape[1]
    n_heads = wq_ref.shape[0]
    halo = SUBLANES

    x_b = xm_ref[...]
    x_f = x_b.astype(F32)
    conv = cb_ref[...] + x_f * cw_ref[CONV_WIDTH - 1:CONV_WIDTH, :]
    edge = jnp.concatenate([tail_ref[...], jnp.zeros((halo, d_m), F32)], axis=0)
    head = jnp.zeros((halo, d_m), F32)
    for j in range(1, CONV_WIDTH):
        w_j = cw_ref[CONV_WIDTH - 1 - j:CONV_WIDTH - j, :]
        conv = conv + jnp.dot(shift_ref[j - 1], x_b, preferred_element_type=F32) * w_j
        head = head + edge[halo - j:2 * halo - j, :] * w_j
    conv = jnp.concatenate([conv[:halo] + head, conv[halo:]], axis=0)
    tail_ref[...] = x_f[L - halo:L, :]
    xc = _silu(conv)
    xc_b = xc.astype(BF16)

    gates = (jnp.dot(xc_b, gw_ref[0], preferred_element_type=F32)
             + jnp.dot(x_b, gw_ref[1], preferred_element_type=F32) + gb_ref[...])
    qs, ks, vs = [], [], []
    for h in range(n_heads):
        sl = slice(h * dh, (h + 1) * dh)
        q = jnp.dot(xc_b[:, sl], wq_ref[h], preferred_element_type=F32)
        k = jnp.dot(xc_b[:, sl], wk_ref[h], preferred_element_type=F32) * k_scale
        v = jnp.dot(x_b[:, sl], wv_ref[h], preferred_element_type=F32)
        qs.append(q.astype(BF16))
        ks.append((k, k.astype(BF16)))
        vs.append(v.astype(BF16))

    ig = gates[:, :LANES]
    bcum = _cumsum_rows(tri_ref[...], _log_sigmoid(gates[:, LANES:]))
    a_col = ig - bcum
    return xc, qs, ks, vs, a_col, bcum


def _mlstm_mix(qs, ks, vs, a_col, bcum, c_sc, n_sc, m_sc):
    n_heads = len(qs)
    L, dh = qs[0].shape
    a_rows = a_col.T
    causal = _tril_mask(L)
    ones_blk = jnp.ones((L, LANES), BF16)

    heads = range(n_heads)
    nt = (((1,), (1,)), ((), ()))
    m_prev = {h: m_sc[h][:, 0:1] for h in heads}
    dm = {h: jnp.where(causal, a_rows[h:h + 1, :], NEG_INF) for h in heads}
    m_run = {h: jnp.maximum(m_prev[h], jnp.max(dm[h], axis=1, keepdims=True)) for h in heads}
    w_inter = {h: jnp.exp(m_prev[h] - m_run[h]) for h in heads}
    s_b = {h: (lax.dot_general(qs[h], ks[h][1], nt, preferred_element_type=F32)
               * jnp.exp(dm[h] - m_run[h])).astype(BF16) for h in heads}
    num = {h: w_inter[h] * jnp.dot(qs[h], c_sc[h].astype(BF16), preferred_element_type=F32)
           + jnp.dot(s_b[h], vs[h], preferred_element_type=F32) for h in heads}
    qn = {h: lax.dot_general(qs[h], jnp.broadcast_to(n_sc[h], (LANES, dh)).astype(BF16), nt,
                             preferred_element_type=F32) for h in heads}
    den = {h: w_inter[h] * qn[h] + jnp.dot(s_b[h], ones_blk, preferred_element_type=F32)
           for h in heads}
    m_t = {h: bcum[:, h:h + 1] + m_run[h] for h in heads}
    inv = {h: 1.0 / jnp.maximum(jnp.abs(den[h]), jnp.exp(-m_t[h])) for h in heads}
    hh = [num[h] * jnp.concatenate([inv[h]] * (dh // LANES), axis=1) for h in heads]

    for h in heads:
        m_last = m_run[h][L - 1:L, :]
        decay = jnp.exp(m_prev[h] - m_last)
        kw = ks[h][0] * jnp.exp(a_col[:, h:h + 1] - m_last)
        c_sc[h] = decay * c_sc[h] + lax.dot_general(
            kw.astype(BF16), vs[h], (((0,), (0,)), ((), ())), preferred_element_type=F32)
        n_sc[h] = decay * n_sc[h] + jnp.sum(kw, axis=0, keepdims=True)
        m_sc[h] = jnp.broadcast_to(m_t[h][L - 1:L, :], m_sc.shape[1:])
    return hh


def _mlstm_out(hh, xc, gm_ref, ng_ref, sk_ref, y_ref):
    heads = range(len(hh))
    dh = hh[0].shape[1]
    mu = [jnp.mean(hh[h], axis=1, keepdims=True) for h in heads]
    cen = [hh[h] - mu[h] for h in heads]
    var = [jnp.mean(cen[h] * cen[h], axis=1, keepdims=True) for h in heads]
    for h in heads:
        sl = slice(h * dh, (h + 1) * dh)
        hn = cen[h] * lax.rsqrt(var[h] + EPS) * ng_ref[:, sl] + sk_ref[:, sl] * xc[:, sl]
        y_ref[:, sl] = (hn * gm_ref[:, sl].astype(F32)).astype(y_ref.dtype)


def _mlstm(xm, gm, conv_w, conv_b, wq, wk, wv, wg, gb, ngain, skip):
    b, s, d = xm.shape
    L = MLSTM_CHUNK
    assert s % L == 0
    n_heads, dh, _ = wq.shape
    n_seqs = MLSTM_SEQS
    assert b % n_seqs == 0
    seq_spec = pl.BlockSpec((n_seqs, L, d), lambda bi, ci: (bi, ci, 0))

    def const(shape):
        nd = len(shape)
        return pl.BlockSpec(shape, lambda bi, ci: (0,) * nd)

    kern = functools.partial(_mlstm_kernel, k_scale=dh ** -0.5)
    return pl.pallas_call(
        kern,
        grid=(b // n_seqs, s // L),
        in_specs=[seq_spec, seq_spec, const(conv_w.shape), const(conv_b.shape), const(wq.shape),
                  const(wk.shape), const(wv.shape), const(wg.shape), const(gb.shape),
                  const(ngain.shape), const(skip.shape)],
        out_specs=seq_spec,
        out_shape=jax.ShapeDtypeStruct((b, s, d), BF16),
        scratch_shapes=[
            pltpu.VMEM((n_seqs, SUBLANES, d), F32),
            pltpu.VMEM((n_seqs, n_heads, dh, dh), F32),
            pltpu.VMEM((n_seqs, n_heads, 1, dh), F32),
            pltpu.VMEM((n_seqs, n_heads, 1, LANES), F32),
            pltpu.VMEM((L, L), BF16),
            pltpu.VMEM((CONV_WIDTH - 1, L, L), BF16),
            pltpu.VMEM((2, d, wg.shape[1]), BF16),
        ],
        compiler_params=pltpu.CompilerParams(
            dimension_semantics=("arbitrary", "arbitrary"), vmem_limit_bytes=VMEM_LIMIT),
        name="mlstm",
    )(xm, gm, conv_w, conv_b, wq, wk, wv, wg, gb, ngain, skip)


def _fox_kernel(q_ref, k_ref, v_ref, cr_ref, gf_ref, g_ref, o_ref):
    s_len = q_ref.shape[0]
    dh = LANES
    n_local = q_ref.shape[1] // dh
    tq = FOX_Q
    causal = _tril_mask(tq)
    head0 = pl.program_id(1) * n_local

    def scores(unit):
        hl, qi = unit
        r0 = qi * tq
        cols = slice(hl * dh, (hl + 1) * dh)
        return lax.dot_general(q_ref[r0:r0 + tq, cols], k_ref[0:r0 + tq, cols],
                               (((1,), (1,)), ((), ())), preferred_element_type=F32)

    order = [(hl, qi) for hl in range(n_local) for qi in reversed(range(s_len // tq))]
    ahead = [scores(u) for u in order[:FOX_LOOKAHEAD]]
    for pos, (hl, qi) in enumerate(order):
        r0 = qi * tq
        kv = r0 + tq
        cols = slice(hl * dh, (hl + 1) * dh)
        s = ahead.pop(0)
        if pos + FOX_LOOKAHEAD < len(order):
            ahead.append(scores(order[pos + FOX_LOOKAHEAD]))
        s = s - cr_ref[pl.ds(head0 + hl, 1), 0:kv]
        s_diag = jnp.where(causal, s[:, r0:], NEG_INF)
        s = s_diag if qi == 0 else jnp.concatenate([s[:, :r0], s_diag], axis=1)
        m = jnp.max(s, axis=1, keepdims=True)
        p = jnp.exp2(s - m).astype(BF16)
        v_ext = jnp.concatenate([v_ref[0:kv, cols], jnp.ones((kv, dh), BF16)], axis=1)
        acc = jnp.dot(p, v_ext, preferred_element_type=F32)

        out = acc[:, :dh] / acc[:, dh:]
        ms = jnp.mean(out * out, axis=1, keepdims=True)
        o_ref[r0:kv, cols] = (out * lax.rsqrt(ms + EPS) * g_ref[:, cols]
                              * gf_ref[r0:kv, cols].astype(F32)).astype(o_ref.dtype)


def _fox(q, k, v, c_row, gf, gain):
    b, s, d = q.shape
    n_heads = N_FOX_HEADS
    dh = d // n_heads
    assert dh == LANES and s % FOX_Q == 0 and n_heads % FOX_HEADS == 0
    width = FOX_HEADS * dh
    head_spec = pl.BlockSpec((None, s, width), lambda bi, hi: (bi, 0, hi))
    return pl.pallas_call(
        _fox_kernel,
        grid=(b, n_heads // FOX_HEADS),
        in_specs=[head_spec, head_spec, head_spec,
                  pl.BlockSpec((c_row.shape[0], s), lambda bi, hi: (0, bi)),
                  head_spec,
                  pl.BlockSpec((1, width), lambda bi, hi: (0, hi))],
        out_specs=head_spec,
        out_shape=jax.ShapeDtypeStruct((b, s, d), BF16),
        compiler_params=pltpu.CompilerParams(
            dimension_semantics=("arbitrary", "arbitrary"), vmem_limit_bytes=VMEM_LIMIT),
        name="fox",
    )(q, k, v, c_row, gf, gain)


def _out_proj_kernel(ym_ref, yf_ref, x_ref, w_ref, g_ref, o_ref, wb_ref, *, final_norm):
    d_m = ym_ref.shape[1]

    @pl.when(pl.program_id(0) == 0)
    def _():
        wb_ref[...] = w_ref[...].astype(BF16)

    rows = x_ref.shape[0]
    for r0 in range(0, rows, OUT_PROJ_SUB):
        sl = slice(r0, r0 + OUT_PROJ_SUB)
        y = jnp.dot(ym_ref[sl, :], wb_ref[0:d_m, :], preferred_element_type=F32)
        y = y + jnp.dot(yf_ref[sl, :], wb_ref[d_m:, :], preferred_element_type=F32)
        r = x_ref[sl, :] + y
        if final_norm:
            ms = jnp.mean(r * r, axis=-1, keepdims=True)
            r = r * lax.rsqrt(ms + EPS) * g_ref[...]
        o_ref[sl, :] = r


def _out_proj(ym, yf, x2, w_out, gain, final_norm):
    t_rows, d = x2.shape
    rows = OUT_PROJ_ROWS
    assert t_rows % rows == 0
    row = lambda width: pl.BlockSpec((rows, width), lambda t: (t, 0))
    const = lambda shape: pl.BlockSpec(shape, lambda t: (0, 0), pipeline_mode=pl.Buffered(1))
    kern = functools.partial(_out_proj_kernel, final_norm=final_norm)
    return pl.pallas_call(
        kern,
        grid=(t_rows // rows,),
        in_specs=[row(ym.shape[1]), row(yf.shape[1]), row(d), const(w_out.shape), const((1, d))],
        out_specs=row(d),
        out_shape=jax.ShapeDtypeStruct((t_rows, d), F32),
        scratch_shapes=[pltpu.VMEM(w_out.shape, BF16)],
        compiler_params=pltpu.CompilerParams(
            dimension_semantics=("arbitrary",), vmem_limit_bytes=VMEM_LIMIT),
        name="out_proj",
    )(ym, yf, x2, w_out, gain)


def _block_diag(w, dh):
    nb, bs, _ = w.shape
    n_heads = nb * bs // dh
    rows = w.reshape(n_heads, dh, bs)
    r = lax.broadcasted_iota(jnp.int32, (n_heads, dh, dh), 1)
    c = lax.broadcasted_iota(jnp.int32, (n_heads, dh, dh), 2)
    out = jnp.zeros((n_heads, dh, dh), w.dtype)
    for b in range(bs):
        out = jnp.where((r // bs == c // bs) & (c % bs == b), rows[:, :, b:b + 1], out)
    return out


def kernel(x, norm_gain, w_in, conv_w, conv_b, w_q_m, w_k_m, w_v_m, w_igate, b_igate, w_fgate,
           b_fgate, mlstm_norm_gain, mlstm_skip, fox_forget_bias, fox_norm_gain, w_out,
           final_norm_gain):
    b, s, d = x.shape
    depth = norm_gain.shape[0]
    d_m = conv_w.shape[-1]
    d_f = fox_norm_gain.shape[-1]
    dh_m = d_m // N_MLSTM_HEADS
    dh_f = d_f // N_FOX_HEADS
    n_main = 2 * d_m + 4 * d_f
    assert d_m == d and d_f == d

    x2 = x.reshape(b * s, d)
    for l in range(depth):
        assert w_in.shape[2] == n_main + N_FOX_HEADS
        f_bias = jnp.pad(fox_forget_bias[l], (0, LANES - N_FOX_HEADS)).reshape(1, LANES)
        xm, gm, qf, kf, vf, gf, c_keys = _in_proj(
            x2, norm_gain[l].reshape(1, d), w_in[l].T, f_bias, s, dh_f ** -0.5 * LOG2E)

        lane_pad = ((0, 0), (0, LANES - N_MLSTM_HEADS))
        wg = jnp.concatenate([jnp.pad(w_igate[l], lane_pad), jnp.pad(w_fgate[l], lane_pad)], axis=1)
        gb = jnp.concatenate([jnp.pad(b_igate[l][None], lane_pad),
                              jnp.pad(b_fgate[l][None], lane_pad)], axis=1)
        y_m = _mlstm(xm.reshape(b, s, d_m), gm.reshape(b, s, d_m), conv_w[l],
                     conv_b[l].reshape(1, d_m),
                     _block_diag(w_q_m[l], dh_m).astype(BF16),
                     _block_diag(w_k_m[l], dh_m).astype(BF16),
                     _block_diag(w_v_m[l], dh_m).astype(BF16),
                     wg.astype(BF16), gb,
                     mlstm_norm_gain[l].reshape(1, d_m), mlstm_skip[l].reshape(1, d_m))

        y_f = _fox(qf.reshape(b, s, d_f), kf.reshape(b, s, d_f), vf.reshape(b, s, d_f), c_keys,
                   gf.reshape(b, s, d_f), fox_norm_gain[l].reshape(1, d_f))

        x2 = _out_proj(y_m.reshape(b * s, d_m), y_f.reshape(b * s, d_f), x2, w_out[l],
                       final_norm_gain.reshape(1, d), final_norm=(l == depth - 1))
    return x2.reshape(b, s, d)
```

```python
import functools
import math

import jax
import jax.numpy as jnp
from jax import lax
from jax.experimental import pallas as pl
from jax.experimental.pallas import tpu as pltpu

EPS = 1e-6
N_MLSTM_HEADS = 4
N_FOX_HEADS = 8
CONV_WIDTH = 4

LANES = 128
SUBLANES = 8
VMEM_LIMIT = 56 * 1024 * 1024

IN_PROJ_BLOCKS = (256, 256, 256, 256)
IN_PROJ_STAGE_ROWS = 512
OUT_PROJ_ROWS = 1024
OUT_PROJ_SUB = 256
MLSTM_CHUNK = 256
MLSTM_SEQS = 2
FOX_HEADS = 4
FOX_LOOKAHEAD = 2
FOX_Q = 256
LOG2E = math.log2(math.e)

BF16 = jnp.bfloat16
F32 = jnp.float32
NEG_INF = float("-inf")


def _log_sigmoid(z):
    return jnp.minimum(z, 0.0) - jnp.log(1.0 + jnp.exp(-jnp.abs(z)))


def _sigmoid(z):
    return 1.0 / (1.0 + jnp.exp(-z))


def _silu(z):
    return z * _sigmoid(z)


def _split3(v):
    hi = v.astype(BF16)
    r = v - hi.astype(F32)
    mid = r.astype(BF16)
    lo = (r - mid.astype(F32)).astype(BF16)
    return hi, mid, lo


def _cumsum_rows(tri, v):
    w = v.shape[1]
    cat = jnp.concatenate(_split3(v), axis=1)
    cs = jnp.dot(tri, cat, preferred_element_type=F32)
    return cs[:, :w] + cs[:, w:2 * w] + cs[:, 2 * w:]


def _tril_mask(n):
    row = lax.broadcasted_iota(jnp.int32, (n, n), 0)
    col = lax.broadcasted_iota(jnp.int32, (n, n), 1)
    return col <= row


def _in_proj_kernel(x_ref, g_ref, w_hbm, fb_ref,
                    xm_ref, gm_ref, q_ref, k_ref, v_ref, gf_ref, c_ref,
                    wb_ref, wfb_ref, tri_ref, carry_ref, stage_ref, gstage_ref, sems,
                    *, tiles_per_seq, q_scale):
    t = pl.program_id(0)
    rows, d = x_ref.shape
    sub = tri_ref.shape[0]
    n_main = wb_ref.shape[1]
    n_gate = gstage_ref.shape[0]
    stage_rows = stage_ref.shape[1]
    n_chunks = n_main // stage_rows

    def chunk_copy(c):
        return pltpu.make_async_copy(w_hbm.at[pl.ds(c * stage_rows, stage_rows), :], stage_ref.at[c % 2],
                                     sems.at[c % 2])

    def gate_copy():
        return pltpu.make_async_copy(w_hbm.at[pl.ds(n_main, n_gate), :], gstage_ref, sems.at[2])

    @pl.when(t == 0)
    def _():
        gate_copy().start()
        chunk_copy(0).start()
        tri_ref[...] = jnp.where(_tril_mask(sub), 1.0, 0.0).astype(BF16)
        for c in range(n_chunks):
            if c + 1 < n_chunks:
                chunk_copy(c + 1).start()
            chunk_copy(c).wait()
            wb_ref[:, c * stage_rows:(c + 1) * stage_rows] = stage_ref[c % 2].T.astype(BF16)
        gate_copy().wait()
        w_gate = jnp.concatenate([gstage_ref[...], jnp.zeros((LANES - n_gate, d), F32)], axis=0)
        wfb_ref[...] = w_gate.T.astype(BF16)

    @pl.when(t % tiles_per_seq == 0)
    def _():
        carry_ref[...] = jnp.zeros_like(carry_ref)

    starts = [sum(IN_PROJ_BLOCKS[:i]) for i in range(len(IN_PROJ_BLOCKS))]
    blocks = [slice(r0, r0 + n) for r0, n in zip(starts, IN_PROJ_BLOCKS)]
    hns = []
    for sl in blocks:
        x = x_ref[sl, :]
        ms = jnp.mean(x * x, axis=-1, keepdims=True)
        hns.append((x * lax.rsqrt(ms + EPS) * g_ref[...]).astype(BF16))

    def proj(hn, idx):
        return jnp.dot(hn, wb_ref[:, idx * d:(idx + 1) * d], preferred_element_type=F32)

    for sl, hn in zip(blocks, hns):
        xm_ref[sl, :] = proj(hn, 0).astype(xm_ref.dtype)
    log_f = [_log_sigmoid(jnp.dot(hn, wfb_ref[...], preferred_element_type=F32) + fb_ref[...])
             for hn in hns]
    for sl, hn in zip(blocks, hns):
        gm_ref[sl, :] = _silu(proj(hn, 1)).astype(gm_ref.dtype)
    for sl, hn in zip(blocks, hns):
        gf_ref[sl, :] = _silu(proj(hn, 5)).astype(gf_ref.dtype)
    carry = carry_ref[...]
    for sl, lf in zip(blocks, log_f):
        n = lf.shape[0]
        c = _cumsum_rows(tri_ref[0:n, 0:n], lf) + carry
        c_ref[:, sl] = (c * LOG2E).T[0:c_ref.shape[0], :]
        carry = c[n - 1:n, :]
    carry_ref[...] = carry
    for sl, hn in zip(blocks, hns):
        q_ref[sl, :] = (proj(hn, 2) * q_scale).astype(q_ref.dtype)
    for sl, hn in zip(blocks, hns):
        k_ref[sl, :] = proj(hn, 3).astype(k_ref.dtype)
    for sl, hn in zip(blocks, hns):
        v_ref[sl, :] = proj(hn, 4).astype(v_ref.dtype)


def _in_proj(x2, gain, w_in_t, f_bias, seq_len, q_scale):
    t_rows, d = x2.shape
    rows, sub = sum(IN_PROJ_BLOCKS), max(IN_PROJ_BLOCKS)
    assert t_rows % rows == 0 and seq_len % rows == 0
    n_main = 6 * d
    n_gate = w_in_t.shape[0] - n_main
    assert 0 < n_gate <= SUBLANES and w_in_t.shape[1] == d
    row_spec = pl.BlockSpec((rows, d), lambda t: (t, 0))
    const = lambda shape: pl.BlockSpec(shape, lambda t: (0, 0), pipeline_mode=pl.Buffered(1))
    kern = functools.partial(_in_proj_kernel, tiles_per_seq=seq_len // rows, q_scale=q_scale)
    return pl.pallas_call(
        kern,
        grid=(t_rows // rows,),
        in_specs=[row_spec, const((1, d)), pl.BlockSpec(memory_space=pl.ANY), const((1, LANES))],
        out_specs=[row_spec] * 6 + [pl.BlockSpec((SUBLANES, rows), lambda t: (0, t))],
        out_shape=[jax.ShapeDtypeStruct((t_rows, d), BF16)] * 6
        + [jax.ShapeDtypeStruct((SUBLANES, t_rows), F32)],
        scratch_shapes=[pltpu.VMEM((d, n_main), BF16), pltpu.VMEM((d, LANES), BF16),
                        pltpu.VMEM((sub, sub), BF16), pltpu.VMEM((1, LANES), F32),
                        pltpu.VMEM((2, IN_PROJ_STAGE_ROWS, d), F32), pltpu.VMEM((n_gate, d), F32),
                        pltpu.SemaphoreType.DMA((3,))],
        compiler_params=pltpu.CompilerParams(
            dimension_semantics=("arbitrary",), vmem_limit_bytes=VMEM_LIMIT),
        name="in_proj",
    )(x2, gain, w_in_t, f_bias)


def _mlstm_kernel(xm_ref, gm_ref, cw_ref, cb_ref, wq_ref, wk_ref, wv_ref, wg_ref, gb_ref,
                  ng_ref, sk_ref, y_ref,
                  tail_ref, c_sc, n_sc, m_sc, tri_ref, shift_ref, gw_ref, *, k_scale):
    ci = pl.program_id(1)
    n_seqs, L, d_m = xm_ref.shape
    n_heads, dh, _ = wq_ref.shape

    @pl.when(ci == 0)
    def _():
        for h in range(n_heads):
            sl = slice(h * dh, (h + 1) * dh)
            g_q = jnp.dot(wq_ref[h], wg_ref[h * dh:(h + 1) * dh, :], preferred_element_type=F32)
            g_k = jnp.dot(wk_ref[h], wg_ref[d_m + h * dh:d_m + (h + 1) * dh, :],
                          preferred_element_type=F32)
            g_v = jnp.dot(wv_ref[h], wg_ref[2 * d_m + h * dh:2 * d_m + (h + 1) * dh, :],
                          preferred_element_type=F32)
            gw_ref[0, sl, :] = (g_q + g_k * k_scale).astype(BF16)
            gw_ref[1, sl, :] = g_v.astype(BF16)
        c_sc[...] = jnp.zeros_like(c_sc)
        n_sc[...] = jnp.zeros_like(n_sc)
        m_sc[...] = jnp.zeros_like(m_sc)
        tail_ref[...] = jnp.zeros_like(tail_ref)
        row = lax.broadcasted_iota(jnp.int32, (L, L), 0)
        col = lax.broadcasted_iota(jnp.int32, (L, L), 1)
        tri_ref[...] = jnp.where(col <= row, 1.0, 0.0).astype(BF16)
        for j in range(1, CONV_WIDTH):
            shift_ref[j - 1] = jnp.where(col == row - j, 1.0, 0.0).astype(BF16)

    seqs = range(n_seqs)
    fronts = [_mlstm_front(xm_ref.at[i], cw_ref, cb_ref, wq_ref, wk_ref, wv_ref, gw_ref, gb_ref,
                           tail_ref.at[i], tri_ref, shift_ref, k_scale) for i in seqs]
    for i in seqs:
        mixed = _mlstm_mix(*fronts[i][1:], c_sc.at[i], n_sc.at[i], m_sc.at[i])
        _mlstm_out(mixed, fronts[i][0], gm_ref.at[i], ng_ref, sk_ref, y_ref.at[i])


def _mlstm_front(xm_ref, cw_ref, cb_ref, wq_ref, wk_ref, wv_ref, gw_ref, gb_ref,
                 tail_ref, tri_ref, shift_ref, k_scale):
    L, d_m = xm_ref.shape
    dh = wq_ref.shape[1]
    n_heads = wq_ref.shape[0]
    halo = SUBLANES

    x_b = xm_ref[...]
    x_f = x_b.astype(F32)
    conv = cb_ref[...] + x_f * cw_ref[CONV_WIDTH - 1:CONV_WIDTH, :]
    edge = jnp.concatenate([tail_ref[...], jnp.zeros((halo, d_m), F32)], axis=0)
    head = jnp.zeros((halo, d_m), F32)
    for j in range(1, CONV_WIDTH):
        w_j = cw_ref[CONV_WIDTH - 1 - j:CONV_WIDTH - j, :]
        conv = conv + jnp.dot(shift_ref[j - 1], x_b, preferred_element_type=F32) * w_j
        head = head + edge[halo - j:2 * halo - j, :] * w_j
    conv = jnp.concatenate([conv[:halo] + head, conv[halo:]], axis=0)
    tail_ref[...] = x_f[L - halo:L, :]
    xc = _silu(conv)
    xc_b = xc.astype(BF16)

    gates = (jnp.dot(xc_b, gw_ref[0], preferred_element_type=F32)
             + jnp.dot(x_b, gw_ref[1], preferred_element_type=F32) + gb_ref[...])
    qs, ks, vs = [], [], []
    for h in range(n_heads):
        sl = slice(h * dh, (h + 1) * dh)
        q = jnp.dot(xc_b[:, sl], wq_ref[h], preferred_element_type=F32)
        k = jnp.dot(xc_b[:, sl], wk_ref[h], preferred_element_type=F32) * k_scale
        v = jnp.dot(x_b[:, sl], wv_ref[h], preferred_element_type=F32)
        qs.append(q.astype(BF16))
        ks.append((k, k.astype(BF16)))
        vs.append(v.astype(BF16))

    ig = gates[:, :LANES]
    bcum = _cumsum_rows(tri_ref[...], _log_sigmoid(gates[:, LANES:]))
    a_col = ig - bcum
    return xc, qs, ks, vs, a_col, bcum


def _mlstm_mix(qs, ks, vs, a_col, bcum, c_sc, n_sc, m_sc):
    n_heads = len(qs)
    L, dh = qs[0].shape
    a_rows = a_col.T
    causal = _tril_mask(L)
    ones_blk = jnp.ones((L, LANES), BF16)

    heads = range(n_heads)
    nt = (((1,), (1,)), ((), ()))
    m_prev = {h: m_sc[h][:, 0:1] for h in heads}
    dm = {h: jnp.where(causal, a_rows[h:h + 1, :], NEG_INF) for h in heads}
    m_run = {h: jnp.maximum(m_prev[h], jnp.max(dm[h], axis=1, keepdims=True)) for h in heads}
    w_inter = {h: jnp.exp(m_prev[h] - m_run[h]) for h in heads}
    s_b = {h: (lax.dot_general(qs[h], ks[h][1], nt, preferred_element_type=F32)
               * jnp.exp(dm[h] - m_run[h])).astype(BF16) for h in heads}
    num = {h: w_inter[h] * jnp.dot(qs[h], c_sc[h].astype(BF16), preferred_element_type=F32)
           + jnp.dot(s_b[h], vs[h], preferred_element_type=F32) for h in heads}
    qn = {h: lax.dot_general(qs[h], jnp.broadcast_to(n_sc[h], (LANES, dh)).astype(BF16), nt,
                             preferred_element_type=F32) for h in heads}
    den = {h: w_inter[h] * qn[h] + jnp.dot(s_b[h], ones_blk, preferred_element_type=F32)
           for h in heads}
    m_t = {h: bcum[:, h:h + 1] + m_run[h] for h in heads}
    inv = {h: 1.0 / jnp.maximum(jnp.abs(den[h]), jnp.exp(-m_t[h])) for h in heads}
    hh = [num[h] * jnp.concatenate([inv[h]] * (dh // LANES), axis=1) for h in heads]

    for h in heads:
        m_last = m_run[h][L - 1:L, :]
        decay = jnp.exp(m_prev[h] - m_last)
        kw = ks[h][0] * jnp.exp(a_col[:, h:h + 1] - m_last)
        c_sc[h] = decay * c_sc[h] + lax.dot_general(
            kw.astype(BF16), vs[h], (((0,), (0,)), ((), ())), preferred_element_type=F32)
        n_sc[h] = decay * n_sc[h] + jnp.sum(kw, axis=0, keepdims=True)
        m_sc[h] = jnp.broadcast_to(m_t[h][L - 1:L, :], m_sc.shape[1:])
    return hh


def _mlstm_out(hh, xc, gm_ref, ng_ref, sk_ref, y_ref):
    heads = range(len(hh))
    dh = hh[0].shape[1]
    mu = [jnp.mean(hh[h], axis=1, keepdims=True) for h in heads]
    cen = [hh[h] - mu[h] for h in heads]
    var = [jnp.mean(cen[h] * cen[h], axis=1, keepdims=True) for h in heads]
    for h in heads:
        sl = slice(h * dh, (h + 1) * dh)
        hn = cen[h] * lax.rsqrt(var[h] + EPS) * ng_ref[:, sl] + sk_ref[:, sl] * xc[:, sl]
        y_ref[:, sl] = (hn * gm_ref[:, sl].astype(F32)).astype(y_ref.dtype)


def _mlstm(xm, gm, conv_w, conv_b, wq, wk, wv, wg, gb, ngain, skip):
    b, s, d = xm.shape
    L = MLSTM_CHUNK
    assert s % L == 0
    n_heads, dh, _ = wq.shape
    n_seqs = MLSTM_SEQS
    assert b % n_seqs == 0
    seq_spec = pl.BlockSpec((n_seqs, L, d), lambda bi, ci: (bi, ci, 0))

    def const(shape):
        nd = len(shape)
        return pl.BlockSpec(shape, lambda bi, ci: (0,) * nd)

    kern = functools.partial(_mlstm_kernel, k_scale=dh ** -0.5)
    return pl.pallas_call(
        kern,
        grid=(b // n_seqs, s // L),
        in_specs=[seq_spec, seq_spec, const(conv_w.shape), const(conv_b.shape), const(wq.shape),
                  const(wk.shape), const(wv.shape), const(wg.shape), const(gb.shape),
                  const(ngain.shape), const(skip.shape)],
        out_specs=seq_spec,
        out_shape=jax.ShapeDtypeStruct((b, s, d), BF16),
        scratch_shapes=[
            pltpu.VMEM((n_seqs, SUBLANES, d), F32),
            pltpu.VMEM((n_seqs, n_heads, dh, dh), F32),
            pltpu.VMEM((n_seqs, n_heads, 1, dh), F32),
            pltpu.VMEM((n_seqs, n_heads, 1, LANES), F32),
            pltpu.VMEM((L, L), BF16),
            pltpu.VMEM((CONV_WIDTH - 1, L, L), BF16),
            pltpu.VMEM((2, d, wg.shape[1]), BF16),
        ],
        compiler_params=pltpu.CompilerParams(
            dimension_semantics=("arbitrary", "arbitrary"), vmem_limit_bytes=VMEM_LIMIT),
        name="mlstm",
    )(xm, gm, conv_w, conv_b, wq, wk, wv, wg, gb, ngain, skip)


def _fox_kernel(q_ref, k_ref, v_ref, cr_ref, gf_ref, g_ref, o_ref):
    s_len = q_ref.shape[0]
    dh = LANES
    n_local = q_ref.shape[1] // dh
    tq = FOX_Q
    causal = _tril_mask(tq)
    head0 = pl.program_id(1) * n_local

    def scores(unit):
        hl, qi = unit
        r0 = qi * tq
        cols = slice(hl * dh, (hl + 1) * dh)
        return lax.dot_general(q_ref[r0:r0 + tq, cols], k_ref[0:r0 + tq, cols],
                               (((1,), (1,)), ((), ())), preferred_element_type=F32)

    order = [(hl, qi) for hl in range(n_local) for qi in reversed(range(s_len // tq))]
    ahead = [scores(u) for u in order[:FOX_LOOKAHEAD]]
    for pos, (hl, qi) in enumerate(order):
        r0 = qi * tq
        kv = r0 + tq
        cols = slice(hl * dh, (hl + 1) * dh)
        s = ahead.pop(0)
        if pos + FOX_LOOKAHEAD < len(order):
            ahead.append(scores(order[pos + FOX_LOOKAHEAD]))
        s = s - cr_ref[pl.ds(head0 + hl, 1), 0:kv]
        s_diag = jnp.where(causal, s[:, r0:], NEG_INF)
        s = s_diag if qi == 0 else jnp.concatenate([s[:, :r0], s_diag], axis=1)
        m = jnp.max(s, axis=1, keepdims=True)
        p = jnp.exp2(s - m).astype(BF16)
        v_ext = jnp.concatenate([v_ref[0:kv, cols], jnp.ones((kv, dh), BF16)], axis=1)
        acc = jnp.dot(p, v_ext, preferred_element_type=F32)

        out = acc[:, :dh] / acc[:, dh:]
        ms = jnp.mean(out * out, axis=1, keepdims=True)
        o_ref[r0:kv, cols] = (out * lax.rsqrt(ms + EPS) * g_ref[:, cols]
                              * gf_ref[r0:kv, cols].astype(F32)).astype(o_ref.dtype)


def _fox(q, k, v, c_row, gf, gain):
    b, s, d = q.shape
    n_heads = N_FOX_HEADS
    dh = d // n_heads
    assert dh == LANES and s % FOX_Q == 0 and n_heads % FOX_HEADS == 0
    width = FOX_HEADS * dh
    head_spec = pl.BlockSpec((None, s, width), lambda bi, hi: (bi, 0, hi))
    return pl.pallas_call(
        _fox_kernel,
        grid=(b, n_heads // FOX_HEADS),
        in_specs=[head_spec, head_spec, head_spec,
                  pl.BlockSpec((c_row.shape[0], s), lambda bi, hi: (0, bi)),
                  head_spec,
                  pl.BlockSpec((1, width), lambda bi, hi: (0, hi))],
        out_specs=head_spec,
        out_shape=jax.ShapeDtypeStruct((b, s, d), BF16),
        compiler_params=pltpu.CompilerParams(
            dimension_semantics=("arbitrary", "arbitrary"), vmem_limit_bytes=VMEM_LIMIT),
        name="fox",
    )(q, k, v, c_row, gf, gain)


def _out_proj_kernel(ym_ref, yf_ref, x_ref, w_ref, g_ref, o_ref, wb_ref, *, final_norm):
    d_m = ym_ref.shape[1]

    @pl.when(pl.program_id(0) == 0)
    def _():
        wb_ref[...] = w_ref[...].astype(BF16)

    rows = x_ref.shape[0]
    for r0 in range(0, rows, OUT_PROJ_SUB):
        sl = slice(r0, r0 + OUT_PROJ_SUB)
        y = jnp.dot(ym_ref[sl, :], wb_ref[0:d_m, :], preferred_element_type=F32)
        y = y + jnp.dot(yf_ref[sl, :], wb_ref[d_m:, :], preferred_element_type=F32)
        r = x_ref[sl, :] + y
        if final_norm:
            ms = jnp.mean(r * r, axis=-1, keepdims=True)
            r = r * lax.rsqrt(ms + EPS) * g_ref[...]
        o_ref[sl, :] = r


def _out_proj(ym, yf, x2, w_out, gain, final_norm):
    t_rows, d = x2.shape
    rows = OUT_PROJ_ROWS
    assert t_rows % rows == 0
    row = lambda width: pl.BlockSpec((rows, width), lambda t: (t, 0))
    const = lambda shape: pl.BlockSpec(shape, lambda t: (0, 0), pipeline_mode=pl.Buffered(1))
    kern = functools.partial(_out_proj_kernel, final_norm=final_norm)
    return pl.pallas_call(
        kern,
        grid=(t_rows // rows,),
        in_specs=[row(ym.shape[1]), row(yf.shape[1]), row(d), const(w_out.shape), const((1, d))],
        out_specs=row(d),
        out_shape=jax.ShapeDtypeStruct((t_rows, d), F32),
        scratch_shapes=[pltpu.VMEM(w_out.shape, BF16)],
        compiler_params=pltpu.CompilerParams(
            dimension_semantics=("arbitrary",), vmem_limit_bytes=VMEM_LIMIT),
        name="out_proj",
    )(ym, yf, x2, w_out, gain)


def _block_diag(w, dh):
    nb, bs, _ = w.shape
    d = nb * bs
    tiled = jnp.broadcast_to(w.reshape(d, 1, bs), (d, dh // bs, bs)).reshape(d, dh)
    r = lax.broadcasted_iota(jnp.int32, (d, dh), 0)
    c = lax.broadcasted_iota(jnp.int32, (d, dh), 1)
    keep = (r % dh) // bs == c // bs
    return jnp.where(keep, tiled, 0.0).reshape(d // dh, dh, dh)


def kernel(x, norm_gain, w_in, conv_w, conv_b, w_q_m, w_k_m, w_v_m, w_igate, b_igate, w_fgate,
           b_fgate, mlstm_norm_gain, mlstm_skip, fox_forget_bias, fox_norm_gain, w_out,
           final_norm_gain):
    b, s, d = x.shape
    depth = norm_gain.shape[0]
    d_m = conv_w.shape[-1]
    d_f = fox_norm_gain.shape[-1]
    dh_m = d_m // N_MLSTM_HEADS
    dh_f = d_f // N_FOX_HEADS
    n_main = 2 * d_m + 4 * d_f
    assert d_m == d and d_f == d

    x2 = x.reshape(b * s, d)
    for l in range(depth):
        assert w_in.shape[2] == n_main + N_FOX_HEADS
        f_bias = jnp.pad(fox_forget_bias[l], (0, LANES - N_FOX_HEADS)).reshape(1, LANES)
        xm, gm, qf, kf, vf, gf, c_keys = _in_proj(
            x2, norm_gain[l].reshape(1, d), w_in[l].T, f_bias, s, dh_f ** -0.5 * LOG2E)

        lane_pad = ((0, 0), (0, LANES - N_MLSTM_HEADS))
        wg = jnp.concatenate([jnp.pad(w_igate[l], lane_pad), jnp.pad(w_fgate[l], lane_pad)], axis=1)
        gb = jnp.concatenate([jnp.pad(b_igate[l][None], lane_pad),
                              jnp.pad(b_fgate[l][None], lane_pad)], axis=1)
        y_m = _mlstm(xm.reshape(b, s, d_m), gm.reshape(b, s, d_m), conv_w[l],
                     conv_b[l].reshape(1, d_m),
                     _block_diag(w_q_m[l], dh_m).astype(BF16),
                     _block_diag(w_k_m[l], dh_m).astype(BF16),
                     _block_diag(w_v_m[l], dh_m).astype(BF16),
                     wg.astype(BF16), gb,
                     mlstm_norm_gain[l].reshape(1, d_m), mlstm_skip[l].reshape(1, d_m))

        y_f = _fox(qf.reshape(b, s, d_f), kf.reshape(b, s, d_f), vf.reshape(b, s, d_f), c_keys,
                   gf.reshape(b, s, d_f), fox_norm_gain[l].reshape(1, d_f))

        x2 = _out_proj(y_m.reshape(b * s, d_m), y_f.reshape(b * s, d_f), x2, w_out[l],
                       final_norm_gain.reshape(1, d), final_norm=(l == depth - 1))
    return x2.reshape(b, s, d)
```

```python
import functools
import math

import jax
import jax.numpy as jnp
from jax import lax
from jax.experimental import pallas as pl
from jax.experimental.pallas import tpu as pltpu

EPS = 1e-6
N_MLSTM_HEADS = 4
N_FOX_HEADS = 8
CONV_WIDTH = 4

LANES = 128
SUBLANES = 8
VMEM_LIMIT = 56 * 1024 * 1024

IN_PROJ_BLOCKS = (256, 256, 256, 256)
IN_PROJ_STAGE_ROWS = 512
OUT_PROJ_ROWS = 1024
OUT_PROJ_SUB = 256
MLSTM_CHUNK = 256
MLSTM_SEQS = 2
FOX_HEADS = 2
FOX_LOOKAHEAD = 2
FOX_Q = 256
LOG2E = math.log2(math.e)

BF16 = jnp.bfloat16
F32 = jnp.float32
NEG_INF = float("-inf")


def _log_sigmoid(z):
    return jnp.minimum(z, 0.0) - jnp.log(1.0 + jnp.exp(-jnp.abs(z)))


def _sigmoid(z):
    return 1.0 / (1.0 + jnp.exp(-z))


def _silu(z):
    return z * _sigmoid(z)


def _split3(v):
    hi = v.astype(BF16)
    r = v - hi.astype(F32)
    mid = r.astype(BF16)
    lo = (r - mid.astype(F32)).astype(BF16)
    return hi, mid, lo


def _cumsum_rows(tri, v):
    w = v.shape[1]
    cat = jnp.concatenate(_split3(v), axis=1)
    cs = jnp.dot(tri, cat, preferred_element_type=F32)
    return cs[:, :w] + cs[:, w:2 * w] + cs[:, 2 * w:]


def _cumsum_rows_packed(tri, v, width):
    lanes = v.shape[1]
    assert 3 * width <= lanes
    lane = lax.broadcasted_iota(jnp.int32, v.shape, 1)
    hi, mid, lo = (t.astype(F32) for t in _split3(jnp.where(lane < width, v, 0.0)))
    packed = hi + pltpu.roll(mid, width, axis=1) + pltpu.roll(lo, 2 * width, axis=1)
    cs = jnp.dot(tri, packed.astype(BF16), preferred_element_type=F32)
    return cs + pltpu.roll(cs, lanes - width, axis=1) + pltpu.roll(cs, lanes - 2 * width, axis=1)


def _tril_mask(n):
    row = lax.broadcasted_iota(jnp.int32, (n, n), 0)
    col = lax.broadcasted_iota(jnp.int32, (n, n), 1)
    return col <= row


def _in_proj_kernel(x_ref, g_ref, w_hbm, fb_ref,
                    xm_ref, gm_ref, q_ref, k_ref, v_ref, gf_ref, c_ref,
                    wb_ref, wfb_ref, tri_ref, carry_ref, stage_ref, gstage_ref, sems,
                    *, tiles_per_seq, q_scale):
    t = pl.program_id(0)
    rows, d = x_ref.shape
    sub = tri_ref.shape[0]
    n_main = wb_ref.shape[1]
    n_gate = gstage_ref.shape[0]
    stage_rows = stage_ref.shape[1]
    n_chunks = n_main // stage_rows

    def chunk_copy(c):
        return pltpu.make_async_copy(w_hbm.at[pl.ds(c * stage_rows, stage_rows), :], stage_ref.at[c % 2],
                                     sems.at[c % 2])

    def gate_copy():
        return pltpu.make_async_copy(w_hbm.at[pl.ds(n_main, n_gate), :], gstage_ref, sems.at[2])

    @pl.when(t == 0)
    def _():
        gate_copy().start()
        chunk_copy(0).start()
        tri_ref[...] = jnp.where(_tril_mask(sub), 1.0, 0.0).astype(BF16)
        for c in range(n_chunks):
            if c + 1 < n_chunks:
                chunk_copy(c + 1).start()
            chunk_copy(c).wait()
            wb_ref[:, c * stage_rows:(c + 1) * stage_rows] = stage_ref[c % 2].T.astype(BF16)
        gate_copy().wait()
        w_gate = jnp.concatenate([gstage_ref[...], jnp.zeros((LANES - n_gate, d), F32)], axis=0)
        wfb_ref[...] = w_gate.T.astype(BF16)

    @pl.when(t % tiles_per_seq == 0)
    def _():
        carry_ref[...] = jnp.zeros_like(carry_ref)

    starts = [sum(IN_PROJ_BLOCKS[:i]) for i in range(len(IN_PROJ_BLOCKS))]
    blocks = [slice(r0, r0 + n) for r0, n in zip(starts, IN_PROJ_BLOCKS)]
    hns = []
    for sl in blocks:
        x = x_ref[sl, :]
        ms = jnp.mean(x * x, axis=-1, keepdims=True)
        hns.append((x * lax.rsqrt(ms + EPS) * g_ref[...]).astype(BF16))

    def proj(hn, idx):
        return jnp.dot(hn, wb_ref[:, idx * d:(idx + 1) * d], preferred_element_type=F32)

    for sl, hn in zip(blocks, hns):
        xm_ref[sl, :] = proj(hn, 0).astype(xm_ref.dtype)
    log_f = [_log_sigmoid(jnp.dot(hn, wfb_ref[...], preferred_element_type=F32) + fb_ref[...])
             for hn in hns]
    for sl, hn in zip(blocks, hns):
        gm_ref[sl, :] = _silu(proj(hn, 1)).astype(gm_ref.dtype)
    for sl, hn in zip(blocks, hns):
        gf_ref[sl, :] = _silu(proj(hn, 5)).astype(gf_ref.dtype)
    carry = carry_ref[...]
    for sl, lf in zip(blocks, log_f):
        n = lf.shape[0]
        c = _cumsum_rows_packed(tri_ref[0:n, 0:n], lf, c_ref.shape[0]) + carry
        c_ref[:, sl] = (c * LOG2E).T[0:c_ref.shape[0], :]
        carry = c[n - 1:n, :]
    carry_ref[...] = carry
    for sl, hn in zip(blocks, hns):
        q_ref[sl, :] = (proj(hn, 2) * q_scale).astype(q_ref.dtype)
    for sl, hn in zip(blocks, hns):
        k_ref[sl, :] = proj(hn, 3).astype(k_ref.dtype)
    for sl, hn in zip(blocks, hns):
        v_ref[sl, :] = proj(hn, 4).astype(v_ref.dtype)


def _in_proj(x2, gain, w_in_t, f_bias, seq_len, q_scale):
    t_rows, d = x2.shape
    rows, sub = sum(IN_PROJ_BLOCKS), max(IN_PROJ_BLOCKS)
    assert t_rows % rows == 0 and seq_len % rows == 0
    n_main = 6 * d
    n_gate = w_in_t.shape[0] - n_main
    assert 0 < n_gate <= SUBLANES and w_in_t.shape[1] == d
    row_spec = pl.BlockSpec((rows, d), lambda t: (t, 0))
    const = lambda shape: pl.BlockSpec(shape, lambda t: (0, 0), pipeline_mode=pl.Buffered(1))
    kern = functools.partial(_in_proj_kernel, tiles_per_seq=seq_len // rows, q_scale=q_scale)
    return pl.pallas_call(
        kern,
        grid=(t_rows // rows,),
        in_specs=[row_spec, const((1, d)), pl.BlockSpec(memory_space=pl.ANY), const((1, LANES))],
        out_specs=[row_spec] * 6 + [pl.BlockSpec((SUBLANES, rows), lambda t: (0, t))],
        out_shape=[jax.ShapeDtypeStruct((t_rows, d), BF16)] * 6
        + [jax.ShapeDtypeStruct((SUBLANES, t_rows), F32)],
        scratch_shapes=[pltpu.VMEM((d, n_main), BF16), pltpu.VMEM((d, LANES), BF16),
                        pltpu.VMEM((sub, sub), BF16), pltpu.VMEM((1, LANES), F32),
                        pltpu.VMEM((2, IN_PROJ_STAGE_ROWS, d), F32), pltpu.VMEM((n_gate, d), F32),
                        pltpu.SemaphoreType.DMA((3,))],
        compiler_params=pltpu.CompilerParams(
            dimension_semantics=("arbitrary",), vmem_limit_bytes=VMEM_LIMIT),
        name="in_proj",
    )(x2, gain, w_in_t, f_bias)


def _mlstm_kernel(xm_ref, gm_ref, cw_ref, cb_ref, wq_ref, wk_ref, wv_ref, wg_ref, gb_ref,
                  ng_ref, sk_ref, y_ref,
                  tail_ref, c_sc, n_sc, m_sc, tri_ref, shift_ref, gw_ref, *, k_scale):
    ci = pl.program_id(1)
    n_seqs, L, d_m = xm_ref.shape
    n_heads, dh, _ = wq_ref.shape

    @pl.when(ci == 0)
    def _():
        for h in range(n_heads):
            sl = slice(h * dh, (h + 1) * dh)
            g_q = jnp.dot(wq_ref[h], wg_ref[h * dh:(h + 1) * dh, :], preferred_element_type=F32)
            g_k = jnp.dot(wk_ref[h], wg_ref[d_m + h * dh:d_m + (h + 1) * dh, :],
                          preferred_element_type=F32)
            g_v = jnp.dot(wv_ref[h], wg_ref[2 * d_m + h * dh:2 * d_m + (h + 1) * dh, :],
                          preferred_element_type=F32)
            gw_ref[0, sl, :] = (g_q + g_k * k_scale).astype(BF16)
            gw_ref[1, sl, :] = g_v.astype(BF16)
        c_sc[...] = jnp.zeros_like(c_sc)
        n_sc[...] = jnp.zeros_like(n_sc)
        m_sc[...] = jnp.zeros_like(m_sc)
        tail_ref[...] = jnp.zeros_like(tail_ref)
        row = lax.broadcasted_iota(jnp.int32, (L, L), 0)
        col = lax.broadcasted_iota(jnp.int32, (L, L), 1)
        tri_ref[...] = jnp.where(col <= row, 1.0, 0.0).astype(BF16)
        for j in range(1, CONV_WIDTH):
            shift_ref[j - 1] = jnp.where(col == row - j, 1.0, 0.0).astype(BF16)

    seqs = range(n_seqs)
    fronts = [_mlstm_front(xm_ref.at[i], cw_ref, cb_ref, wq_ref, wk_ref, wv_ref, gw_ref, gb_ref,
                           tail_ref.at[i], tri_ref, shift_ref, k_scale) for i in seqs]
    for i in seqs:
        mixed = _mlstm_mix(*fronts[i][1:], c_sc.at[i], n_sc.at[i], m_sc.at[i])
        _mlstm_out(mixed, fronts[i][0], gm_ref.at[i], ng_ref, sk_ref, y_ref.at[i])


def _mlstm_front(xm_ref, cw_ref, cb_ref, wq_ref, wk_ref, wv_ref, gw_ref, gb_ref,
                 tail_ref, tri_ref, shift_ref, k_scale):
    L, d_m = xm_ref.shape
    dh = wq_ref.shape[1]
    n_heads = wq_ref.shape[0]
    halo = SUBLANES

    x_b = xm_ref[...]
    x_f = x_b.astype(F32)
    conv = cb_ref[...] + x_f * cw_ref[CONV_WIDTH - 1:CONV_WIDTH, :]
    edge = jnp.concatenate([tail_ref[...], jnp.zeros((halo, d_m), F32)], axis=0)
    head = jnp.zeros((halo, d_m), F32)
    for j in range(1, CONV_WIDTH):
        w_j = cw_ref[CONV_WIDTH - 1 - j:CONV_WIDTH - j, :]
        conv = conv + jnp.dot(shift_ref[j - 1], x_b, preferred_element_type=F32) * w_j
        head = head + edge[halo - j:2 * halo - j, :] * w_j
    conv = jnp.concatenate([conv[:halo] + head, conv[halo:]], axis=0)
    tail_ref[...] = x_f[L - halo:L, :]
    xc = _silu(conv)
    xc_b = xc.astype(BF16)

    gates = (jnp.dot(xc_b, gw_ref[0], preferred_element_type=F32)
             + jnp.dot(x_b, gw_ref[1], preferred_element_type=F32) + gb_ref[...])
    qs, ks, vs = [], [], []
    for h in range(n_heads):
        sl = slice(h * dh, (h + 1) * dh)
        q = jnp.dot(xc_b[:, sl], wq_ref[h], preferred_element_type=F32)
        k = jnp.dot(xc_b[:, sl], wk_ref[h], preferred_element_type=F32) * k_scale
        v = jnp.dot(x_b[:, sl], wv_ref[h], preferred_element_type=F32)
        qs.append(q.astype(BF16))
        ks.append((k, k.astype(BF16)))
        vs.append(v.astype(BF16))

    ig = gates[:, :LANES]
    bcum = _cumsum_rows(tri_ref[...], _log_sigmoid(gates[:, LANES:]))
    a_col = ig - bcum
    return xc, qs, ks, vs, a_col, bcum


def _mlstm_mix(qs, ks, vs, a_col, bcum, c_sc, n_sc, m_sc):
    n_heads = len(qs)
    L, dh = qs[0].shape
    a_rows = a_col.T
    causal = _tril_mask(L)
    ones_blk = jnp.ones((L, LANES), BF16)

    heads = range(n_heads)
    nt = (((1,), (1,)), ((), ()))
    m_prev = {h: m_sc[h][:, 0:1] for h in heads}
    dm = {h: jnp.where(causal, a_rows[h:h + 1, :], NEG_INF) for h in heads}
    m_run = {h: jnp.maximum(m_prev[h], jnp.max(dm[h], axis=1, keepdims=True)) for h in heads}
    w_inter = {h: jnp.exp(m_prev[h] - m_run[h]) for h in heads}
    s_b = {h: (lax.dot_general(qs[h], ks[h][1], nt, preferred_element_type=F32)
               * jnp.exp(dm[h] - m_run[h])).astype(BF16) for h in heads}
    num = {h: w_inter[h] * jnp.dot(qs[h], c_sc[h].astype(BF16), preferred_element_type=F32)
           + jnp.dot(s_b[h], vs[h], preferred_element_type=F32) for h in heads}
    qn = {h: lax.dot_general(qs[h], jnp.broadcast_to(n_sc[h], (LANES, dh)).astype(BF16), nt,
                             preferred_element_type=F32) for h in heads}
    den = {h: w_inter[h] * qn[h] + jnp.dot(s_b[h], ones_blk, preferred_element_type=F32)
           for h in heads}
    m_t = {h: bcum[:, h:h + 1] + m_run[h] for h in heads}
    inv = {h: 1.0 / jnp.maximum(jnp.abs(den[h]), jnp.exp(-m_t[h])) for h in heads}
    hh = [num[h] * jnp.concatenate([inv[h]] * (dh // LANES), axis=1) for h in heads]

    for h in heads:
        m_last = m_run[h][L - 1:L, :]
        decay = jnp.exp(m_prev[h] - m_last)
        kw = ks[h][0] * jnp.exp(a_col[:, h:h + 1] - m_last)
        c_sc[h] = decay * c_sc[h] + lax.dot_general(
            kw.astype(BF16), vs[h], (((0,), (0,)), ((), ())), preferred_element_type=F32)
        n_sc[h] = decay * n_sc[h] + jnp.sum(kw, axis=0, keepdims=True)
        m_sc[h] = jnp.broadcast_to(m_t[h][L - 1:L, :], m_sc.shape[1:])
    return hh


def _mlstm_out(hh, xc, gm_ref, ng_ref, sk_ref, y_ref):
    heads = range(len(hh))
    dh = hh[0].shape[1]
    mu = [jnp.mean(hh[h], axis=1, keepdims=True) for h in heads]
    cen = [hh[h] - mu[h] for h in heads]
    var = [jnp.mean(cen[h] * cen[h], axis=1, keepdims=True) for h in heads]
    for h in heads:
        sl = slice(h * dh, (h + 1) * dh)
        hn = cen[h] * lax.rsqrt(var[h] + EPS) * ng_ref[:, sl] + sk_ref[:, sl] * xc[:, sl]
        y_ref[:, sl] = (hn * gm_ref[:, sl].astype(F32)).astype(y_ref.dtype)


def _mlstm(xm, gm, conv_w, conv_b, wq, wk, wv, wg, gb, ngain, skip):
    b, s, d = xm.shape
    L = MLSTM_CHUNK
    assert s % L == 0
    n_heads, dh, _ = wq.shape
    n_seqs = MLSTM_SEQS
    assert b % n_seqs == 0
    seq_spec = pl.BlockSpec((n_seqs, L, d), lambda bi, ci: (bi, ci, 0))

    def const(shape):
        nd = len(shape)
        return pl.BlockSpec(shape, lambda bi, ci: (0,) * nd)

    kern = functools.partial(_mlstm_kernel, k_scale=dh ** -0.5)
    return pl.pallas_call(
        kern,
        grid=(b // n_seqs, s // L),
        in_specs=[seq_spec, seq_spec, const(conv_w.shape), const(conv_b.shape), const(wq.shape),
                  const(wk.shape), const(wv.shape), const(wg.shape), const(gb.shape),
                  const(ngain.shape), const(skip.shape)],
        out_specs=seq_spec,
        out_shape=jax.ShapeDtypeStruct((b, s, d), BF16),
        scratch_shapes=[
            pltpu.VMEM((n_seqs, SUBLANES, d), F32),
            pltpu.VMEM((n_seqs, n_heads, dh, dh), F32),
            pltpu.VMEM((n_seqs, n_heads, 1, dh), F32),
            pltpu.VMEM((n_seqs, n_heads, 1, LANES), F32),
            pltpu.VMEM((L, L), BF16),
            pltpu.VMEM((CONV_WIDTH - 1, L, L), BF16),
            pltpu.VMEM((2, d, wg.shape[1]), BF16),
        ],
        compiler_params=pltpu.CompilerParams(
            dimension_semantics=("arbitrary", "arbitrary"), vmem_limit_bytes=VMEM_LIMIT),
        name="mlstm",
    )(xm, gm, conv_w, conv_b, wq, wk, wv, wg, gb, ngain, skip)


def _fox_kernel(q_ref, k_ref, v_ref, cr_ref, gf_ref, g_ref, o_ref):
    s_len = q_ref.shape[0]
    dh = LANES
    n_local = q_ref.shape[1] // dh
    tq = FOX_Q
    causal = _tril_mask(tq)
    head0 = pl.program_id(1) * n_local

    def scores(unit):
        hl, qi = unit
        r0 = qi * tq
        cols = slice(hl * dh, (hl + 1) * dh)
        return lax.dot_general(q_ref[r0:r0 + tq, cols], k_ref[0:r0 + tq, cols],
                               (((1,), (1,)), ((), ())), preferred_element_type=F32)

    order = [(hl, qi) for hl in range(n_local) for qi in reversed(range(s_len // tq))]
    ahead = [scores(u) for u in order[:FOX_LOOKAHEAD]]
    for pos, (hl, qi) in enumerate(order):
        r0 = qi * tq
        kv = r0 + tq
        cols = slice(hl * dh, (hl + 1) * dh)
        s = ahead.pop(0)
        if pos + FOX_LOOKAHEAD < len(order):
            ahead.append(scores(order[pos + FOX_LOOKAHEAD]))
        s = s - cr_ref[pl.ds(head0 + hl, 1), 0:kv]
        s_diag = jnp.where(causal, s[:, r0:], NEG_INF)
        s = s_diag if qi == 0 else jnp.concatenate([s[:, :r0], s_diag], axis=1)
        m = jnp.max(s, axis=1, keepdims=True)
        p = jnp.exp2(s - m).astype(BF16)
        v_ext = jnp.concatenate([v_ref[0:kv, cols], jnp.ones((kv, dh), BF16)], axis=1)
        acc = jnp.dot(p, v_ext, preferred_element_type=F32)

        out = acc[:, :dh] / acc[:, dh:]
        ms = jnp.mean(out * out, axis=1, keepdims=True)
        o_ref[r0:kv, cols] = (out * lax.rsqrt(ms + EPS) * g_ref[:, cols]
                              * gf_ref[r0:kv, cols].astype(F32)).astype(o_ref.dtype)


def _fox(q, k, v, c_row, gf, gain):
    b, s, d = q.shape
    n_heads = N_FOX_HEADS
    dh = d // n_heads
    assert dh == LANES and s % FOX_Q == 0 and n_heads % FOX_HEADS == 0
    width = FOX_HEADS * dh
    head_spec = pl.BlockSpec((None, s, width), lambda bi, hi: (bi, 0, hi))
    return pl.pallas_call(
        _fox_kernel,
        grid=(b, n_heads // FOX_HEADS),
        in_specs=[head_spec, head_spec, head_spec,
                  pl.BlockSpec((c_row.shape[0], s), lambda bi, hi: (0, bi)),
                  head_spec,
                  pl.BlockSpec((1, width), lambda bi, hi: (0, hi))],
        out_specs=head_spec,
        out_shape=jax.ShapeDtypeStruct((b, s, d), BF16),
        compiler_params=pltpu.CompilerParams(
            dimension_semantics=("arbitrary", "arbitrary"), vmem_limit_bytes=VMEM_LIMIT),
        name="fox",
    )(q, k, v, c_row, gf, gain)


def _out_proj_kernel(ym_ref, yf_ref, x_ref, w_ref, g_ref, o_ref, wb_ref, *, final_norm):
    d_m = ym_ref.shape[1]

    @pl.when(pl.program_id(0) == 0)
    def _():
        wb_ref[...] = w_ref[...].astype(BF16)

    rows = x_ref.shape[0]
    for r0 in range(0, rows, OUT_PROJ_SUB):
        sl = slice(r0, r0 + OUT_PROJ_SUB)
        y = jnp.dot(ym_ref[sl, :], wb_ref[0:d_m, :], preferred_element_type=F32)
        y = y + jnp.dot(yf_ref[sl, :], wb_ref[d_m:, :], preferred_element_type=F32)
        r = x_ref[sl, :] + y
        if final_norm:
            ms = jnp.mean(r * r, axis=-1, keepdims=True)
            r = r * lax.rsqrt(ms + EPS) * g_ref[...]
        o_ref[sl, :] = r


def _out_proj(ym, yf, x2, w_out, gain, final_norm):
    t_rows, d = x2.shape
    rows = OUT_PROJ_ROWS
    assert t_rows % rows == 0
    row = lambda width: pl.BlockSpec((rows, width), lambda t: (t, 0))
    const = lambda shape: pl.BlockSpec(shape, lambda t: (0, 0), pipeline_mode=pl.Buffered(1))
    kern = functools.partial(_out_proj_kernel, final_norm=final_norm)
    return pl.pallas_call(
        kern,
        grid=(t_rows // rows,),
        in_specs=[row(ym.shape[1]), row(yf.shape[1]), row(d), const(w_out.shape), const((1, d))],
        out_specs=row(d),
        out_shape=jax.ShapeDtypeStruct((t_rows, d), F32),
        scratch_shapes=[pltpu.VMEM(w_out.shape, BF16)],
        compiler_params=pltpu.CompilerParams(
            dimension_semantics=("arbitrary",), vmem_limit_bytes=VMEM_LIMIT),
        name="out_proj",
    )(ym, yf, x2, w_out, gain)


def _block_diag(w, dh):
    nb, bs, _ = w.shape
    d = nb * bs
    tiled = jnp.broadcast_to(w.reshape(d, 1, bs), (d, dh // bs, bs)).reshape(d, dh)
    r = lax.broadcasted_iota(jnp.int32, (d, dh), 0)
    c = lax.broadcasted_iota(jnp.int32, (d, dh), 1)
    keep = (r % dh) // bs == c // bs
    return jnp.where(keep, tiled, 0.0).reshape(d // dh, dh, dh)


def kernel(x, norm_gain, w_in, conv_w, conv_b, w_q_m, w_k_m, w_v_m, w_igate, b_igate, w_fgate,
           b_fgate, mlstm_norm_gain, mlstm_skip, fox_forget_bias, fox_norm_gain, w_out,
           final_norm_gain):
    b, s, d = x.shape
    depth = norm_gain.shape[0]
    d_m = conv_w.shape[-1]
    d_f = fox_norm_gain.shape[-1]
    dh_m = d_m // N_MLSTM_HEADS
    dh_f = d_f // N_FOX_HEADS
    n_main = 2 * d_m + 4 * d_f
    assert d_m == d and d_f == d

    x2 = x.reshape(b * s, d)
    for l in range(depth):
        assert w_in.shape[2] == n_main + N_FOX_HEADS
        f_bias = jnp.pad(fox_forget_bias[l], (0, LANES - N_FOX_HEADS)).reshape(1, LANES)
        xm, gm, qf, kf, vf, gf, c_keys = _in_proj(
            x2, norm_gain[l].reshape(1, d), w_in[l].T, f_bias, s, dh_f ** -0.5 * LOG2E)

        lane_pad = ((0, 0), (0, LANES - N_MLSTM_HEADS))
        wg = jnp.concatenate([jnp.pad(w_igate[l], lane_pad), jnp.pad(w_fgate[l], lane_pad)], axis=1)
        gb = jnp.concatenate([jnp.pad(b_igate[l][None], lane_pad),
                              jnp.pad(b_fgate[l][None], lane_pad)], axis=1)
        y_m = _mlstm(xm.reshape(b, s, d_m), gm.reshape(b, s, d_m), conv_w[l],
                     conv_b[l].reshape(1, d_m),
                     _block_diag(w_q_m[l], dh_m).astype(BF16),
                     _block_diag(w_k_m[l], dh_m).astype(BF16),
                     _block_diag(w_v_m[l], dh_m).astype(BF16),
                     wg.astype(BF16), gb,
                     mlstm_norm_gain[l].reshape(1, d_m), mlstm_skip[l].reshape(1, d_m))

        y_f = _fox(qf.reshape(b, s, d_f), kf.reshape(b, s, d_f), vf.reshape(b, s, d_f), c_keys,
                   gf.reshape(b, s, d_f), fox_norm_gain[l].reshape(1, d_f))

        x2 = _out_proj(y_m.reshape(b * s, d_m), y_f.reshape(b * s, d_f), x2, w_out[l],
                       final_norm_gain.reshape(1, d), final_norm=(l == depth - 1))
    return x2.reshape(b, s, d)
```

```python
import functools
import math

import jax
import jax.numpy as jnp
from jax import lax
from jax.experimental import pallas as pl
from jax.experimental.pallas import tpu as pltpu

EPS = 1e-6
N_MLSTM_HEADS = 4
N_FOX_HEADS = 8
CONV_WIDTH = 4

LANES = 128
SUBLANES = 8
VMEM_LIMIT = 56 * 1024 * 1024

IN_PROJ_BLOCKS = (256, 256, 256, 256)
IN_PROJ_STAGE_ROWS = 512
OUT_PROJ_SUB = 256
OUT_PROJ_GROUP = 7
MLSTM_CHUNK = 256
MLSTM_SEQS = 2
FOX_HEADS = 2
FOX_LOOKAHEAD = 2
FOX_Q = 256
LOG2E = math.log2(math.e)

BF16 = jnp.bfloat16
F32 = jnp.float32
NEG_INF = float("-inf")


def _log_sigmoid(z):
    return jnp.minimum(z, 0.0) - jnp.log(1.0 + jnp.exp(-jnp.abs(z)))


def _sigmoid(z):
    return 1.0 / (1.0 + jnp.exp(-z))


def _silu(z):
    return z * _sigmoid(z)


def _split3(v):
    hi = v.astype(BF16)
    r = v - hi.astype(F32)
    mid = r.astype(BF16)
    lo = (r - mid.astype(F32)).astype(BF16)
    return hi, mid, lo


def _cumsum_rows(tri, v):
    w = v.shape[1]
    cat = jnp.concatenate(_split3(v), axis=1)
    cs = jnp.dot(tri, cat, preferred_element_type=F32)
    return cs[:, :w] + cs[:, w:2 * w] + cs[:, 2 * w:]


def _cumsum_rows_packed(tri, v, width):
    lanes = v.shape[1]
    assert 3 * width <= lanes
    lane = lax.broadcasted_iota(jnp.int32, v.shape, 1)
    hi, mid, lo = (t.astype(F32) for t in _split3(jnp.where(lane < width, v, 0.0)))
    packed = hi + pltpu.roll(mid, width, axis=1) + pltpu.roll(lo, 2 * width, axis=1)
    cs = jnp.dot(tri, packed.astype(BF16), preferred_element_type=F32)
    return cs + pltpu.roll(cs, lanes - width, axis=1) + pltpu.roll(cs, lanes - 2 * width, axis=1)


def _tril_mask(n):
    row = lax.broadcasted_iota(jnp.int32, (n, n), 0)
    col = lax.broadcasted_iota(jnp.int32, (n, n), 1)
    return col <= row


def _in_proj_kernel(x_ref, g_ref, w_hbm, fb_ref,
                    xm_ref, gm_ref, q_ref, k_ref, v_ref, gf_ref, c_ref,
                    wb_ref, wfb_ref, tri_ref, carry_ref, stage_ref, gstage_ref, sems,
                    *, tiles_per_seq, q_scale):
    t = pl.program_id(0)
    rows, d = x_ref.shape
    sub = tri_ref.shape[0]
    n_main = wb_ref.shape[1]
    n_gate = gstage_ref.shape[0]
    stage_rows = stage_ref.shape[1]
    n_chunks = n_main // stage_rows

    def chunk_copy(c):
        return pltpu.make_async_copy(w_hbm.at[pl.ds(c * stage_rows, stage_rows), :], stage_ref.at[c % 2],
                                     sems.at[c % 2])

    def gate_copy():
        return pltpu.make_async_copy(w_hbm.at[pl.ds(n_main, n_gate), :], gstage_ref, sems.at[2])

    @pl.when(t == 0)
    def _():
        gate_copy().start()
        chunk_copy(0).start()
        tri_ref[...] = jnp.where(_tril_mask(sub), 1.0, 0.0).astype(BF16)
        for c in range(n_chunks):
            if c + 1 < n_chunks:
                chunk_copy(c + 1).start()
            chunk_copy(c).wait()
            wb_ref[:, c * stage_rows:(c + 1) * stage_rows] = stage_ref[c % 2].T.astype(BF16)
        gate_copy().wait()
        w_gate = jnp.concatenate([gstage_ref[...], jnp.zeros((LANES - n_gate, d), F32)], axis=0)
        wfb_ref[...] = w_gate.T.astype(BF16)

    @pl.when(t % tiles_per_seq == 0)
    def _():
        carry_ref[...] = jnp.zeros_like(carry_ref)

    starts = [sum(IN_PROJ_BLOCKS[:i]) for i in range(len(IN_PROJ_BLOCKS))]
    blocks = [slice(r0, r0 + n) for r0, n in zip(starts, IN_PROJ_BLOCKS)]
    hns = []
    for sl in blocks:
        x = x_ref[sl, :]
        ms = jnp.mean(x * x, axis=-1, keepdims=True)
        hns.append((x * lax.rsqrt(ms + EPS) * g_ref[...]).astype(BF16))

    def proj(hn, idx):
        return jnp.dot(hn, wb_ref[:, idx * d:(idx + 1) * d], preferred_element_type=F32)

    for sl, hn in zip(blocks, hns):
        xm_ref[sl, :] = proj(hn, 0).astype(xm_ref.dtype)
    log_f = [_log_sigmoid(jnp.dot(hn, wfb_ref[...], preferred_element_type=F32) + fb_ref[...])
             for hn in hns]
    for sl, hn in zip(blocks, hns):
        gm_ref[sl, :] = _silu(proj(hn, 1)).astype(gm_ref.dtype)
    for sl, hn in zip(blocks, hns):
        gf_ref[sl, :] = _silu(proj(hn, 5)).astype(gf_ref.dtype)
    carry = carry_ref[...]
    for sl, lf in zip(blocks, log_f):
        n = lf.shape[0]
        c = _cumsum_rows_packed(tri_ref[0:n, 0:n], lf, c_ref.shape[0]) + carry
        c_ref[:, sl] = (c * LOG2E).T[0:c_ref.shape[0], :]
        carry = c[n - 1:n, :]
    carry_ref[...] = carry
    for sl, hn in zip(blocks, hns):
        q_ref[sl, :] = (proj(hn, 2) * q_scale).astype(q_ref.dtype)
    for sl, hn in zip(blocks, hns):
        k_ref[sl, :] = proj(hn, 3).astype(k_ref.dtype)
    for sl, hn in zip(blocks, hns):
        v_ref[sl, :] = proj(hn, 4).astype(v_ref.dtype)


def _in_proj(x2, gain, w_in_t, f_bias, seq_len, q_scale):
    t_rows, d = x2.shape
    rows, sub = sum(IN_PROJ_BLOCKS), max(IN_PROJ_BLOCKS)
    assert t_rows % rows == 0 and seq_len % rows == 0
    n_main = 6 * d
    n_gate = w_in_t.shape[0] - n_main
    assert 0 < n_gate <= SUBLANES and w_in_t.shape[1] == d
    row_spec = pl.BlockSpec((rows, d), lambda t: (t, 0))
    const = lambda shape: pl.BlockSpec(shape, lambda t: (0, 0), pipeline_mode=pl.Buffered(1))
    kern = functools.partial(_in_proj_kernel, tiles_per_seq=seq_len // rows, q_scale=q_scale)
    return pl.pallas_call(
        kern,
        grid=(t_rows // rows,),
        in_specs=[row_spec, const((1, d)), pl.BlockSpec(memory_space=pl.ANY), const((1, LANES))],
        out_specs=[row_spec] * 6 + [pl.BlockSpec((SUBLANES, rows), lambda t: (0, t))],
        out_shape=[jax.ShapeDtypeStruct((t_rows, d), BF16)] * 6
        + [jax.ShapeDtypeStruct((SUBLANES, t_rows), F32)],
        scratch_shapes=[pltpu.VMEM((d, n_main), BF16), pltpu.VMEM((d, LANES), BF16),
                        pltpu.VMEM((sub, sub), BF16), pltpu.VMEM((1, LANES), F32),
                        pltpu.VMEM((2, IN_PROJ_STAGE_ROWS, d), F32), pltpu.VMEM((n_gate, d), F32),
                        pltpu.SemaphoreType.DMA((3,))],
        compiler_params=pltpu.CompilerParams(
            dimension_semantics=("arbitrary",), vmem_limit_bytes=VMEM_LIMIT),
        name="in_proj",
    )(x2, gain, w_in_t, f_bias)


def _mlstm_kernel(xm_ref, gm_ref, cw_ref, cb_ref, wq_ref, wk_ref, wv_ref, wg_ref, gb_ref,
                  ng_ref, sk_ref, y_ref,
                  tail_ref, c_sc, n_sc, m_sc, tri_ref, shift_ref, gw_ref, *, k_scale):
    ci = pl.program_id(1)
    n_seqs, L, d_m = xm_ref.shape
    n_heads, dh, _ = wq_ref.shape

    @pl.when(ci == 0)
    def _():
        for h in range(n_heads):
            sl = slice(h * dh, (h + 1) * dh)
            g_q = jnp.dot(wq_ref[h], wg_ref[h * dh:(h + 1) * dh, :], preferred_element_type=F32)
            g_k = jnp.dot(wk_ref[h], wg_ref[d_m + h * dh:d_m + (h + 1) * dh, :],
                          preferred_element_type=F32)
            g_v = jnp.dot(wv_ref[h], wg_ref[2 * d_m + h * dh:2 * d_m + (h + 1) * dh, :],
                          preferred_element_type=F32)
            gw_ref[0, sl, :] = (g_q + g_k * k_scale).astype(BF16)
            gw_ref[1, sl, :] = g_v.astype(BF16)
        c_sc[...] = jnp.zeros_like(c_sc)
        n_sc[...] = jnp.zeros_like(n_sc)
        m_sc[...] = jnp.zeros_like(m_sc)
        tail_ref[...] = jnp.zeros_like(tail_ref)
        row = lax.broadcasted_iota(jnp.int32, (L, L), 0)
        col = lax.broadcasted_iota(jnp.int32, (L, L), 1)
        tri_ref[...] = jnp.where(col <= row, 1.0, 0.0).astype(BF16)
        for j in range(1, CONV_WIDTH):
            shift_ref[j - 1] = jnp.where(col == row - j, 1.0, 0.0).astype(BF16)

    seqs = range(n_seqs)
    fronts = [_mlstm_front(xm_ref.at[i], cw_ref, cb_ref, wq_ref, wk_ref, wv_ref, gw_ref, gb_ref,
                           tail_ref.at[i], tri_ref, shift_ref, k_scale) for i in seqs]
    for i in seqs:
        mixed = _mlstm_mix(*fronts[i][1:], c_sc.at[i], n_sc.at[i], m_sc.at[i])
        _mlstm_out(mixed, fronts[i][0], gm_ref.at[i], ng_ref, sk_ref, y_ref.at[i])


def _mlstm_front(xm_ref, cw_ref, cb_ref, wq_ref, wk_ref, wv_ref, gw_ref, gb_ref,
                 tail_ref, tri_ref, shift_ref, k_scale):
    L, d_m = xm_ref.shape
    dh = wq_ref.shape[1]
    n_heads = wq_ref.shape[0]
    halo = SUBLANES

    x_b = xm_ref[...]
    x_f = x_b.astype(F32)
    conv = cb_ref[...] + x_f * cw_ref[CONV_WIDTH - 1:CONV_WIDTH, :]
    edge = jnp.concatenate([tail_ref[...], jnp.zeros((halo, d_m), F32)], axis=0)
    head = jnp.zeros((halo, d_m), F32)
    for j in range(1, CONV_WIDTH):
        w_j = cw_ref[CONV_WIDTH - 1 - j:CONV_WIDTH - j, :]
        conv = conv + jnp.dot(shift_ref[j - 1], x_b, preferred_element_type=F32) * w_j
        head = head + edge[halo - j:2 * halo - j, :] * w_j
    conv = jnp.concatenate([conv[:halo] + head, conv[halo:]], axis=0)
    tail_ref[...] = x_f[L - halo:L, :]
    xc = _silu(conv)
    xc_b = xc.astype(BF16)

    gates = (jnp.dot(xc_b, gw_ref[0], preferred_element_type=F32)
             + jnp.dot(x_b, gw_ref[1], preferred_element_type=F32) + gb_ref[...])
    qs, ks, vs = [], [], []
    for h in range(n_heads):
        sl = slice(h * dh, (h + 1) * dh)
        q = jnp.dot(xc_b[:, sl], wq_ref[h], preferred_element_type=F32)
        k = jnp.dot(xc_b[:, sl], wk_ref[h], preferred_element_type=F32) * k_scale
        v = jnp.dot(x_b[:, sl], wv_ref[h], preferred_element_type=F32)
        qs.append(q.astype(BF16))
        ks.append((k, k.astype(BF16)))
        vs.append(v.astype(BF16))

    ig = gates[:, :LANES]
    bcum = _cumsum_rows(tri_ref[...], _log_sigmoid(gates[:, LANES:]))
    a_col = ig - bcum
    return xc, qs, ks, vs, a_col, bcum


def _mlstm_mix(qs, ks, vs, a_col, bcum, c_sc, n_sc, m_sc):
    n_heads = len(qs)
    L, dh = qs[0].shape
    a_rows = a_col.T
    causal = _tril_mask(L)
    ones_blk = jnp.ones((L, LANES), BF16)

    heads = range(n_heads)
    nt = (((1,), (1,)), ((), ()))
    m_prev = {h: m_sc[h][:, 0:1] for h in heads}
    dm = {h: jnp.where(causal, a_rows[h:h + 1, :], NEG_INF) for h in heads}
    m_run = {h: jnp.maximum(m_prev[h], jnp.max(dm[h], axis=1, keepdims=True)) for h in heads}
    w_inter = {h: jnp.exp(m_prev[h] - m_run[h]) for h in heads}
    s_b = {h: (lax.dot_general(qs[h], ks[h][1], nt, preferred_element_type=F32)
               * jnp.exp(dm[h] - m_run[h])).astype(BF16) for h in heads}
    num = {h: w_inter[h] * jnp.dot(qs[h], c_sc[h].astype(BF16), preferred_element_type=F32)
           + jnp.dot(s_b[h], vs[h], preferred_element_type=F32) for h in heads}
    qn = {h: lax.dot_general(qs[h], jnp.broadcast_to(n_sc[h], (LANES, dh)).astype(BF16), nt,
                             preferred_element_type=F32) for h in heads}
    den = {h: w_inter[h] * qn[h] + jnp.dot(s_b[h], ones_blk, preferred_element_type=F32)
           for h in heads}
    m_t = {h: bcum[:, h:h + 1] + m_run[h] for h in heads}
    inv = {h: 1.0 / jnp.maximum(jnp.abs(den[h]), jnp.exp(-m_t[h])) for h in heads}
    hh = [num[h] * jnp.concatenate([inv[h]] * (dh // LANES), axis=1) for h in heads]

    for h in heads:
        m_last = m_run[h][L - 1:L, :]
        decay = jnp.exp(m_prev[h] - m_last)
        kw = ks[h][0] * jnp.exp(a_col[:, h:h + 1] - m_last)
        c_sc[h] = decay * c_sc[h] + lax.dot_general(
            kw.astype(BF16), vs[h], (((0,), (0,)), ((), ())), preferred_element_type=F32)
        n_sc[h] = decay * n_sc[h] + jnp.sum(kw, axis=0, keepdims=True)
        m_sc[h] = jnp.broadcast_to(m_t[h][L - 1:L, :], m_sc.shape[1:])
    return hh


def _mlstm_out(hh, xc, gm_ref, ng_ref, sk_ref, y_ref):
    heads = range(len(hh))
    dh = hh[0].shape[1]
    mu = [jnp.mean(hh[h], axis=1, keepdims=True) for h in heads]
    cen = [hh[h] - mu[h] for h in heads]
    var = [jnp.mean(cen[h] * cen[h], axis=1, keepdims=True) for h in heads]
    for h in heads:
        sl = slice(h * dh, (h + 1) * dh)
        hn = cen[h] * lax.rsqrt(var[h] + EPS) * ng_ref[:, sl] + sk_ref[:, sl] * xc[:, sl]
        y_ref[:, sl] = (hn * gm_ref[:, sl].astype(F32)).astype(y_ref.dtype)


def _mlstm(xm, gm, conv_w, conv_b, wq, wk, wv, wg, gb, ngain, skip):
    b, s, d = xm.shape
    L = MLSTM_CHUNK
    assert s % L == 0
    n_heads, dh, _ = wq.shape
    n_seqs = MLSTM_SEQS
    assert b % n_seqs == 0
    seq_spec = pl.BlockSpec((n_seqs, L, d), lambda bi, ci: (bi, ci, 0))

    def const(shape):
        nd = len(shape)
        return pl.BlockSpec(shape, lambda bi, ci: (0,) * nd)

    kern = functools.partial(_mlstm_kernel, k_scale=dh ** -0.5)
    return pl.pallas_call(
        kern,
        grid=(b // n_seqs, s // L),
        in_specs=[seq_spec, seq_spec, const(conv_w.shape), const(conv_b.shape), const(wq.shape),
                  const(wk.shape), const(wv.shape), const(wg.shape), const(gb.shape),
                  const(ngain.shape), const(skip.shape)],
        out_specs=seq_spec,
        out_shape=jax.ShapeDtypeStruct((b, s, d), BF16),
        scratch_shapes=[
            pltpu.VMEM((n_seqs, SUBLANES, d), F32),
            pltpu.VMEM((n_seqs, n_heads, dh, dh), F32),
            pltpu.VMEM((n_seqs, n_heads, 1, dh), F32),
            pltpu.VMEM((n_seqs, n_heads, 1, LANES), F32),
            pltpu.VMEM((L, L), BF16),
            pltpu.VMEM((CONV_WIDTH - 1, L, L), BF16),
            pltpu.VMEM((2, d, wg.shape[1]), BF16),
        ],
        compiler_params=pltpu.CompilerParams(
            dimension_semantics=("arbitrary", "arbitrary"), vmem_limit_bytes=VMEM_LIMIT),
        name="mlstm",
    )(xm, gm, conv_w, conv_b, wq, wk, wv, wg, gb, ngain, skip)


def _fox_kernel(q_ref, k_ref, v_ref, cr_ref, gf_ref, g_ref, o_ref):
    s_len = q_ref.shape[0]
    dh = LANES
    n_local = q_ref.shape[1] // dh
    tq = FOX_Q
    causal = _tril_mask(tq)
    head0 = pl.program_id(1) * n_local

    def scores(unit):
        hl, qi = unit
        r0 = qi * tq
        cols = slice(hl * dh, (hl + 1) * dh)
        return lax.dot_general(q_ref[r0:r0 + tq, cols], k_ref[0:r0 + tq, cols],
                               (((1,), (1,)), ((), ())), preferred_element_type=F32)

    order = [(hl, qi) for hl in range(n_local) for qi in reversed(range(s_len // tq))]
    ahead = [scores(u) for u in order[:FOX_LOOKAHEAD]]
    for pos, (hl, qi) in enumerate(order):
        r0 = qi * tq
        kv = r0 + tq
        cols = slice(hl * dh, (hl + 1) * dh)
        s = ahead.pop(0)
        if pos + FOX_LOOKAHEAD < len(order):
            ahead.append(scores(order[pos + FOX_LOOKAHEAD]))
        s = s - cr_ref[pl.ds(head0 + hl, 1), 0:kv]
        s_diag = jnp.where(causal, s[:, r0:], NEG_INF)
        s = s_diag if qi == 0 else jnp.concatenate([s[:, :r0], s_diag], axis=1)
        m = jnp.max(s, axis=1, keepdims=True)
        p = jnp.exp2(s - m).astype(BF16)
        v_ext = jnp.concatenate([v_ref[0:kv, cols], jnp.ones((kv, dh), BF16)], axis=1)
        acc = jnp.dot(p, v_ext, preferred_element_type=F32)

        out = acc[:, :dh] / acc[:, dh:]
        ms = jnp.mean(out * out, axis=1, keepdims=True)
        o_ref[r0:kv, cols] = (out * lax.rsqrt(ms + EPS) * g_ref[:, cols]
                              * gf_ref[r0:kv, cols].astype(F32)).astype(o_ref.dtype)


def _fox(q, k, v, c_row, gf, gain):
    b, s, d = q.shape
    n_heads = N_FOX_HEADS
    dh = d // n_heads
    assert dh == LANES and s % FOX_Q == 0 and n_heads % FOX_HEADS == 0
    width = FOX_HEADS * dh
    head_spec = pl.BlockSpec((None, s, width), lambda bi, hi: (bi, 0, hi))
    return pl.pallas_call(
        _fox_kernel,
        grid=(b, n_heads // FOX_HEADS),
        in_specs=[head_spec, head_spec, head_spec,
                  pl.BlockSpec((c_row.shape[0], s), lambda bi, hi: (0, bi)),
                  head_spec,
                  pl.BlockSpec((1, width), lambda bi, hi: (0, hi))],
        out_specs=head_spec,
        out_shape=jax.ShapeDtypeStruct((b, s, d), BF16),
        compiler_params=pltpu.CompilerParams(
            dimension_semantics=("arbitrary", "arbitrary"), vmem_limit_bytes=VMEM_LIMIT),
        name="fox",
    )(q, k, v, c_row, gf, gain)


def _out_proj_kernel(ym_hbm, yf_hbm, x_hbm, w_hbm, g_ref, o_hbm,
                     wb_ref, ym_buf, yf_buf, x_buf, o_buf, r_ref, in_sems, out_sems, *, final_norm):
    d_m = ym_buf.shape[2]
    rows = x_buf.shape[1]
    n_blocks = x_hbm.shape[0] // rows
    grp = o_buf.shape[0] // 2
    n_groups = (n_blocks - 1) // grp
    first_slot = 2 * grp

    def block_rows(b):
        start = b * rows
        return pl.ds(start if isinstance(start, int) else pl.multiple_of(start, rows), rows)

    def in_copies(b, slot):
        return (pltpu.make_async_copy(ym_hbm.at[block_rows(b), :], ym_buf.at[slot], in_sems.at[0, slot]),
                pltpu.make_async_copy(yf_hbm.at[block_rows(b), :], yf_buf.at[slot], in_sems.at[1, slot]),
                pltpu.make_async_copy(x_hbm.at[block_rows(b), :], x_buf.at[slot], in_sems.at[2, slot]))

    def out_copy(b, slot):
        return pltpu.make_async_copy(o_buf.at[slot], o_hbm.at[block_rows(b), :], out_sems.at[slot])

    def group_in(i, start):
        half = (i % 2) * grp
        for u in range(grp):
            b = 1 + grp * i + u
            b = min(b, n_blocks - 1) if isinstance(b, int) else jnp.minimum(b, n_blocks - 1)
            for c in in_copies(b, half + u):
                c.start() if start else c.wait()

    def matmul_residual(slot):
        y = jnp.dot(ym_buf[slot], wb_ref[0:d_m, :], preferred_element_type=F32)
        y = y + jnp.dot(yf_buf[slot], wb_ref[d_m:, :], preferred_element_type=F32)
        return x_buf[slot] + y

    def finish(r, slot):
        if final_norm:
            ms = jnp.mean(r * r, axis=-1, keepdims=True)
            r = r * lax.rsqrt(ms + EPS) * g_ref[...]
        o_buf[slot] = r

    def w_copy(s):
        return pltpu.make_async_copy(w_hbm.at[block_rows(s), :], x_buf.at[s], in_sems.at[2, s])

    n_w = w_hbm.shape[0] // rows
    for s in range(n_w):
        w_copy(s).start()
    for s in range(n_w):
        w_copy(s).wait()
        wb_ref[s * rows:(s + 1) * rows, :] = x_buf[s].astype(BF16)

    for c in in_copies(0, first_slot):
        c.start()
    group_in(0, True)
    o_buf[...] = jnp.zeros_like(o_buf)
    for s in range(2 * grp):
        out_copy(s, s).start()
    for c in in_copies(0, first_slot):
        c.wait()
    r_ref[...] = matmul_residual(first_slot)

    def group(i, _):
        half = (i % 2) * grp
        group_in(i + 1, True)
        group_in(i, False)
        for u in range(grp):
            out_copy(0, half + u).wait()
        pending = r_ref[...]
        for u in range(grp):
            r = matmul_residual(half + u)
            finish(pending, half + u)
            pending = r
        r_ref[...] = pending
        for u in range(grp):
            out_copy(grp * i + u, half + u).start()
        return 0

    lax.fori_loop(0, n_groups, group, 0)
    last_slot = (n_groups % 2) * grp
    out_copy(0, last_slot).wait()
    finish(r_ref[...], last_slot)
    out_copy(n_blocks - 1, last_slot).start()
    group_in(n_groups, False)
    for s in range(2 * grp):
        out_copy(0, s).wait()


def _out_proj(ym, yf, x2, w_out, gain, final_norm):
    t_rows, d = x2.shape
    rows, grp = OUT_PROJ_SUB, OUT_PROJ_GROUP
    assert t_rows % rows == 0 and (t_rows // rows - 1) % grp == 0 and t_rows // rows > grp
    n_in = 2 * grp + 1
    assert w_out.shape[0] % rows == 0 and w_out.shape[0] // rows <= n_in and w_out.shape[1] == d
    hbm = pl.BlockSpec(memory_space=pl.ANY)
    const = lambda shape: pl.BlockSpec(shape, lambda t: (0, 0), pipeline_mode=pl.Buffered(1))
    kern = functools.partial(_out_proj_kernel, final_norm=final_norm)
    return pl.pallas_call(
        kern,
        grid=(1,),
        in_specs=[hbm, hbm, hbm, hbm, const((1, d))],
        out_specs=hbm,
        out_shape=jax.ShapeDtypeStruct((t_rows, d), F32),
        scratch_shapes=[pltpu.VMEM(w_out.shape, BF16),
                        pltpu.VMEM((n_in, rows, ym.shape[1]), BF16),
                        pltpu.VMEM((n_in, rows, yf.shape[1]), BF16),
                        pltpu.VMEM((n_in, rows, d), F32),
                        pltpu.VMEM((2 * grp, rows, d), F32),
                        pltpu.VMEM((rows, d), F32),
                        pltpu.SemaphoreType.DMA((3, n_in)), pltpu.SemaphoreType.DMA((2 * grp,))],
        compiler_params=pltpu.CompilerParams(
            dimension_semantics=("arbitrary",), vmem_limit_bytes=VMEM_LIMIT),
        name="out_proj",
    )(ym, yf, x2, w_out, gain)


def _block_diag(w, dh):
    nb, bs, _ = w.shape
    d = nb * bs
    tiled = jnp.broadcast_to(w.reshape(d, 1, bs), (d, dh // bs, bs)).reshape(d, dh)
    r = lax.broadcasted_iota(jnp.int32, (d, dh), 0)
    c = lax.broadcasted_iota(jnp.int32, (d, dh), 1)
    keep = (r % dh) // bs == c // bs
    return jnp.where(keep, tiled, 0.0).reshape(d // dh, dh, dh)


def kernel(x, norm_gain, w_in, conv_w, conv_b, w_q_m, w_k_m, w_v_m, w_igate, b_igate, w_fgate,
           b_fgate, mlstm_norm_gain, mlstm_skip, fox_forget_bias, fox_norm_gain, w_out,
           final_norm_gain):
    b, s, d = x.shape
    depth = norm_gain.shape[0]
    d_m = conv_w.shape[-1]
    d_f = fox_norm_gain.shape[-1]
    dh_m = d_m // N_MLSTM_HEADS
    dh_f = d_f // N_FOX_HEADS
    n_main = 2 * d_m + 4 * d_f
    assert d_m == d and d_f == d

    x2 = x.reshape(b * s, d)
    for l in range(depth):
        assert w_in.shape[2] == n_main + N_FOX_HEADS
        f_bias = jnp.pad(fox_forget_bias[l], (0, LANES - N_FOX_HEADS)).reshape(1, LANES)
        xm, gm, qf, kf, vf, gf, c_keys = _in_proj(
            x2, norm_gain[l].reshape(1, d), w_in[l].T, f_bias, s, dh_f ** -0.5 * LOG2E)

        lane_pad = ((0, 0), (0, LANES - N_MLSTM_HEADS))
        wg = jnp.concatenate([jnp.pad(w_igate[l], lane_pad), jnp.pad(w_fgate[l], lane_pad)], axis=1)
        gb = jnp.concatenate([jnp.pad(b_igate[l][None], lane_pad),
                              jnp.pad(b_fgate[l][None], lane_pad)], axis=1)
        y_m = _mlstm(xm.reshape(b, s, d_m), gm.reshape(b, s, d_m), conv_w[l],
                     conv_b[l].reshape(1, d_m),
                     _block_diag(w_q_m[l], dh_m).astype(BF16),
                     _block_diag(w_k_m[l], dh_m).astype(BF16),
                     _block_diag(w_v_m[l], dh_m).astype(BF16),
                     wg.astype(BF16), gb,
                     mlstm_norm_gain[l].reshape(1, d_m), mlstm_skip[l].reshape(1, d_m))

        y_f = _fox(qf.reshape(b, s, d_f), kf.reshape(b, s, d_f), vf.reshape(b, s, d_f), c_keys,
                   gf.reshape(b, s, d_f), fox_norm_gain[l].reshape(1, d_f))

        x2 = _out_proj(y_m.reshape(b * s, d_m), y_f.reshape(b * s, d_f), x2, w_out[l],
                       final_norm_gain.reshape(1, d), final_norm=(l == depth - 1))
    return x2.reshape(b, s, d)
```

```python
import functools
import math

import jax
import jax.numpy as jnp
from jax import lax
from jax.experimental import pallas as pl
from jax.experimental.pallas import tpu as pltpu

EPS = 1e-6
N_MLSTM_HEADS = 4
N_FOX_HEADS = 8
CONV_WIDTH = 4

LANES = 128
SUBLANES = 8
VMEM_LIMIT = 56 * 1024 * 1024

IN_PROJ_BLOCKS = (256, 256, 256, 256)
IN_PROJ_STAGE_ROWS = 512
OUT_PROJ_SUB = 256
OUT_PROJ_GROUP = 3
MLSTM_CHUNK = 256
MLSTM_SEQS = 2
FOX_HEADS = 2
FOX_LOOKAHEAD = 2
FOX_Q = 256
LOG2E = math.log2(math.e)

BF16 = jnp.bfloat16
F32 = jnp.float32
NEG_INF = float("-inf")


def _log_sigmoid(z):
    return jnp.minimum(z, 0.0) - jnp.log(1.0 + jnp.exp(-jnp.abs(z)))


def _sigmoid(z):
    return 1.0 / (1.0 + jnp.exp(-z))


def _silu(z):
    return z * _sigmoid(z)


def _split3(v):
    hi = v.astype(BF16)
    r = v - hi.astype(F32)
    mid = r.astype(BF16)
    lo = (r - mid.astype(F32)).astype(BF16)
    return hi, mid, lo


def _cumsum_rows(tri, v):
    w = v.shape[1]
    cat = jnp.concatenate(_split3(v), axis=1)
    cs = jnp.dot(tri, cat, preferred_element_type=F32)
    return cs[:, :w] + cs[:, w:2 * w] + cs[:, 2 * w:]


def _cumsum_rows_packed(tri, v, width):
    lanes = v.shape[1]
    assert 3 * width <= lanes
    lane = lax.broadcasted_iota(jnp.int32, v.shape, 1)
    hi, mid, lo = (t.astype(F32) for t in _split3(jnp.where(lane < width, v, 0.0)))
    packed = hi + pltpu.roll(mid, width, axis=1) + pltpu.roll(lo, 2 * width, axis=1)
    cs = jnp.dot(tri, packed.astype(BF16), preferred_element_type=F32)
    return cs + pltpu.roll(cs, lanes - width, axis=1) + pltpu.roll(cs, lanes - 2 * width, axis=1)


def _tril_mask(n):
    row = lax.broadcasted_iota(jnp.int32, (n, n), 0)
    col = lax.broadcasted_iota(jnp.int32, (n, n), 1)
    return col <= row


def _in_proj_kernel(x_ref, g_ref, w_hbm, fb_ref,
                    xm_ref, gm_ref, q_ref, k_ref, v_ref, gf_ref, c_ref,
                    wb_ref, wfb_ref, tri_ref, carry_ref, stage_ref, gstage_ref, sems,
                    *, tiles_per_seq, q_scale):
    t = pl.program_id(0)
    rows, d = x_ref.shape
    sub = tri_ref.shape[0]
    n_main = wb_ref.shape[1]
    n_gate = gstage_ref.shape[0]
    stage_rows = stage_ref.shape[1]
    n_chunks = n_main // stage_rows

    def chunk_copy(c):
        return pltpu.make_async_copy(w_hbm.at[pl.ds(c * stage_rows, stage_rows), :], stage_ref.at[c % 2],
                                     sems.at[c % 2])

    def gate_copy():
        return pltpu.make_async_copy(w_hbm.at[pl.ds(n_main, n_gate), :], gstage_ref, sems.at[2])

    @pl.when(t == 0)
    def _():
        gate_copy().start()
        chunk_copy(0).start()
        tri_ref[...] = jnp.where(_tril_mask(sub), 1.0, 0.0).astype(BF16)
        for c in range(n_chunks):
            if c + 1 < n_chunks:
                chunk_copy(c + 1).start()
            chunk_copy(c).wait()
            wb_ref[:, c * stage_rows:(c + 1) * stage_rows] = stage_ref[c % 2].T.astype(BF16)
        gate_copy().wait()
        w_gate = jnp.concatenate([gstage_ref[...], jnp.zeros((LANES - n_gate, d), F32)], axis=0)
        wfb_ref[...] = w_gate.T.astype(BF16)

    @pl.when(t % tiles_per_seq == 0)
    def _():
        carry_ref[...] = jnp.zeros_like(carry_ref)

    starts = [sum(IN_PROJ_BLOCKS[:i]) for i in range(len(IN_PROJ_BLOCKS))]
    blocks = [slice(r0, r0 + n) for r0, n in zip(starts, IN_PROJ_BLOCKS)]
    hns = []
    for sl in blocks:
        x = x_ref[sl, :]
        ms = jnp.mean(x * x, axis=-1, keepdims=True)
        hns.append((x * lax.rsqrt(ms + EPS) * g_ref[...]).astype(BF16))

    def proj(hn, idx):
        return jnp.dot(hn, wb_ref[:, idx * d:(idx + 1) * d], preferred_element_type=F32)

    for sl, hn in zip(blocks, hns):
        xm_ref[sl, :] = proj(hn, 0).astype(xm_ref.dtype)
    log_f = [_log_sigmoid(jnp.dot(hn, wfb_ref[...], preferred_element_type=F32) + fb_ref[...])
             for hn in hns]
    for sl, hn in zip(blocks, hns):
        gm_ref[sl, :] = _silu(proj(hn, 1)).astype(gm_ref.dtype)
    for sl, hn in zip(blocks, hns):
        gf_ref[sl, :] = _silu(proj(hn, 5)).astype(gf_ref.dtype)
    carry = carry_ref[...]
    for sl, lf in zip(blocks, log_f):
        n = lf.shape[0]
        c = _cumsum_rows_packed(tri_ref[0:n, 0:n], lf, c_ref.shape[0]) + carry
        c_ref[:, sl] = (c * LOG2E).T[0:c_ref.shape[0], :]
        carry = c[n - 1:n, :]
    carry_ref[...] = carry
    for sl, hn in zip(blocks, hns):
        q_ref[sl, :] = (proj(hn, 2) * q_scale).astype(q_ref.dtype)
    for sl, hn in zip(blocks, hns):
        k_ref[sl, :] = proj(hn, 3).astype(k_ref.dtype)
    for sl, hn in zip(blocks, hns):
        v_ref[sl, :] = proj(hn, 4).astype(v_ref.dtype)


def _in_proj(x2, gain, w_in_t, f_bias, seq_len, q_scale):
    t_rows, d = x2.shape
    rows, sub = sum(IN_PROJ_BLOCKS), max(IN_PROJ_BLOCKS)
    assert t_rows % rows == 0 and seq_len % rows == 0
    n_main = 6 * d
    n_gate = w_in_t.shape[0] - n_main
    assert 0 < n_gate <= SUBLANES and w_in_t.shape[1] == d
    row_spec = pl.BlockSpec((rows, d), lambda t: (t, 0))
    const = lambda shape: pl.BlockSpec(shape, lambda t: (0, 0), pipeline_mode=pl.Buffered(1))
    kern = functools.partial(_in_proj_kernel, tiles_per_seq=seq_len // rows, q_scale=q_scale)
    return pl.pallas_call(
        kern,
        grid=(t_rows // rows,),
        in_specs=[row_spec, const((1, d)), pl.BlockSpec(memory_space=pl.ANY), const((1, LANES))],
        out_specs=[row_spec] * 6 + [pl.BlockSpec((SUBLANES, rows), lambda t: (0, t))],
        out_shape=[jax.ShapeDtypeStruct((t_rows, d), BF16)] * 6
        + [jax.ShapeDtypeStruct((SUBLANES, t_rows), F32)],
        scratch_shapes=[pltpu.VMEM((d, n_main), BF16), pltpu.VMEM((d, LANES), BF16),
                        pltpu.VMEM((sub, sub), BF16), pltpu.VMEM((1, LANES), F32),
                        pltpu.VMEM((2, IN_PROJ_STAGE_ROWS, d), F32), pltpu.VMEM((n_gate, d), F32),
                        pltpu.SemaphoreType.DMA((3,))],
        compiler_params=pltpu.CompilerParams(
            dimension_semantics=("arbitrary",), vmem_limit_bytes=VMEM_LIMIT),
        name="in_proj",
    )(x2, gain, w_in_t, f_bias)


def _mlstm_kernel(xm_ref, gm_ref, cw_ref, cb_ref, wq_ref, wk_ref, wv_ref, wg_ref, gb_ref,
                  ng_ref, sk_ref, y_ref,
                  tail_ref, c_sc, n_sc, m_sc, tri_ref, shift_ref, gw_ref, *, k_scale):
    ci = pl.program_id(1)
    n_seqs, L, d_m = xm_ref.shape
    n_heads, dh, _ = wq_ref.shape

    @pl.when(ci == 0)
    def _():
        for h in range(n_heads):
            sl = slice(h * dh, (h + 1) * dh)
            g_q = jnp.dot(wq_ref[h], wg_ref[h * dh:(h + 1) * dh, :], preferred_element_type=F32)
            g_k = jnp.dot(wk_ref[h], wg_ref[d_m + h * dh:d_m + (h + 1) * dh, :],
                          preferred_element_type=F32)
            g_v = jnp.dot(wv_ref[h], wg_ref[2 * d_m + h * dh:2 * d_m + (h + 1) * dh, :],
                          preferred_element_type=F32)
            gw_ref[0, sl, :] = (g_q + g_k * k_scale).astype(BF16)
            gw_ref[1, sl, :] = g_v.astype(BF16)
        c_sc[...] = jnp.zeros_like(c_sc)
        n_sc[...] = jnp.zeros_like(n_sc)
        m_sc[...] = jnp.zeros_like(m_sc)
        tail_ref[...] = jnp.zeros_like(tail_ref)
        row = lax.broadcasted_iota(jnp.int32, (L, L), 0)
        col = lax.broadcasted_iota(jnp.int32, (L, L), 1)
        tri_ref[...] = jnp.where(col <= row, 1.0, 0.0).astype(BF16)
        for j in range(1, CONV_WIDTH):
            shift_ref[j - 1] = jnp.where(col == row - j, 1.0, 0.0).astype(BF16)

    seqs = range(n_seqs)
    fronts = [_mlstm_front(xm_ref.at[i], cw_ref, cb_ref, wq_ref, wk_ref, wv_ref, gw_ref, gb_ref,
                           tail_ref.at[i], tri_ref, shift_ref, k_scale) for i in seqs]
    for i in seqs:
        mixed = _mlstm_mix(*fronts[i][1:], c_sc.at[i], n_sc.at[i], m_sc.at[i])
        _mlstm_out(mixed, fronts[i][0], gm_ref.at[i], ng_ref, sk_ref, y_ref.at[i])


def _mlstm_front(xm_ref, cw_ref, cb_ref, wq_ref, wk_ref, wv_ref, gw_ref, gb_ref,
                 tail_ref, tri_ref, shift_ref, k_scale):
    L, d_m = xm_ref.shape
    dh = wq_ref.shape[1]
    n_heads = wq_ref.shape[0]
    halo = SUBLANES

    x_b = xm_ref[...]
    x_f = x_b.astype(F32)
    conv = cb_ref[...] + x_f * cw_ref[CONV_WIDTH - 1:CONV_WIDTH, :]
    edge = jnp.concatenate([tail_ref[...], jnp.zeros((halo, d_m), F32)], axis=0)
    head = jnp.zeros((halo, d_m), F32)
    for j in range(1, CONV_WIDTH):
        w_j = cw_ref[CONV_WIDTH - 1 - j:CONV_WIDTH - j, :]
        conv = conv + jnp.dot(shift_ref[j - 1], x_b, preferred_element_type=F32) * w_j
        head = head + edge[halo - j:2 * halo - j, :] * w_j
    conv = jnp.concatenate([conv[:halo] + head, conv[halo:]], axis=0)
    tail_ref[...] = x_f[L - halo:L, :]
    xc = _silu(conv)
    xc_b = xc.astype(BF16)

    gates = (jnp.dot(xc_b, gw_ref[0], preferred_element_type=F32)
             + jnp.dot(x_b, gw_ref[1], preferred_element_type=F32) + gb_ref[...])
    qs, ks, vs = [], [], []
    for h in range(n_heads):
        sl = slice(h * dh, (h + 1) * dh)
        q = jnp.dot(xc_b[:, sl], wq_ref[h], preferred_element_type=F32)
        k = jnp.dot(xc_b[:, sl], wk_ref[h], preferred_element_type=F32) * k_scale
        v = jnp.dot(x_b[:, sl], wv_ref[h], preferred_element_type=F32)
        qs.append(q.astype(BF16))
        ks.append((k, k.astype(BF16)))
        vs.append(v.astype(BF16))

    ig = gates[:, :LANES]
    bcum = _cumsum_rows(tri_ref[...], _log_sigmoid(gates[:, LANES:]))
    a_col = ig - bcum
    return xc, qs, ks, vs, a_col, bcum


def _mlstm_mix(qs, ks, vs, a_col, bcum, c_sc, n_sc, m_sc):
    n_heads = len(qs)
    L, dh = qs[0].shape
    a_rows = a_col.T
    causal = _tril_mask(L)
    ones_blk = jnp.ones((L, LANES), BF16)

    heads = range(n_heads)
    nt = (((1,), (1,)), ((), ()))
    m_prev = {h: m_sc[h][:, 0:1] for h in heads}
    dm = {h: jnp.where(causal, a_rows[h:h + 1, :], NEG_INF) for h in heads}
    m_run = {h: jnp.maximum(m_prev[h], jnp.max(dm[h], axis=1, keepdims=True)) for h in heads}
    w_inter = {h: jnp.exp(m_prev[h] - m_run[h]) for h in heads}
    s_b = {h: (lax.dot_general(qs[h], ks[h][1], nt, preferred_element_type=F32)
               * jnp.exp(dm[h] - m_run[h])).astype(BF16) for h in heads}
    num = {h: w_inter[h] * jnp.dot(qs[h], c_sc[h].astype(BF16), preferred_element_type=F32)
           + jnp.dot(s_b[h], vs[h], preferred_element_type=F32) for h in heads}
    qn = {h: lax.dot_general(qs[h], jnp.broadcast_to(n_sc[h], (LANES, dh)).astype(BF16), nt,
                             preferred_element_type=F32) for h in heads}
    den = {h: w_inter[h] * qn[h] + jnp.dot(s_b[h], ones_blk, preferred_element_type=F32)
           for h in heads}
    m_t = {h: bcum[:, h:h + 1] + m_run[h] for h in heads}
    inv = {h: 1.0 / jnp.maximum(jnp.abs(den[h]), jnp.exp(-m_t[h])) for h in heads}
    hh = [num[h] * jnp.concatenate([inv[h]] * (dh // LANES), axis=1) for h in heads]

    for h in heads:
        m_last = m_run[h][L - 1:L, :]
        decay = jnp.exp(m_prev[h] - m_last)
        kw = ks[h][0] * jnp.exp(a_col[:, h:h + 1] - m_last)
        c_sc[h] = decay * c_sc[h] + lax.dot_general(
            kw.astype(BF16), vs[h], (((0,), (0,)), ((), ())), preferred_element_type=F32)
        n_sc[h] = decay * n_sc[h] + jnp.sum(kw, axis=0, keepdims=True)
        m_sc[h] = jnp.broadcast_to(m_t[h][L - 1:L, :], m_sc.shape[1:])
    return hh


def _mlstm_out(hh, xc, gm_ref, ng_ref, sk_ref, y_ref):
    heads = range(len(hh))
    dh = hh[0].shape[1]
    mu = [jnp.mean(hh[h], axis=1, keepdims=True) for h in heads]
    cen = [hh[h] - mu[h] for h in heads]
    var = [jnp.mean(cen[h] * cen[h], axis=1, keepdims=True) for h in heads]
    for h in heads:
        sl = slice(h * dh, (h + 1) * dh)
        hn = cen[h] * lax.rsqrt(var[h] + EPS) * ng_ref[:, sl] + sk_ref[:, sl] * xc[:, sl]
        y_ref[:, sl] = (hn * gm_ref[:, sl].astype(F32)).astype(y_ref.dtype)


def _mlstm(xm, gm, conv_w, conv_b, wq, wk, wv, wg, gb, ngain, skip):
    b, s, d = xm.shape
    L = MLSTM_CHUNK
    assert s % L == 0
    n_heads, dh, _ = wq.shape
    n_seqs = MLSTM_SEQS
    assert b % n_seqs == 0
    seq_spec = pl.BlockSpec((n_seqs, L, d), lambda bi, ci: (bi, ci, 0))

    def const(shape):
        nd = len(shape)
        return pl.BlockSpec(shape, lambda bi, ci: (0,) * nd)

    kern = functools.partial(_mlstm_kernel, k_scale=dh ** -0.5)
    return pl.pallas_call(
        kern,
        grid=(b // n_seqs, s // L),
        in_specs=[seq_spec, seq_spec, const(conv_w.shape), const(conv_b.shape), const(wq.shape),
                  const(wk.shape), const(wv.shape), const(wg.shape), const(gb.shape),
                  const(ngain.shape), const(skip.shape)],
        out_specs=seq_spec,
        out_shape=jax.ShapeDtypeStruct((b, s, d), BF16),
        scratch_shapes=[
            pltpu.VMEM((n_seqs, SUBLANES, d), F32),
            pltpu.VMEM((n_seqs, n_heads, dh, dh), F32),
            pltpu.VMEM((n_seqs, n_heads, 1, dh), F32),
            pltpu.VMEM((n_seqs, n_heads, 1, LANES), F32),
            pltpu.VMEM((L, L), BF16),
            pltpu.VMEM((CONV_WIDTH - 1, L, L), BF16),
            pltpu.VMEM((2, d, wg.shape[1]), BF16),
        ],
        compiler_params=pltpu.CompilerParams(
            dimension_semantics=("arbitrary", "arbitrary"), vmem_limit_bytes=VMEM_LIMIT),
        name="mlstm",
    )(xm, gm, conv_w, conv_b, wq, wk, wv, wg, gb, ngain, skip)


def _fox_kernel(q_ref, k_ref, v_ref, cr_ref, gf_ref, g_ref, o_ref):
    s_len = q_ref.shape[0]
    dh = LANES
    n_local = q_ref.shape[1] // dh
    tq = FOX_Q
    causal = _tril_mask(tq)
    head0 = pl.program_id(1) * n_local

    def scores(unit):
        hl, qi = unit
        r0 = qi * tq
        cols = slice(hl * dh, (hl + 1) * dh)
        return lax.dot_general(q_ref[r0:r0 + tq, cols], k_ref[0:r0 + tq, cols],
                               (((1,), (1,)), ((), ())), preferred_element_type=F32)

    order = [(hl, qi) for hl in range(n_local) for qi in reversed(range(s_len // tq))]
    ahead = [scores(u) for u in order[:FOX_LOOKAHEAD]]
    for pos, (hl, qi) in enumerate(order):
        r0 = qi * tq
        kv = r0 + tq
        cols = slice(hl * dh, (hl + 1) * dh)
        s = ahead.pop(0)
        if pos + FOX_LOOKAHEAD < len(order):
            ahead.append(scores(order[pos + FOX_LOOKAHEAD]))
        s = s - cr_ref[pl.ds(head0 + hl, 1), 0:kv]
        s_diag = jnp.where(causal, s[:, r0:], NEG_INF)
        s = s_diag if qi == 0 else jnp.concatenate([s[:, :r0], s_diag], axis=1)
        m = jnp.max(s, axis=1, keepdims=True)
        p = jnp.exp2(s - m).astype(BF16)
        v_ext = jnp.concatenate([v_ref[0:kv, cols], jnp.ones((kv, dh), BF16)], axis=1)
        acc = jnp.dot(p, v_ext, preferred_element_type=F32)

        out = acc[:, :dh] / acc[:, dh:]
        ms = jnp.mean(out * out, axis=1, keepdims=True)
        o_ref[r0:kv, cols] = (out * lax.rsqrt(ms + EPS) * g_ref[:, cols]
                              * gf_ref[r0:kv, cols].astype(F32)).astype(o_ref.dtype)


def _fox(q, k, v, c_row, gf, gain):
    b, s, d = q.shape
    n_heads = N_FOX_HEADS
    dh = d // n_heads
    assert dh == LANES and s % FOX_Q == 0 and n_heads % FOX_HEADS == 0
    width = FOX_HEADS * dh
    head_spec = pl.BlockSpec((None, s, width), lambda bi, hi: (bi, 0, hi))
    return pl.pallas_call(
        _fox_kernel,
        grid=(b, n_heads // FOX_HEADS),
        in_specs=[head_spec, head_spec, head_spec,
                  pl.BlockSpec((c_row.shape[0], s), lambda bi, hi: (0, bi)),
                  head_spec,
                  pl.BlockSpec((1, width), lambda bi, hi: (0, hi))],
        out_specs=head_spec,
        out_shape=jax.ShapeDtypeStruct((b, s, d), BF16),
        compiler_params=pltpu.CompilerParams(
            dimension_semantics=("arbitrary", "arbitrary"), vmem_limit_bytes=VMEM_LIMIT),
        name="fox",
    )(q, k, v, c_row, gf, gain)


def _out_proj_kernel(ym_hbm, yf_hbm, x_hbm, w_hbm, g_ref, o_hbm,
                     wb_ref, ym_buf, yf_buf, x_buf, o_buf, r_ref, in_sems, out_sems, *, final_norm):
    d_m = ym_buf.shape[2]
    rows = x_buf.shape[1]
    n_blocks = x_hbm.shape[0] // rows
    grp = o_buf.shape[0] // 2
    n_groups = (n_blocks - 1) // grp
    first_slot = 2 * grp

    def block_rows(b):
        start = b * rows
        return pl.ds(start if isinstance(start, int) else pl.multiple_of(start, rows), rows)

    def in_copies(b, slot):
        return (pltpu.make_async_copy(ym_hbm.at[block_rows(b), :], ym_buf.at[slot], in_sems.at[0, slot]),
                pltpu.make_async_copy(yf_hbm.at[block_rows(b), :], yf_buf.at[slot], in_sems.at[1, slot]),
                pltpu.make_async_copy(x_hbm.at[block_rows(b), :], x_buf.at[slot], in_sems.at[2, slot]))

    def out_copy(b, slot):
        return pltpu.make_async_copy(o_buf.at[slot], o_hbm.at[block_rows(b), :], out_sems.at[slot])

    def group_in(i, start):
        half = (i % 2) * grp
        for u in range(grp):
            b = 1 + grp * i + u
            b = min(b, n_blocks - 1) if isinstance(b, int) else jnp.minimum(b, n_blocks - 1)
            for c in in_copies(b, half + u):
                c.start() if start else c.wait()

    def matmul_residual(slot):
        y = jnp.dot(ym_buf[slot], wb_ref[0:d_m, :], preferred_element_type=F32)
        y = y + jnp.dot(yf_buf[slot], wb_ref[d_m:, :], preferred_element_type=F32)
        return x_buf[slot] + y

    def finish(r, slot):
        if final_norm:
            ms = jnp.mean(r * r, axis=-1, keepdims=True)
            r = r * lax.rsqrt(ms + EPS) * g_ref[...]
        o_buf[slot] = r

    n_in = x_buf.shape[0]

    def w_copy(s):
        return pltpu.make_async_copy(w_hbm.at[block_rows(s), :], x_buf.at[s % n_in], in_sems.at[2, s % n_in])

    n_w = w_hbm.shape[0] // rows
    for s0 in range(0, n_w, n_in):
        batch = range(s0, min(s0 + n_in, n_w))
        for s in batch:
            w_copy(s).start()
        for s in batch:
            w_copy(s).wait()
            wb_ref[s * rows:(s + 1) * rows, :] = x_buf[s % n_in].astype(BF16)

    for c in in_copies(0, first_slot):
        c.start()
    group_in(0, True)
    o_buf[...] = jnp.zeros_like(o_buf)
    for s in range(2 * grp):
        out_copy(s, s).start()
    for c in in_copies(0, first_slot):
        c.wait()
    r_ref[...] = matmul_residual(first_slot)

    def group(i, _):
        half = (i % 2) * grp
        group_in(i + 1, True)
        group_in(i, False)
        for u in range(grp):
            out_copy(0, half + u).wait()
        pending = r_ref[...]
        for u in range(grp):
            r = matmul_residual(half + u)
            finish(pending, half + u)
            pending = r
        r_ref[...] = pending
        for u in range(grp):
            out_copy(grp * i + u, half + u).start()
        return 0

    lax.fori_loop(0, n_groups, group, 0)
    last_slot = (n_groups % 2) * grp
    out_copy(0, last_slot).wait()
    finish(r_ref[...], last_slot)
    out_copy(n_blocks - 1, last_slot).start()
    group_in(n_groups, False)
    for s in range(2 * grp):
        out_copy(0, s).wait()


def _out_proj(ym, yf, x2, w_out, gain, final_norm):
    t_rows, d = x2.shape
    rows, grp = OUT_PROJ_SUB, OUT_PROJ_GROUP
    assert t_rows % rows == 0 and (t_rows // rows - 1) % grp == 0 and t_rows // rows > grp
    n_in = 2 * grp + 1
    assert w_out.shape[0] % rows == 0 and w_out.shape[1] == d
    hbm = pl.BlockSpec(memory_space=pl.ANY)
    const = lambda shape: pl.BlockSpec(shape, lambda t: (0, 0), pipeline_mode=pl.Buffered(1))
    kern = functools.partial(_out_proj_kernel, final_norm=final_norm)
    return pl.pallas_call(
        kern,
        grid=(1,),
        in_specs=[hbm, hbm, hbm, hbm, const((1, d))],
        out_specs=hbm,
        out_shape=jax.ShapeDtypeStruct((t_rows, d), F32),
        scratch_shapes=[pltpu.VMEM(w_out.shape, BF16),
                        pltpu.VMEM((n_in, rows, ym.shape[1]), BF16),
                        pltpu.VMEM((n_in, rows, yf.shape[1]), BF16),
                        pltpu.VMEM((n_in, rows, d), F32),
                        pltpu.VMEM((2 * grp, rows, d), F32),
                        pltpu.VMEM((rows, d), F32),
                        pltpu.SemaphoreType.DMA((3, n_in)), pltpu.SemaphoreType.DMA((2 * grp,))],
        compiler_params=pltpu.CompilerParams(
            dimension_semantics=("arbitrary",), vmem_limit_bytes=VMEM_LIMIT),
        name="out_proj",
    )(ym, yf, x2, w_out, gain)


def _block_diag(w, dh):
    nb, bs, _ = w.shape
    d = nb * bs
    tiled = jnp.broadcast_to(w.reshape(d, 1, bs), (d, dh // bs, bs)).reshape(d, dh)
    r = lax.broadcasted_iota(jnp.int32, (d, dh), 0)
    c = lax.broadcasted_iota(jnp.int32, (d, dh), 1)
    keep = (r % dh) // bs == c // bs
    return jnp.where(keep, tiled, 0.0).reshape(d // dh, dh, dh)


def kernel(x, norm_gain, w_in, conv_w, conv_b, w_q_m, w_k_m, w_v_m, w_igate, b_igate, w_fgate,
           b_fgate, mlstm_norm_gain, mlstm_skip, fox_forget_bias, fox_norm_gain, w_out,
           final_norm_gain):
    b, s, d = x.shape
    depth = norm_gain.shape[0]
    d_m = conv_w.shape[-1]
    d_f = fox_norm_gain.shape[-1]
    dh_m = d_m // N_MLSTM_HEADS
    dh_f = d_f // N_FOX_HEADS
    n_main = 2 * d_m + 4 * d_f
    assert d_m == d and d_f == d

    x2 = x.reshape(b * s, d)
    for l in range(depth):
        assert w_in.shape[2] == n_main + N_FOX_HEADS
        f_bias = jnp.pad(fox_forget_bias[l], (0, LANES - N_FOX_HEADS)).reshape(1, LANES)
        xm, gm, qf, kf, vf, gf, c_keys = _in_proj(
            x2, norm_gain[l].reshape(1, d), w_in[l].T, f_bias, s, dh_f ** -0.5 * LOG2E)

        lane_pad = ((0, 0), (0, LANES - N_MLSTM_HEADS))
        wg = jnp.concatenate([jnp.pad(w_igate[l], lane_pad), jnp.pad(w_fgate[l], lane_pad)], axis=1)
        gb = jnp.concatenate([jnp.pad(b_igate[l][None], lane_pad),
                              jnp.pad(b_fgate[l][None], lane_pad)], axis=1)
        y_m = _mlstm(xm.reshape(b, s, d_m), gm.reshape(b, s, d_m), conv_w[l],
                     conv_b[l].reshape(1, d_m),
                     _block_diag(w_q_m[l], dh_m).astype(BF16),
                     _block_diag(w_k_m[l], dh_m).astype(BF16),
                     _block_diag(w_v_m[l], dh_m).astype(BF16),
                     wg.astype(BF16), gb,
                     mlstm_norm_gain[l].reshape(1, d_m), mlstm_skip[l].reshape(1, d_m))

        y_f = _fox(qf.reshape(b, s, d_f), kf.reshape(b, s, d_f), vf.reshape(b, s, d_f), c_keys,
                   gf.reshape(b, s, d_f), fox_norm_gain[l].reshape(1, d_f))

        x2 = _out_proj(y_m.reshape(b * s, d_m), y_f.reshape(b * s, d_f), x2, w_out[l],
                       final_norm_gain.reshape(1, d), final_norm=(l == depth - 1))
    return x2.reshape(b, s, d)
```
